```python
import math
import jax, jax.numpy as jnp
from jax import lax
import numpy as np

D_MODEL = 1024
BATCH = 16
SEQ = 4096
DEPTH = 2

D_MIX = D_MODEL
SSD_HEADS = 8
SSD_HEAD_DIM = 64
SSD_WIDTH = SSD_HEADS * SSD_HEAD_DIM
SSD_GROUPS = 2
SSD_STATE = 128
SSD_CONV = 4
SSD_CHUNK = 128
SSD_CONV_DIM = SSD_WIDTH + 2 * SSD_GROUPS * SSD_STATE
MLA_HEADS = 4
MLA_Q_LORA = 256
MLA_KV_LORA = 128
MLA_NOPE = 64
MLA_ROPE = 32
MLA_V = 64
MLA_QK = MLA_NOPE + MLA_ROPE
MLA_WIDTH = MLA_HEADS * MLA_V
ROPE_BASE = 10000.0
MOBA_HEADS = 4
MOBA_HEAD_DIM = 64
MOBA_WIDTH = MOBA_HEADS * MOBA_HEAD_DIM
MOBA_BLOCK = 256
MOBA_TOPK = 3
Q_BLOCK = 128
REL_BUCKETS = 32
REL_MAX_DIST = 128
N_EXPERTS = 16
N_EXPERT_GROUPS = 4
EXPERTS_PER_GROUP = N_EXPERTS // N_EXPERT_GROUPS
TOPK_GROUPS = 1
TOPK_EXPERTS = 2
D_EXPERT = 512
EPS = 1e-6
IN_SIZES = (SSD_WIDTH, SSD_CONV_DIM, SSD_HEADS, MLA_Q_LORA, MLA_KV_LORA, MLA_ROPE, MOBA_WIDTH, MOBA_WIDTH, MOBA_WIDTH)
N_IN = sum(IN_SIZES)

kernel_name = "hybrid_ssd_mla_moba_moe_block"


def _rms(x):
    xf = x.astype(jnp.float32)
    return (xf * lax.rsqrt(jnp.mean(xf * xf, axis=-1, keepdims=True) + EPS)).astype(x.dtype)


def rms_norm(x, g):
    return _rms(x) * g


def rope(x):
    s, d = x.shape[1], x.shape[-1]
    half = d // 2
    inv = ROPE_BASE ** (-2.0 * jnp.arange(half, dtype=jnp.float32) / d)
    ang = jnp.arange(s, dtype=jnp.float32)[:, None] * inv[None, :]
    cos = jnp.cos(ang)[None, :, None, :]
    sin = jnp.sin(ang)[None, :, None, :]
    x1, x2 = x[..., :half], x[..., half:]
    return jnp.concatenate([x1 * cos - x2 * sin, x1 * sin + x2 * cos], axis=-1).astype(x.dtype)


def rel_bucket(n):
    max_exact = REL_BUCKETS // 2
    nf = jnp.maximum(n, max_exact).astype(jnp.float32)
    large = max_exact + (jnp.log(nf / max_exact) / math.log(REL_MAX_DIST / max_exact)
                         * (REL_BUCKETS - max_exact)).astype(jnp.int32)
    large = jnp.minimum(large, REL_BUCKETS - 1)
    return jnp.where(n < max_exact, n, large)


def causal_depthwise_conv(x, w, b):
    k, ch = w.shape
    y = lax.conv_general_dilated(x, w[:, None, :], window_strides=(1,), padding=[(k - 1, 0)],
                                 dimension_numbers=("NWC", "WIO", "NWC"), feature_group_count=ch)
    return y + b


def ssd_mixer(z, xbc, dt_raw, conv_w, conv_b, dt_bias, a_log, d_skip, norm_g):
    bsz, s, _ = z.shape
    g, r, p, n, L = SSD_GROUPS, SSD_HEADS // SSD_GROUPS, SSD_HEAD_DIM, SSD_STATE, SSD_CHUNK
    nc = s // L
    xbc = jax.nn.silu(causal_depthwise_conv(xbc, conv_w, conv_b))
    xs = xbc[..., :SSD_WIDTH].reshape(bsz, nc, L, g, r, p)
    bm = xbc[..., SSD_WIDTH:SSD_WIDTH + g * n].reshape(bsz, nc, L, g, n)
    cm = xbc[..., SSD_WIDTH + g * n:].reshape(bsz, nc, L, g, n)
    dt = jax.nn.softplus((dt_raw + dt_bias).astype(jnp.float32)).reshape(bsz, nc, L, g, r)
    a = -jnp.exp(a_log.astype(jnp.float32)).reshape(g, r)
    a_cum = jnp.cumsum(dt * a, axis=2)
    xdt = xs * dt[..., None]
    causal = jnp.tril(jnp.ones((L, L), dtype=bool))[None, None, :, :, None, None]
    seg = a_cum[:, :, :, None] - a_cum[:, :, None, :]
    decay = jnp.exp(jnp.where(causal, seg, -jnp.inf))
    cb = jnp.einsum("bclgn,bcsgn->bclsg", cm, bm)
    y_diag = jnp.einsum("bclsgr,bcsgrp->bclgrp", cb[..., None] * decay, xdt)
    to_end = jnp.exp(a_cum[:, :, -1:] - a_cum)
    states = jnp.einsum("bclgn,bclgrp->bcgrpn", bm, xdt * to_end[..., None])
    chunk_decay = jnp.exp(a_cum[:, :, -1])

    def step(h, inp):
        st, dc = inp
        return h * dc[..., None, None] + st, h

    h0 = jnp.zeros((bsz, g, r, p, n), states.dtype)
    _, h_prev = lax.scan(step, h0, (jnp.moveaxis(states, 1, 0), jnp.moveaxis(chunk_decay, 1, 0)))
    h_prev = jnp.moveaxis(h_prev, 0, 1)
    y_off = jnp.einsum("bclgn,bcgrpn->bclgrp", cm, h_prev) * jnp.exp(a_cum)[..., None]
    y = y_diag + y_off + xs * d_skip.reshape(g, r)[:, :, None]
    y = y.reshape(bsz, s, SSD_WIDTH).astype(z.dtype) * jax.nn.silu(z)
    y = _rms(y.reshape(bsz, s, g, SSD_WIDTH // g)).reshape(bsz, s, SSD_WIDTH)
    return y * norm_g


def dense_causal_attention(q, k, v, scale):
    bsz, s, h, dk = q.shape
    nq = s // Q_BLOCK
    qb = jnp.moveaxis(q.reshape(bsz, nq, Q_BLOCK, h, dk), 1, 0)
    kpos = jnp.arange(s)

    def one_block(args):
        qi, i = args
        logits = jnp.einsum("bqhd,bkhd->bhqk", qi, k).astype(jnp.float32) * scale
        qpos = i * Q_BLOCK + jnp.arange(Q_BLOCK)
        logits = jnp.where(kpos[None, :] <= qpos[:, None], logits, -jnp.inf)
        pr = jax.nn.softmax(logits, axis=-1).astype(v.dtype)
        return jnp.einsum("bhqk,bkhd->bqhd", pr, v)

    out = lax.map(one_block, (qb, jnp.arange(nq)))
    return jnp.moveaxis(out, 0, 1).reshape(bsz, s, h * v.shape[-1])


def mla_mixer(cq, ckv, krope, q_norm_g, w_uq, kv_norm_g, w_ukv, q_gain, k_gain):
    bsz, s, _ = cq.shape
    q = (rms_norm(cq, q_norm_g) @ w_uq).reshape(bsz, s, MLA_HEADS, MLA_QK)
    kv = (rms_norm(ckv, kv_norm_g) @ w_ukv).reshape(bsz, s, MLA_HEADS, MLA_NOPE + MLA_V)
    k_nope, v = kv[..., :MLA_NOPE], kv[..., MLA_NOPE:]
    q = jnp.concatenate([q[..., :MLA_NOPE], rope(q[..., MLA_NOPE:])], axis=-1)
    k_pe = jnp.broadcast_to(rope(krope[:, :, None, :]), (bsz, s, MLA_HEADS, MLA_ROPE))
    k = jnp.concatenate([k_nope, k_pe], axis=-1)
    q = rms_norm(q, q_gain)
    k = rms_norm(k, k_gain)
    return dense_causal_attention(q, k, v, MLA_QK ** -0.5)


def moba_mixer(q, k, v, q_gain, k_gain, rel_bias):
    bsz, s, _ = q.shape
    h, d, blk = MOBA_HEADS, MOBA_HEAD_DIM, MOBA_BLOCK
    q = rms_norm(q.reshape(bsz, s, h, d), q_gain)
    k = rms_norm(k.reshape(bsz, s, h, d), k_gain)
    v = v.reshape(bsz, s, h, d)
    nb = -(-s // blk)
    pad = nb * blk - s
    kblk = jnp.pad(k, ((0, 0), (0, pad), (0, 0), (0, 0))).reshape(bsz, nb, blk, h, d)
    vblk = jnp.pad(v, ((0, 0), (0, pad), (0, 0), (0, 0))).reshape(bsz, nb, blk, h, d)
    kmean = jnp.mean(kblk, axis=2)
    nq = s // Q_BLOCK
    qb = q.reshape(bsz, nq, Q_BLOCK, h, d)
    k_sel = min(MOBA_TOPK, nb)
    scale = d ** -0.5
    bias_t = rel_bias.T
    hidx3 = jnp.arange(h)[None, :, None]
    hidx4 = jnp.arange(h)[None, :, None, None]
    offs = jnp.arange(blk)

    def per_batch(args):
        qb_b, kblk_b, vblk_b, kmean_b = args
        k_by_head = jnp.moveaxis(kblk_b, 2, 0)
        v_by_head = jnp.moveaxis(vblk_b, 2, 0)

        def per_chunk(args2):
            qi, ci = args2
            qpos = ci * Q_BLOCK + jnp.arange(Q_BLOCK)
            own = (ci * Q_BLOCK) // blk
            gate = jnp.einsum("qhd,nhd->qhn", qi, kmean_b).astype(jnp.float32)
            gate = jnp.where((jnp.arange(nb) < own)[None, None, :], gate, -jnp.inf)
            top_val, top_idx = lax.top_k(gate, k_sel)
            valid = jnp.isfinite(top_val)
            gk = k_by_head[hidx3, top_idx]
            gv = v_by_head[hidx3, top_idx]
            sel = jnp.einsum("qhd,qhksd->qhks", qi, gk).astype(jnp.float32) * scale
            sel_dist = jnp.maximum(qpos[:, None, None, None] - (top_idx[..., None] * blk + offs), 0)
            sel = sel + bias_t[hidx4, rel_bucket(sel_dist)]
            sel = jnp.where(valid[..., None], sel, -jnp.inf)
            own_k = lax.dynamic_index_in_dim(kblk_b, own, axis=0, keepdims=False)
            own_v = lax.dynamic_index_in_dim(vblk_b, own, axis=0, keepdims=False)
            own_l = jnp.einsum("qhd,shd->qhs", qi, own_k).astype(jnp.float32) * scale
            dist = qpos[:, None] - (own * blk + offs)[None, :]
            own_l = own_l + jnp.moveaxis(bias_t[:, rel_bucket(jnp.maximum(dist, 0))], 0, 1)
            own_l = jnp.where((dist >= 0)[:, None, :], own_l, -jnp.inf)
            logits = jnp.concatenate([sel.reshape(Q_BLOCK, h, k_sel * blk), own_l], axis=-1)
            pr = jax.nn.softmax(logits, axis=-1).astype(v.dtype)
            p_sel = pr[..., :k_sel * blk].reshape(Q_BLOCK, h, k_sel, blk)
            p_own = pr[..., k_sel * blk:]
            return jnp.einsum("qhks,qhksd->qhd", p_sel, gv) + jnp.einsum("qhs,shd->qhd", p_own, own_v)

        return lax.map(per_chunk, (qb_b, jnp.arange(nq)))

    out = lax.map(per_batch, (qb, kblk, vblk, kmean))
    return out.reshape(bsz, s, h * d)


def moe_ffn(x, router_w, router_bias, w_gate, w_up, w_down):
    bsz, s, dm = x.shape
    t = x.reshape(-1, dm)
    scores = jax.nn.sigmoid((t @ router_w).astype(jnp.float32))
    biased = scores + router_bias.astype(jnp.float32)
    grp = biased.reshape(-1, N_EXPERT_GROUPS, EXPERTS_PER_GROUP)
    group_score = jnp.sum(lax.top_k(grp, 2)[0], axis=-1)
    _, gidx = lax.top_k(group_score, TOPK_GROUPS)
    gmask = jnp.sum(jax.nn.one_hot(gidx, N_EXPERT_GROUPS, dtype=jnp.float32), axis=-2) > 0
    emask = jnp.repeat(gmask, EXPERTS_PER_GROUP, axis=-1)
    _, eidx = lax.top_k(jnp.where(emask, biased, -jnp.inf), TOPK_EXPERTS)
    w = jnp.take_along_axis(scores, eidx, axis=-1)
    w = w / jnp.sum(w, axis=-1, keepdims=True)
    gates = jnp.sum(jax.nn.one_hot(eidx, N_EXPERTS, dtype=jnp.float32) * w[..., None], axis=-2).astype(t.dtype)
    out = jnp.zeros_like(t)
    for e in range(N_EXPERTS):
        hid = jax.nn.silu(t @ w_gate[e]) * (t @ w_up[e])
        out = out + gates[:, e:e + 1] * (hid @ w_down[e])
    return out.reshape(bsz, s, dm)


def setup_inputs(seed: int = 0) -> dict:
    key = jax.random.key(seed)
    ks = jax.random.split(key, 32)
    f32 = jnp.float32

    def nrm(k, shape, scale):
        return jax.random.normal(k, shape, f32) * scale

    def gain(k, shape):
        return 1.0 + 0.1 * jax.random.normal(k, shape, f32)

    dt = jnp.exp(jax.random.uniform(ks[9], (DEPTH, SSD_HEADS), f32, math.log(1e-3), math.log(1e-1)))
    return {
        "x": nrm(ks[0], (BATCH, SEQ, D_MODEL), 1.0),
        "c": nrm(ks[1], (BATCH, D_MODEL), 1.0),
        "ada_w": nrm(ks[2], (DEPTH, D_MODEL, 6 * D_MODEL), 0.5 * D_MODEL ** -0.5),
        "ada_b": nrm(ks[3], (DEPTH, 6 * D_MODEL), 0.02),
        "norm_mix_g": gain(ks[4], (DEPTH, D_MODEL)),
        "norm_ffn_g": gain(ks[5], (DEPTH, D_MODEL)),
        "w_in": nrm(ks[6], (DEPTH, D_MODEL, N_IN), D_MODEL ** -0.5),
        "ssd_conv_w": nrm(ks[7], (DEPTH, SSD_CONV, SSD_CONV_DIM), SSD_CONV ** -0.5),
        "ssd_conv_b": nrm(ks[8], (DEPTH, SSD_CONV_DIM), 0.02),
        "ssd_dt_bias": dt + jnp.log(-jnp.expm1(-dt)),
        "ssd_a_log": jnp.log(jax.random.uniform(ks[10], (DEPTH, SSD_HEADS), f32, 1.0, 16.0)),
        "ssd_d": gain(ks[11], (DEPTH, SSD_HEADS)),
        "ssd_norm_g": gain(ks[12], (DEPTH, SSD_WIDTH)),
        "mla_q_norm_g": gain(ks[13], (DEPTH, MLA_Q_LORA)),
        "mla_w_uq": nrm(ks[14], (DEPTH, MLA_Q_LORA, MLA_HEADS * MLA_QK), MLA_Q_LORA ** -0.5),
        "mla_kv_norm_g": gain(ks[15], (DEPTH, MLA_KV_LORA)),
        "mla_w_ukv": nrm(ks[16], (DEPTH, MLA_KV_LORA, MLA_HEADS * (MLA_NOPE + MLA_V)), MLA_KV_LORA ** -0.5),
        "mla_q_gain": gain(ks[17], (DEPTH, MLA_QK)),
        "mla_k_gain": gain(ks[18], (DEPTH, MLA_QK)),
        "mla_out_g": gain(ks[19], (DEPTH, MLA_WIDTH)),
        "moba_q_gain": gain(ks[20], (DEPTH, MOBA_HEAD_DIM)),
        "moba_k_gain": gain(ks[21], (DEPTH, MOBA_HEAD_DIM)),
        "moba_out_g": gain(ks[22], (DEPTH, MOBA_WIDTH)),
        "rel_bias": nrm(ks[23], (REL_BUCKETS, MOBA_HEADS), 0.5),
        "w_out": nrm(ks[24], (DEPTH, D_MIX, D_MODEL), D_MIX ** -0.5),
        "router_w": nrm(ks[25], (D_MODEL, N_EXPERTS), D_MODEL ** -0.5),
        "router_bias": nrm(ks[26], (N_EXPERTS,), 0.01),
        "moe_w_gate": nrm(ks[27], (DEPTH, N_EXPERTS, D_MODEL, D_EXPERT), D_MODEL ** -0.5),
        "moe_w_up": nrm(ks[28], (DEPTH, N_EXPERTS, D_MODEL, D_EXPERT), D_MODEL ** -0.5),
        "moe_w_down": nrm(ks[29], (DEPTH, N_EXPERTS, D_EXPERT, D_MODEL), D_EXPERT ** -0.5),
    }


def reference(x, c, ada_w, ada_b, norm_mix_g, norm_ffn_g, w_in, ssd_conv_w, ssd_conv_b, ssd_dt_bias,
              ssd_a_log, ssd_d, ssd_norm_g, mla_q_norm_g, mla_w_uq, mla_kv_norm_g, mla_w_ukv, mla_q_gain,
              mla_k_gain, mla_out_g, moba_q_gain, moba_k_gain, moba_out_g, rel_bias, w_out, router_w,
              router_bias, moe_w_gate, moe_w_up, moe_w_down):
    split_at = np.cumsum(IN_SIZES)[:-1].tolist()
    c_act = jax.nn.silu(c)
    for l in range(DEPTH):
        mod = (c_act @ ada_w[l] + ada_b[l])[:, None, :]
        sh_m, sc_m, g_m, sh_f, sc_f, g_f = jnp.split(mod, 6, axis=-1)
        hm = rms_norm(x, norm_mix_g[l]) * (1.0 + sc_m) + sh_m
        proj = hm @ w_in[l]
        z, xbc, dt_raw, cq, ckv, krope, mq, mk, mv = jnp.split(proj, split_at, axis=-1)
        y_ssd = ssd_mixer(z, xbc, dt_raw, ssd_conv_w[l], ssd_conv_b[l], ssd_dt_bias[l], ssd_a_log[l],
                          ssd_d[l], ssd_norm_g[l])
        y_mla = rms_norm(mla_mixer(cq, ckv, krope, mla_q_norm_g[l], mla_w_uq[l], mla_kv_norm_g[l],
                                   mla_w_ukv[l], mla_q_gain[l], mla_k_gain[l]), mla_out_g[l])
        y_moba = rms_norm(moba_mixer(mq, mk, mv, moba_q_gain[l], moba_k_gain[l], rel_bias), moba_out_g[l])
        y = jnp.concatenate([y_ssd, y_mla, y_moba], axis=-1) @ w_out[l]
        x = x + g_m * y
        hf = rms_norm(x, norm_ffn_g[l]) * (1.0 + sc_f) + sh_f
        x = x + g_f * moe_ffn(hf, router_w, router_bias, moe_w_gate[l], moe_w_up[l], moe_w_down[l])
    return x
```

```python
import functools
import math

import numpy as np
import jax
import jax.numpy as jnp
from jax import lax
from jax.experimental import pallas as pl
from jax.experimental.pallas import tpu as pltpu

D_MODEL = 1024
DEPTH = 2
SSD_HEADS = 8
SSD_HEAD_DIM = 64
SSD_WIDTH = SSD_HEADS * SSD_HEAD_DIM
SSD_GROUPS = 2
SSD_STATE = 128
SSD_CONV = 4
SSD_CHUNK = 128
SSD_CONV_DIM = SSD_WIDTH + 2 * SSD_GROUPS * SSD_STATE
MLA_HEADS = 4
MLA_Q_LORA = 256
MLA_KV_LORA = 128
MLA_NOPE = 64
MLA_ROPE = 32
MLA_V = 64
MLA_QK = MLA_NOPE + MLA_ROPE
ROPE_BASE = 10000.0
MOBA_HEADS = 4
MOBA_HEAD_DIM = 64
MOBA_BLOCK = 256
MOBA_TOPK = 3
REL_BUCKETS = 32
REL_MAX_DIST = 128
N_EXPERTS = 16
N_EXPERT_GROUPS = 4
EXPERTS_PER_GROUP = 4
D_EXPERT = 512
EPS = 1e-6

LANES = 128
HEAD_PAD = 128
ATT_BLOCK = 256
VMEM_LIMIT = 56 * 1024 * 1024
MASK_NEG = -1e9
N_PAIRS = 6
N_BUCKETS = N_EXPERT_GROUPS * N_PAIRS
MOE_TILE = 256
PAIR_LO = (0, 0, 0, 1, 1, 2)
PAIR_HI = (1, 2, 3, 2, 3, 3)

COL_Z = 0
COL_XBC = COL_Z + SSD_WIDTH
COL_DT = COL_XBC + SSD_CONV_DIM
COL_MLA = COL_DT + LANES
MLA_IN_W = MLA_Q_LORA + MLA_KV_LORA + LANES
COL_MOBA = COL_MLA + MLA_IN_W
MOBA_IN_W = 2 * MOBA_HEADS * HEAD_PAD + MOBA_HEADS * MOBA_HEAD_DIM
N_IN_PAD = COL_MOBA + MOBA_IN_W

F32 = jnp.float32
BF16 = jnp.bfloat16


def _params(*sem):
    return pltpu.CompilerParams(dimension_semantics=sem, vmem_limit_bytes=VMEM_LIMIT)


def _dot(a, b):
    return jnp.dot(a, b, preferred_element_type=F32)


def _dot_nt(a, b):
    return lax.dot_general(a, b, (((1,), (1,)), ((), ())), preferred_element_type=F32)


def _dot_exact(a, b):
    return jnp.dot(a, b, preferred_element_type=F32, precision=lax.Precision.HIGHEST)


def _silu(v):
    return v * (1.0 / (1.0 + jnp.exp(-v)))


def _sigmoid(v):
    return 1.0 / (1.0 + jnp.exp(-v))


def _softplus(v):
    return jnp.maximum(v, 0.0) + jnp.log1p(jnp.exp(-jnp.abs(v)))


def _ada_kernel(c_ref, w_ref, b_ref, o_ref):
    c = c_ref[...]
    o_ref[0] = _dot_exact(_silu(c), w_ref[0]) + b_ref[0]


def _modulation(c, ada_w, ada_b):
    depth, d, n = ada_w.shape
    bsz = c.shape[0]
    tn = d
    return pl.pallas_call(
        _ada_kernel,
        grid=(depth, n // tn),
        in_specs=[pl.BlockSpec((bsz, d), lambda l, j: (0, 0)),
                  pl.BlockSpec((1, d, tn), lambda l, j: (l, 0, j)),
                  pl.BlockSpec((1, 1, tn), lambda l, j: (l, 0, j))],
        out_specs=pl.BlockSpec((1, bsz, tn), lambda l, j: (l, 0, j)),
        out_shape=jax.ShapeDtypeStruct((depth, bsz, n), F32),
        compiler_params=_params("arbitrary", "arbitrary"),
        name="ada_modulation",
    )(c, ada_w, ada_b.reshape(depth, 1, n))


def _inproj_body(x, g_ref, sc_ref, sh_ref, w_ref, z_ref, xbc_ref, dt_ref, mla_ref, moba_ref):
    ms = jnp.mean(x * x, axis=-1, keepdims=True)
    hm = (x * lax.rsqrt(ms + EPS) * g_ref[...]) * (1.0 + sc_ref[0]) + sh_ref[0]
    hb = hm.astype(BF16)
    z_ref[0] = _dot(hb, w_ref[:, COL_Z:COL_XBC]).astype(z_ref.dtype)
    xbc_ref[0] = _dot(hb, w_ref[:, COL_XBC:COL_DT]).astype(xbc_ref.dtype)
    dt_ref[0] = _dot(hb, w_ref[:, COL_DT:COL_MLA])
    mla_ref[0] = _dot(hb, w_ref[:, COL_MLA:COL_MOBA]).astype(mla_ref.dtype)
    moba_ref[0] = _dot(hb, w_ref[:, COL_MOBA:N_IN_PAD]).astype(moba_ref.dtype)


def _inproj_kernel(x_ref, g_ref, sc_ref, sh_ref, w_ref, *outs):
    _inproj_body(x_ref[0], g_ref, sc_ref, sh_ref, w_ref, *outs)


def _inproj_res_kernel(x_ref, moe_ref, gf_ref, g_ref, sc_ref, sh_ref, w_ref, xo_ref, *outs):
    x = x_ref[0] + gf_ref[0] * moe_ref[0].astype(F32)
    xo_ref[0] = x
    _inproj_body(x, g_ref, sc_ref, sh_ref, w_ref, *outs)


def _inproj(x, moe_prev, gf_prev, g, sc, sh, w_pad, tm):
    bsz, s, d = x.shape
    tok = lambda w: pl.BlockSpec((1, tm, w), lambda b, i: (b, i, 0))
    per_b = pl.BlockSpec((1, 1, d), lambda b, i: (b, 0, 0))
    out_shapes = [jax.ShapeDtypeStruct((bsz, s, SSD_WIDTH), BF16),
                  jax.ShapeDtypeStruct((bsz, s, SSD_CONV_DIM), BF16),
                  jax.ShapeDtypeStruct((bsz, s, LANES), F32),
                  jax.ShapeDtypeStruct((bsz, s, MLA_IN_W), BF16),
                  jax.ShapeDtypeStruct((bsz, s, MOBA_IN_W), BF16)]
    out_specs = [tok(SSD_WIDTH), tok(SSD_CONV_DIM), tok(LANES), tok(MLA_IN_W), tok(MOBA_IN_W)]
    common_specs = [pl.BlockSpec((1, d), lambda b, i: (0, 0)), per_b, per_b,
                    pl.BlockSpec((d, N_IN_PAD), lambda b, i: (0, 0))]
    common = (g.reshape(1, d), sc, sh, w_pad)
    if moe_prev is None:
        outs = pl.pallas_call(
            _inproj_kernel, grid=(bsz, s // tm),
            in_specs=[tok(d)] + common_specs, out_specs=out_specs, out_shape=out_shapes,
            compiler_params=_params("arbitrary", "arbitrary"), name="in_proj",
        )(x, *common)
        return x, outs
    outs = pl.pallas_call(
        _inproj_res_kernel, grid=(bsz, s // tm),
        in_specs=[tok(d), tok(d), per_b] + common_specs,
        out_specs=[tok(d)] + out_specs,
        out_shape=[jax.ShapeDtypeStruct((bsz, s, d), F32)] + out_shapes,
        compiler_params=_params("arbitrary", "arbitrary"), name="in_proj_res",
    )(x, moe_prev, gf_prev, *common)
    return outs[0], outs[1:]


SSD_STEP_CHUNKS = 2
CONV_HALO = 8


def _ssd_kernel(z_ref, xbc_ref, dt_ref, cw_ref, cb_ref, dtb_ref, alog_ref, dskip_ref, ng_ref,
                expand_ref, o_ref, state_ref, ext_ref):
    L, G, N, P = SSD_CHUNK, SSD_GROUPS, SSD_STATE, SSD_HEAD_DIM
    R = SSD_HEADS // G
    GW = R * P
    first = pl.program_id(1) == 0

    @pl.when(first)
    def _():
        state_ref[...] = jnp.zeros_like(state_ref)
        ext_ref[0:CONV_HALO, :] = jnp.zeros((CONV_HALO, SSD_CONV_DIM), F32)

    row = lax.broadcasted_iota(jnp.int32, (L, L), 0)
    col = lax.broadcasted_iota(jnp.int32, (L, L), 1)
    causal = col <= row
    tril = causal.astype(F32)
    a_row = -jnp.exp(alog_ref[...])

    for ck in range(SSD_STEP_CHUNKS):
        r0 = ck * L
        ext_ref[CONV_HALO:CONV_HALO + L, :] = xbc_ref[0, r0:r0 + L, :].astype(F32)
        conv = cb_ref[...]
        for k in range(SSD_CONV):
            off = CONV_HALO - (SSD_CONV - 1) + k
            conv = conv + cw_ref[k:k + 1, :] * ext_ref[off:off + L, :]
        ext_ref[0:CONV_HALO, :] = ext_ref[L:L + CONV_HALO, :]
        xc = _silu(conv)
        xs = xc[:, :SSD_WIDTH]

        dt = _softplus(dt_ref[0, r0:r0 + L, :] + dtb_ref[...])
        a_cum = _dot_exact(tril, dt * a_row)
        a_cum_t = a_cum.T
        a_last = a_cum[L - 1:L, :]
        stacked = jnp.concatenate(
            [dt, jnp.exp(a_cum), jnp.exp(a_last - a_cum),
             jnp.broadcast_to(jnp.exp(a_last), (8, LANES))], axis=0)
        wide = _dot_exact(stacked, expand_ref[...])
        dt_w, ea_w, te_w, cd_w = wide[0:L], wide[L:2 * L], wide[2 * L:3 * L], wide[3 * L:3 * L + 1]
        xdt = xs * dt_w
        xdt_b = xdt.astype(BF16)
        xw_b = (xdt * te_w).astype(BF16)

        y_parts = []
        for g in range(G):
            bm = xc[:, SSD_WIDTH + g * N:SSD_WIDTH + (g + 1) * N]
            cm = xc[:, SSD_WIDTH + (G + g) * N:SSD_WIDTH + (G + g + 1) * N]
            bm_b, cm_b = bm.astype(BF16), cm.astype(BF16)
            cb = _dot_nt(cm_b, bm_b)
            st = state_ref[g]
            y_off = _dot(cm_b, st.astype(BF16)) * ea_w[:, g * GW:(g + 1) * GW]
            diag = []
            for r in range(R):
                h = g * R + r
                seg = a_cum[:, h:h + 1] - a_cum_t[h:h + 1, :]
                m = (cb * jnp.exp(jnp.where(causal, seg, -jnp.inf))).astype(BF16)
                diag.append(_dot(m, xdt_b[:, h * P:(h + 1) * P]))
            y_parts.append(jnp.concatenate(diag, axis=1) + y_off)
            state_ref[g] = st * cd_w[:, g * GW:(g + 1) * GW] + _dot(
                bm.T.astype(BF16), xw_b[:, g * GW:(g + 1) * GW])

        y = jnp.concatenate(y_parts, axis=1) + xs * dskip_ref[...]
        y = y * _silu(z_ref[0, r0:r0 + L, :].astype(F32))
        outs = []
        for g in range(G):
            yg = y[:, g * GW:(g + 1) * GW]
            outs.append(yg * lax.rsqrt(jnp.mean(yg * yg, axis=-1, keepdims=True) + EPS))
        o_ref[0, r0:r0 + L, :] = (jnp.concatenate(outs, axis=1) * ng_ref[...]).astype(o_ref.dtype)


def _ssd(z, xbc, dt, conv_w, conv_b, dt_bias, a_log, d_skip, norm_g):
    bsz, s, _ = z.shape
    ts = SSD_STEP_CHUNKS * SSD_CHUNK
    pad_h = lambda v: jnp.pad(v.astype(F32), (0, LANES - SSD_HEADS)).reshape(1, LANES)
    expand = jnp.repeat(jnp.eye(LANES, SSD_HEADS, dtype=F32), SSD_HEAD_DIM, axis=1)
    tok = lambda w: pl.BlockSpec((1, ts, w), lambda b, i: (b, i, 0))
    full = lambda a: pl.BlockSpec(a.shape, lambda b, i: (0,) * a.ndim)
    params = (conv_w, conv_b.reshape(1, -1), pad_h(dt_bias), pad_h(a_log),
              jnp.repeat(d_skip, SSD_HEAD_DIM).reshape(1, -1), norm_g.reshape(1, -1), expand)
    return pl.pallas_call(
        _ssd_kernel, grid=(bsz, s // ts),
        in_specs=[tok(SSD_WIDTH), tok(SSD_CONV_DIM), tok(LANES)] + [full(p) for p in params],
        out_specs=tok(SSD_WIDTH),
        out_shape=jax.ShapeDtypeStruct((bsz, s, SSD_WIDTH), BF16),
        scratch_shapes=[pltpu.VMEM((SSD_GROUPS, SSD_STATE, SSD_WIDTH // SSD_GROUPS), F32),
                        pltpu.VMEM((CONV_HALO + SSD_CHUNK, SSD_CONV_DIM), F32)],
        compiler_params=_params("arbitrary", "arbitrary"), name="ssd_scan",
    )(z, xbc, dt, *params)


def _rope_mix(v, cos, sin_hi, sin_lo):
    w = v.shape[1]
    half = MLA_ROPE // 2
    return v * cos + pltpu.roll(v, half, axis=1) * sin_hi + pltpu.roll(v, w - half, axis=1) * sin_lo


def _mla_prep_kernel(in_ref, gq_ref, wq_ref, gkv_ref, wkv_ref, cos_ref, shi_ref, slo_ref,
                     qgain_ref, kgain_ref, q_ref, k_ref, v_ref):
    H = MLA_HEADS
    xin = in_ref[0].astype(F32)
    cq = xin[:, :MLA_Q_LORA]
    ckv = xin[:, MLA_Q_LORA:MLA_Q_LORA + MLA_KV_LORA]
    kpe = xin[:, MLA_Q_LORA + MLA_KV_LORA:]
    cqn = cq * lax.rsqrt(jnp.mean(cq * cq, axis=-1, keepdims=True) + EPS) * gq_ref[...]
    ckvn = ckv * lax.rsqrt(jnp.mean(ckv * ckv, axis=-1, keepdims=True) + EPS) * gkv_ref[...]
    q = _dot(cqn.astype(BF16), wq_ref[...])
    kv = _dot(ckvn.astype(BF16), wkv_ref[...])
    cos, shi, slo = cos_ref[...], shi_ref[...], slo_ref[...]
    tile = lambda t: jnp.concatenate([t] * H, axis=1)
    q = _rope_mix(q, tile(cos), tile(shi), tile(slo))
    kpe = _rope_mix(kpe, cos, shi, slo)
    scale = MLA_QK ** -0.5
    for h in range(H):
        sl = slice(h * HEAD_PAD, (h + 1) * HEAD_PAD)
        qh = q[:, sl]
        qn = qh * lax.rsqrt(jnp.sum(qh * qh, axis=-1, keepdims=True) / MLA_QK + EPS)
        q_ref[0, :, sl] = (qn * (qgain_ref[...] * scale)).astype(q_ref.dtype)
        kh = kv[:, sl] + kpe
        kn = kh * lax.rsqrt(jnp.sum(kh * kh, axis=-1, keepdims=True) / MLA_QK + EPS)
        k_ref[0, :, sl] = (kn * kgain_ref[...]).astype(k_ref.dtype)
    v_ref[0] = kv[:, H * HEAD_PAD:].astype(v_ref.dtype)


def _rope_tables(s):
    half = MLA_ROPE // 2
    inv = ROPE_BASE ** (-2.0 * jnp.arange(half, dtype=F32) / MLA_ROPE)
    ang = jnp.arange(s, dtype=F32)[:, None] * inv[None, :]
    cos, sin = jnp.cos(ang), jnp.sin(ang)
    zeros = lambda n: jnp.zeros((s, n), F32)
    cos_t = jnp.concatenate([jnp.ones((s, MLA_NOPE), F32), cos, cos, zeros(HEAD_PAD - MLA_QK)], axis=1)
    sin_hi = jnp.concatenate([zeros(MLA_NOPE + half), sin, zeros(HEAD_PAD - MLA_QK)], axis=1)
    sin_lo = jnp.concatenate([zeros(MLA_NOPE), -sin, zeros(half + HEAD_PAD - MLA_QK)], axis=1)
    return cos_t, sin_hi, sin_lo


def _pad_heads(w, heads, width):
    lead = w.shape[:-1]
    w = w.reshape(*lead, heads, width)
    w = jnp.pad(w, [(0, 0)] * len(lead) + [(0, 0), (0, HEAD_PAD - width)])
    return w.reshape(*lead, heads * HEAD_PAD)


def _mla_prep(mla_in, gq, w_uq, gkv, w_ukv, q_gain, k_gain, tm):
    bsz, s, _ = mla_in.shape
    H = MLA_HEADS
    wq = _pad_heads(w_uq, H, MLA_QK).astype(BF16)
    wkv = w_ukv.reshape(MLA_KV_LORA, H, MLA_NOPE + MLA_V)
    wk = _pad_heads(wkv[:, :, :MLA_NOPE].reshape(MLA_KV_LORA, H * MLA_NOPE), H, MLA_NOPE)
    wv = wkv[:, :, MLA_NOPE:].reshape(MLA_KV_LORA, H * MLA_V)
    wkv_p = jnp.concatenate([wk, wv], axis=1).astype(BF16)
    cos_t, sin_hi, sin_lo = _rope_tables(s)
    pad_g = lambda v: jnp.pad(v, (0, HEAD_PAD - MLA_QK)).reshape(1, HEAD_PAD)
    tok = lambda w: pl.BlockSpec((1, tm, w), lambda b, i: (b, i, 0))
    full = lambda a: pl.BlockSpec(a.shape, lambda b, i: (0,) * a.ndim)
    pos = pl.BlockSpec((tm, HEAD_PAD), lambda b, i: (i, 0))
    gq2, gkv2, qg, kg = gq.reshape(1, -1), gkv.reshape(1, -1), pad_g(q_gain), pad_g(k_gain)
    return pl.pallas_call(
        _mla_prep_kernel, grid=(bsz, s // tm),
        in_specs=[tok(MLA_IN_W), full(gq2), full(wq), full(gkv2), full(wkv_p), pos, pos, pos,
                  full(qg), full(kg)],
        out_specs=[tok(H * HEAD_PAD), tok(H * HEAD_PAD), tok(H * MLA_V)],
        out_shape=[jax.ShapeDtypeStruct((bsz, s, H * HEAD_PAD), BF16),
                   jax.ShapeDtypeStruct((bsz, s, H * HEAD_PAD), BF16),
                   jax.ShapeDtypeStruct((bsz, s, H * MLA_V), BF16)],
        compiler_params=_params("arbitrary", "arbitrary"), name="mla_prep",
    )(mla_in, gq2, wq, gkv2, wkv_p, cos_t, sin_hi, sin_lo, qg, kg)


GATE_LANE0 = MOBA_HEAD_DIM


def _moba_prep_kernel(in_ref, qgain_ref, kgain_ref, q_ref, k_ref, v_ref, km_ref):
    H, W = MOBA_HEADS, MOBA_HEADS * HEAD_PAD
    blk = pl.program_id(1)
    lane = lax.broadcasted_iota(jnp.int32, (1, HEAD_PAD), 1)
    tag = (lane == GATE_LANE0 + blk).astype(F32)
    scale = MOBA_HEAD_DIM ** -0.5
    for h in range(H):
        sl = slice(h * HEAD_PAD, (h + 1) * HEAD_PAD)
        qh = in_ref[0, :, sl].astype(F32)
        qn = qh * lax.rsqrt(jnp.sum(qh * qh, axis=-1, keepdims=True) / MOBA_HEAD_DIM + EPS)
        q_ref[0, :, sl] = (qn * (qgain_ref[...] * scale)).astype(q_ref.dtype)
        kh = in_ref[0, :, W + h * HEAD_PAD:W + (h + 1) * HEAD_PAD].astype(F32)
        kn = kh * lax.rsqrt(jnp.sum(kh * kh, axis=-1, keepdims=True) / MOBA_HEAD_DIM + EPS)
        kn = kn * kgain_ref[...]
        km_ref[0, 0, :, sl] = jnp.mean(kn, axis=0, keepdims=True)
        k_ref[0, :, sl] = (kn + tag).astype(k_ref.dtype)
    v_ref[0] = in_ref[0, :, 2 * W:]


def _moba_prep(moba_in, q_gain, k_gain):
    bsz, s, _ = moba_in.shape
    H = MOBA_HEADS
    tm = MOBA_BLOCK
    nb = s // tm
    pad_g = lambda v: jnp.pad(v, (0, HEAD_PAD - MOBA_HEAD_DIM)).reshape(1, HEAD_PAD)
    tok = lambda w: pl.BlockSpec((1, tm, w), lambda b, i: (b, i, 0))
    full = lambda a: pl.BlockSpec(a.shape, lambda b, i: (0,) * a.ndim)
    qg, kg = pad_g(q_gain), pad_g(k_gain)
    return pl.pallas_call(
        _moba_prep_kernel, grid=(bsz, nb),
        in_specs=[tok(MOBA_IN_W), full(qg), full(kg)],
        out_specs=[tok(H * HEAD_PAD), tok(H * HEAD_PAD), tok(H * MOBA_HEAD_DIM),
                   pl.BlockSpec((1, 1, 1, H * HEAD_PAD), lambda b, i: (b, i, 0, 0))],
        out_shape=[jax.ShapeDtypeStruct((bsz, s, H * HEAD_PAD), BF16),
                   jax.ShapeDtypeStruct((bsz, s, H * HEAD_PAD), BF16),
                   jax.ShapeDtypeStruct((bsz, s, H * MOBA_HEAD_DIM), BF16),
                   jax.ShapeDtypeStruct((bsz, nb, 1, H * HEAD_PAD), F32)],
        compiler_params=_params("arbitrary", "arbitrary"), name="moba_prep",
    )(moba_in, qg, kg)


def _attn_kernel(*refs, gated):
    if gated:
        q_ref, k_ref, v_ref, b0_ref, b1_ref, kmr_ref, og_ref, o_ref = refs
    else:
        q_ref, k_ref, v_ref, b0_ref, og_ref, o_ref = refs
    T = ATT_BLOCK
    j = pl.program_id(1)
    lane = lax.broadcasted_iota(jnp.int32, (T, HEAD_PAD), 1)
    halves = []
    pair = []
    for h in range(4):
        hs = slice(h * HEAD_PAD, (h + 1) * HEAD_PAD)
        vs = slice((h // 2) * HEAD_PAD, (h // 2 + 1) * HEAD_PAD)
        qh = q_ref[0, :, hs]
        if gated:
            gate = _dot_nt(qh, kmr_ref[0, h])
            valid = (lane >= GATE_LANE0) & (lane < GATE_LANE0 + j)
            g = jnp.where(valid, gate, -jnp.inf)
            chosen = jnp.zeros((T, HEAD_PAD), jnp.bool_)
            for _ in range(MOBA_TOPK):
                mx = jnp.max(g, axis=-1, keepdims=True)
                hit = (g == mx) & (mx > -jnp.inf)
                first = jnp.min(jnp.where(hit, lane, 2 * HEAD_PAD), axis=-1, keepdims=True)
                pick = lane == first
                chosen = chosen | pick
                g = jnp.where(pick, -jnp.inf, g)
            tagged = (lane >= GATE_LANE0) & (lane < GATE_LANE0 + k_ref.shape[1] // T)
            neg = jnp.where(tagged & jnp.logical_not(chosen), MASK_NEG, 0.0)
            q_past = (qh.astype(F32) + neg).astype(BF16)
        else:
            q_past = qh

        def update(carry, s, vb):
            m, l, acc = carry
            m_new = jnp.maximum(m, jnp.max(s, axis=-1, keepdims=True))
            alpha = jnp.exp(m - m_new)
            p = jnp.exp(s - m_new)
            l = alpha * l + jnp.sum(p, axis=-1, keepdims=True)
            acc = alpha * acc + _dot(p.astype(BF16), vb)
            return m_new, l, acc

        r_own = pl.multiple_of(j * T, T)
        s0 = _dot_nt(qh, k_ref[0, pl.ds(r_own, T), hs]) + b0_ref[h]
        m0 = jnp.max(s0, axis=-1, keepdims=True)
        p0 = jnp.exp(s0 - m0)
        carry = (m0, jnp.sum(p0, axis=-1, keepdims=True),
                 _dot(p0.astype(BF16), v_ref[0, pl.ds(r_own, T), vs]))

        def past_block(n, carry, bias=None):
            r = pl.multiple_of(n * T, T)
            s = _dot_nt(q_past, k_ref[0, pl.ds(r, T), hs])
            if bias is not None:
                s = s + bias
            return update(carry, s, v_ref[0, pl.ds(r, T), vs])

        if gated:
            carry = lax.cond(j >= 1, lambda c: past_block(j - 1, c, b1_ref[h]), lambda c: c, carry)
            n_far = jnp.maximum(j - 1, 0)
        else:
            n_far = j
        m, l, acc = lax.fori_loop(0, n_far, past_block, carry)
        pair.append(acc / l)
        if h % 2 == 1:
            halves.append(jnp.where(lane < MOBA_HEAD_DIM, pair[0], pair[1]))
            pair = []
    y = jnp.concatenate(halves, axis=1)
    y = y * lax.rsqrt(jnp.mean(y * y, axis=-1, keepdims=True) + EPS) * og_ref[...]
    o_ref[0] = y.astype(o_ref.dtype)


def _attention(q, k, v, bias0, out_gain, bias1=None, kmr=None):
    bsz, s, w = q.shape
    T = ATT_BLOCK
    gated = kmr is not None
    qspec = pl.BlockSpec((1, T, w), lambda b, i: (b, i, 0))
    seq = lambda a: pl.BlockSpec((1,) + a.shape[1:], lambda b, i: (b,) + (0,) * (a.ndim - 1))
    full = lambda a: pl.BlockSpec(a.shape, lambda b, i: (0,) * a.ndim)
    og = out_gain.reshape(1, -1)
    if gated:
        args = (q, k, v, bias0, bias1, kmr, og)
        specs = [qspec, seq(k), seq(v), full(bias0), full(bias1), seq(kmr), full(og)]
    else:
        args = (q, k, v, bias0, og)
        specs = [qspec, seq(k), seq(v), full(bias0), full(og)]
    return pl.pallas_call(
        functools.partial(_attn_kernel, gated=gated), grid=(bsz, s // T),
        in_specs=specs, out_specs=pl.BlockSpec((1, T, v.shape[-1]), lambda b, i: (b, i, 0)),
        out_shape=jax.ShapeDtypeStruct((bsz, s, v.shape[-1]), BF16),
        compiler_params=_params("arbitrary", "arbitrary"),
        name="moba_attention" if gated else "mla_attention",
    )(*args)


def _rel_bucket(n):
    max_exact = REL_BUCKETS // 2
    nf = jnp.maximum(n, max_exact).astype(F32)
    large = max_exact + (jnp.log(nf / max_exact) / math.log(REL_MAX_DIST / max_exact)
                         * (REL_BUCKETS - max_exact)).astype(jnp.int32)
    large = jnp.minimum(large, REL_BUCKETS - 1)
    return jnp.where(n < max_exact, n, large)


def _moba_bias_tiles(rel_bias):
    T = ATT_BLOCK
    i = jnp.arange(T)[:, None]
    c = jnp.arange(T)[None, :]
    bias_t = rel_bias.T
    far = bias_t[:, REL_BUCKETS - 1][:, None, None]
    d0 = i - c
    b0 = jnp.where((d0 >= 0)[None], bias_t[:, _rel_bucket(jnp.maximum(d0, 0))] - far, MASK_NEG)
    b1 = bias_t[:, _rel_bucket(T + i - c)] - far
    return b0.astype(F32), b1.astype(F32)


def _causal_tile(heads):
    T = ATT_BLOCK
    i = jnp.arange(T)[:, None]
    c = jnp.arange(T)[None, :]
    return jnp.broadcast_to(jnp.where(c <= i, 0.0, MASK_NEG).astype(F32), (heads, T, T))


def _outproj_kernel(ys_ref, ya_ref, yb_ref, x_ref, w_ref, gm_ref, g_ref, sc_ref, sh_ref, rw_ref,
                    rb_ref, xo_ref, hf_ref, route_ref):
    w0, w1 = SSD_WIDTH, SSD_WIDTH + MLA_HEADS * MLA_V
    y = (_dot(ys_ref[0], w_ref[0:w0, :]) + _dot(ya_ref[0], w_ref[w0:w1, :])
         + _dot(yb_ref[0], w_ref[w1:, :]))
    x = x_ref[0] + gm_ref[0] * y
    xo_ref[0] = x
    ms = jnp.mean(x * x, axis=-1, keepdims=True)
    hf = (x * lax.rsqrt(ms + EPS) * g_ref[...]) * (1.0 + sc_ref[0]) + sh_ref[0]
    hb = hf.astype(BF16)
    hf_ref[0] = hb
    logits_t = _dot(hb, rw_ref[...]).T
    score = _sigmoid(logits_t[0:N_EXPERTS, :])
    biased = score + rb_ref[...]
    E = EXPERTS_PER_GROUP
    gsum = []
    for g in range(N_EXPERT_GROUPS):
        r = [biased[g * E + e:g * E + e + 1, :] for e in range(E)]
        best = r[0] + r[1]
        for lo, hi in zip(PAIR_LO[1:], PAIR_HI[1:]):
            best = jnp.maximum(best, r[lo] + r[hi])
        gsum.append(best)
    gmax = functools.reduce(jnp.maximum, gsum)
    taken = jnp.zeros_like(gmax, dtype=jnp.bool_)
    gid = jnp.zeros_like(gmax)
    vb = [jnp.zeros_like(gmax) for _ in range(E)]
    vs = [jnp.zeros_like(gmax) for _ in range(E)]
    for g in range(N_EXPERT_GROUPS):
        is_g = (gsum[g] == gmax) & jnp.logical_not(taken)
        taken = taken | is_g
        gid = jnp.where(is_g, float(g), gid)
        for e in range(E):
            vb[e] = jnp.where(is_g, biased[g * E + e:g * E + e + 1, :], vb[e])
            vs[e] = jnp.where(is_g, score[g * E + e:g * E + e + 1, :], vs[e])
    sel = []
    for e in range(E):
        rank = jnp.zeros_like(gmax)
        for o in range(E):
            if o < e:
                rank = rank + (vb[o] >= vb[e]).astype(F32)
            elif o > e:
                rank = rank + (vb[o] > vb[e]).astype(F32)
        sel.append(rank < 2.0)
    pidx = jnp.zeros_like(gmax)
    w_lo = jnp.zeros_like(gmax)
    w_hi = jnp.zeros_like(gmax)
    for p, (lo, hi) in enumerate(zip(PAIR_LO, PAIR_HI)):
        is_p = sel[lo] & sel[hi]
        pidx = jnp.where(is_p, float(p), pidx)
        w_lo = jnp.where(is_p, vs[lo], w_lo)
        w_hi = jnp.where(is_p, vs[hi], w_hi)
    tot = w_lo + w_hi
    rows = lax.broadcasted_iota(jnp.int32, (8, gmax.shape[1]), 0)
    route_ref[0] = jnp.where(rows == 0, gid * N_PAIRS + pidx,
                             jnp.where(rows == 1, w_lo / tot, jnp.where(rows == 2, w_hi / tot, 0.0)))


def _outproj(y_ssd, y_mla, y_moba, x, w_out, gm, g, sc, sh, router_w, router_bias, tm):
    bsz, s, d = x.shape
    tok = lambda w: pl.BlockSpec((1, tm, w), lambda b, i: (b, i, 0))
    per_b = pl.BlockSpec((1, 1, d), lambda b, i: (b, 0, 0))
    full = lambda a: pl.BlockSpec(a.shape, lambda b, i: (0,) * a.ndim)
    rw = jnp.pad(router_w, ((0, 0), (0, LANES - N_EXPERTS))).astype(BF16)
    rb = router_bias.astype(F32).reshape(N_EXPERTS, 1)
    g2 = g.reshape(1, d)
    return pl.pallas_call(
        _outproj_kernel, grid=(bsz, s // tm),
        in_specs=[tok(y_ssd.shape[-1]), tok(y_mla.shape[-1]), tok(y_moba.shape[-1]), tok(d),
                  full(w_out), per_b, full(g2), per_b, per_b, full(rw), full(rb)],
        out_specs=[tok(d), tok(d), pl.BlockSpec((1, 8, tm), lambda b, i: (b, 0, i))],
        out_shape=[jax.ShapeDtypeStruct((bsz, s, d), F32), jax.ShapeDtypeStruct((bsz, s, d), BF16),
                   jax.ShapeDtypeStruct((bsz, 8, s), F32)],
        compiler_params=_params("arbitrary", "arbitrary"), name="out_proj_router",
    )(y_ssd, y_mla, y_moba, x, w_out, gm, g2, sc, sh, rw, rb)


def _moe_kernel(ea_ref, eb_ref, nv_ref, x_ref, w_ref, gua_ref, gub_ref, da_ref, db_ref, o_ref):
    i = pl.program_id(0)

    @pl.when(nv_ref[i] > 0)
    def _():
        xb = x_ref[...]
        out = None
        for gu_ref, d_ref, col in ((gua_ref, da_ref, 0), (gub_ref, db_ref, 1)):
            gu = _dot(xb, gu_ref[0])
            hid = (_silu(gu[:, :D_EXPERT]) * gu[:, D_EXPERT:]).astype(BF16)
            y = _dot(hid, d_ref[0]) * w_ref[:, col:col + 1]
            out = y if out is None else out + y
        o_ref[...] = out.astype(o_ref.dtype)

    @pl.when(nv_ref[i] <= 0)
    def _():
        o_ref[...] = jnp.zeros_like(o_ref)


def _moe(xs, ws, tile_ea, tile_eb, tile_nv, w_gu, w_down):
    nslot, d = xs.shape
    nt = nslot // MOE_TILE
    grid_spec = pltpu.PrefetchScalarGridSpec(
        num_scalar_prefetch=3, grid=(nt,),
        in_specs=[pl.BlockSpec((MOE_TILE, d), lambda i, ea, eb, nv: (i, 0)),
                  pl.BlockSpec((MOE_TILE, ws.shape[1]), lambda i, ea, eb, nv: (i, 0)),
                  pl.BlockSpec((1, d, 2 * D_EXPERT), lambda i, ea, eb, nv: (ea[i], 0, 0)),
                  pl.BlockSpec((1, d, 2 * D_EXPERT), lambda i, ea, eb, nv: (eb[i], 0, 0)),
                  pl.BlockSpec((1, D_EXPERT, d), lambda i, ea, eb, nv: (ea[i], 0, 0)),
                  pl.BlockSpec((1, D_EXPERT, d), lambda i, ea, eb, nv: (eb[i], 0, 0))],
        out_specs=pl.BlockSpec((MOE_TILE, d), lambda i, ea, eb, nv: (i, 0)))
    return pl.pallas_call(
        _moe_kernel, grid_spec=grid_spec,
        out_shape=jax.ShapeDtypeStruct((nslot, d), BF16),
        compiler_params=_params("arbitrary"), name="moe_ffn",
    )(tile_ea, tile_eb, tile_nv, xs, ws, w_gu, w_gu, w_down, w_down)


def _bucket_layout(bucket):
    t = bucket.shape[0]
    nslot = t + N_BUCKETS * MOE_TILE
    nt = nslot // MOE_TILE
    order = jnp.argsort(bucket, stable=True).astype(jnp.int32)
    counts = jnp.sum(bucket[:, None] == jnp.arange(N_BUCKETS)[None, :], axis=0).astype(jnp.int32)
    starts = jnp.cumsum(counts) - counts
    padded = ((counts + MOE_TILE - 1) // MOE_TILE) * MOE_TILE
    pends = jnp.cumsum(padded)
    pstarts = pends - padded
    sorted_bucket = bucket[order]
    slot = pstarts[sorted_bucket] + jnp.arange(t, dtype=jnp.int32) - starts[sorted_bucket]
    slot_token = jnp.zeros((nslot,), jnp.int32).at[slot].set(order)
    token_slot = jnp.zeros((t,), jnp.int32).at[order].set(slot)
    tile_start = jnp.arange(nt, dtype=jnp.int32) * MOE_TILE
    tile_bucket = jnp.sum(tile_start[:, None] >= pends[None, :], axis=1).astype(jnp.int32)
    live = tile_bucket < N_BUCKETS
    tb = jnp.minimum(tile_bucket, N_BUCKETS - 1)
    tile_nv = jnp.where(live, jnp.clip(pstarts[tb] + counts[tb] - tile_start, 0, MOE_TILE), 0)
    last_live = jnp.max(jnp.where(live, tb, 0))
    tb = jnp.where(live, tb, last_live)
    grp, pair = tb // N_PAIRS, tb % N_PAIRS
    tile_ea = grp * EXPERTS_PER_GROUP + jnp.asarray(PAIR_LO, jnp.int32)[pair]
    tile_eb = grp * EXPERTS_PER_GROUP + jnp.asarray(PAIR_HI, jnp.int32)[pair]
    return slot_token, token_slot, tile_ea.astype(jnp.int32), tile_eb.astype(jnp.int32), \
        tile_nv.astype(jnp.int32)


def _final_kernel(x_ref, moe_ref, gf_ref, o_ref):
    o_ref[0] = x_ref[0] + gf_ref[0] * moe_ref[0].astype(F32)


def _final_residual(x, moe, gf, tm):
    bsz, s, d = x.shape
    tok = pl.BlockSpec((1, tm, d), lambda b, i: (b, i, 0))
    return pl.pallas_call(
        _final_kernel, grid=(bsz, s // tm),
        in_specs=[tok, tok, pl.BlockSpec((1, 1, d), lambda b, i: (b, 0, 0))],
        out_specs=tok, out_shape=jax.ShapeDtypeStruct((bsz, s, d), F32),
        compiler_params=_params("arbitrary", "arbitrary"), name="final_residual",
    )(x, moe, gf)


def _pad_w_in(w_in):
    d = w_in.shape[0]
    sizes = (SSD_WIDTH, SSD_CONV_DIM, SSD_HEADS, MLA_Q_LORA, MLA_KV_LORA, MLA_ROPE,
             MOBA_HEADS * MOBA_HEAD_DIM, MOBA_HEADS * MOBA_HEAD_DIM, MOBA_HEADS * MOBA_HEAD_DIM)
    offs = np.cumsum((0,) + sizes)
    z, xbc, dtw, cq, ckv, krope, mq, mk, mv = [w_in[:, offs[i]:offs[i + 1]] for i in range(9)]
    zc = lambda n: jnp.zeros((d, n), w_in.dtype)
    kpe = jnp.concatenate([zc(MLA_NOPE), krope, zc(HEAD_PAD - MLA_QK)], axis=1)
    cols = [z, xbc, dtw, zc(LANES - SSD_HEADS), cq, ckv, kpe,
            _pad_heads(mq, MOBA_HEADS, MOBA_HEAD_DIM), _pad_heads(mk, MOBA_HEADS, MOBA_HEAD_DIM), mv]
    return jnp.concatenate(cols, axis=1).astype(BF16)


def kernel(x, c, ada_w, ada_b, norm_mix_g, norm_ffn_g, w_in, ssd_conv_w, ssd_conv_b, ssd_dt_bias,
           ssd_a_log, ssd_d, ssd_norm_g, mla_q_norm_g, mla_w_uq, mla_kv_norm_g, mla_w_ukv, mla_q_gain,
           mla_k_gain, mla_out_g, moba_q_gain, moba_k_gain, moba_out_g, rel_bias, w_out, router_w,
           router_bias, moe_w_gate, moe_w_up, moe_w_down):
    bsz, s, d = x.shape
    depth = ada_w.shape[0]
    tm = min(512, s)
    nb = s // MOBA_BLOCK
    mod = _modulation(c, ada_w, ada_b).reshape(depth, bsz, 6, 1, d)
    b0_moba, b1_moba = _moba_bias_tiles(rel_bias)
    b0_mla = _causal_tile(MLA_HEADS)
    moe_prev, gf_prev = None, None
    for l in range(depth):
        sh_m, sc_m, g_m, sh_f, sc_f, g_f = [mod[l, :, i] for i in range(6)]
        x, (z, xbc, dt, mla_in, moba_in) = _inproj(
            x, moe_prev, gf_prev, norm_mix_g[l], sc_m, sh_m, _pad_w_in(w_in[l]), tm)
        y_ssd = _ssd(z, xbc, dt, ssd_conv_w[l], ssd_conv_b[l], ssd_dt_bias[l], ssd_a_log[l],
                     ssd_d[l], ssd_norm_g[l])
        q, k, v = _mla_prep(mla_in, mla_q_norm_g[l], mla_w_uq[l], mla_kv_norm_g[l], mla_w_ukv[l],
                            mla_q_gain[l], mla_k_gain[l], tm)
        y_mla = _attention(q, k, v, b0_mla, mla_out_g[l])
        mq, mk, mv, kmean = _moba_prep(moba_in, moba_q_gain[l], moba_k_gain[l])
        km = kmean.reshape(bsz, nb, MOBA_HEADS, HEAD_PAD).transpose(0, 2, 1, 3)
        kmr = jnp.pad(km, ((0, 0), (0, 0), (GATE_LANE0, HEAD_PAD - GATE_LANE0 - nb), (0, 0))).astype(BF16)
        y_moba = _attention(mq, mk, mv, b0_moba, moba_out_g[l], b1_moba, kmr)
        x, hf, route = _outproj(y_ssd, y_mla, y_moba, x, w_out[l].astype(BF16), g_m, norm_ffn_g[l],
                                sc_f, sh_f, router_w, router_bias, tm)
        bucket = route[:, 0, :].reshape(-1).astype(jnp.int32)
        slot_token, token_slot, tile_ea, tile_eb, tile_nv = _bucket_layout(bucket)
        ws = jnp.stack([route[:, 1, :].reshape(-1), route[:, 2, :].reshape(-1)], axis=1)
        xs = hf.reshape(-1, d)[slot_token]
        w_gu = jnp.concatenate([moe_w_gate[l], moe_w_up[l]], axis=-1).astype(BF16)
        ys = _moe(xs, ws[slot_token], tile_ea, tile_eb, tile_nv, w_gu, moe_w_down[l].astype(BF16))
        moe_prev = ys[token_slot].reshape(bsz, s, d)
        gf_prev = g_f
    return _final_residual(x, moe_prev, gf_prev, tm)
```

```python
import functools
import math

import numpy as np
import jax
import jax.numpy as jnp
from jax import lax
from jax.experimental import pallas as pl
from jax.experimental.pallas import tpu as pltpu

D_MODEL = 1024
DEPTH = 2
SSD_HEADS = 8
SSD_HEAD_DIM = 64
SSD_WIDTH = SSD_HEADS * SSD_HEAD_DIM
SSD_GROUPS = 2
SSD_STATE = 128
SSD_CONV = 4
SSD_CHUNK = 128
SSD_CONV_DIM = SSD_WIDTH + 2 * SSD_GROUPS * SSD_STATE
MLA_HEADS = 4
MLA_Q_LORA = 256
MLA_KV_LORA = 128
MLA_NOPE = 64
MLA_ROPE = 32
MLA_V = 64
MLA_QK = MLA_NOPE + MLA_ROPE
ROPE_BASE = 10000.0
MOBA_HEADS = 4
MOBA_HEAD_DIM = 64
MOBA_BLOCK = 256
MOBA_TOPK = 3
REL_BUCKETS = 32
REL_MAX_DIST = 128
N_EXPERTS = 16
N_EXPERT_GROUPS = 4
EXPERTS_PER_GROUP = 4
D_EXPERT = 512
EPS = 1e-6

LANES = 128
HEAD_PAD = 128
ATT_BLOCK = 256
ATT_HEADS = 4
V_PAD = 64
VMEM_LIMIT = 56 * 1024 * 1024
MASK_NEG = -1e9
N_PAIRS = 6
N_BUCKETS = N_EXPERT_GROUPS * N_PAIRS
MOE_TILE = 256
PAIR_LO = (0, 0, 0, 1, 1, 2)
PAIR_HI = (1, 2, 3, 2, 3, 3)

COL_Z = 0
COL_XBC = COL_Z + SSD_WIDTH
COL_DT = COL_XBC + SSD_CONV_DIM
COL_MLA = COL_DT + LANES
MLA_IN_W = MLA_Q_LORA + MLA_KV_LORA + LANES
COL_MOBA = COL_MLA + MLA_IN_W
MOBA_IN_W = 2 * MOBA_HEADS * HEAD_PAD + MOBA_HEADS * MOBA_HEAD_DIM
N_IN_PAD = COL_MOBA + MOBA_IN_W

F32 = jnp.float32
BF16 = jnp.bfloat16


def _params(*sem):
    return pltpu.CompilerParams(dimension_semantics=sem, vmem_limit_bytes=VMEM_LIMIT)


def _dot(a, b):
    return jnp.dot(a, b, preferred_element_type=F32)


def _dot_nt(a, b):
    return lax.dot_general(a, b, (((1,), (1,)), ((), ())), preferred_element_type=F32)


def _dot_exact(a, b):
    return jnp.dot(a, b, preferred_element_type=F32, precision=lax.Precision.HIGHEST)


def _silu(v):
    return v * (1.0 / (1.0 + jnp.exp(-v)))


def _sigmoid(v):
    return 1.0 / (1.0 + jnp.exp(-v))


def _softplus(v):
    return jnp.maximum(v, 0.0) + jnp.log1p(jnp.exp(-jnp.abs(v)))


def _ada_kernel(c_ref, w_ref, b_ref, o_ref):
    c = c_ref[...]
    o_ref[0] = _dot_exact(_silu(c), w_ref[0]) + b_ref[0]


def _modulation(c, ada_w, ada_b):
    depth, d, n = ada_w.shape
    bsz = c.shape[0]
    tn = d
    return pl.pallas_call(
        _ada_kernel,
        grid=(depth, n // tn),
        in_specs=[pl.BlockSpec((bsz, d), lambda l, j: (0, 0)),
                  pl.BlockSpec((1, d, tn), lambda l, j: (l, 0, j)),
                  pl.BlockSpec((1, 1, tn), lambda l, j: (l, 0, j))],
        out_specs=pl.BlockSpec((1, bsz, tn), lambda l, j: (l, 0, j)),
        out_shape=jax.ShapeDtypeStruct((depth, bsz, n), F32),
        compiler_params=_params("arbitrary", "arbitrary"),
        name="ada_modulation",
    )(c, ada_w, ada_b.reshape(depth, 1, n))


def _inproj_body(x, g_ref, sc_ref, sh_ref, w_ref, z_ref, xbc_ref, dt_ref, mla_ref, moba_ref):
    ms = jnp.mean(x * x, axis=-1, keepdims=True)
    hm = (x * lax.rsqrt(ms + EPS) * g_ref[...]) * (1.0 + sc_ref[0]) + sh_ref[0]
    hb = hm.astype(BF16)
    z_ref[0] = _dot(hb, w_ref[:, COL_Z:COL_XBC]).astype(z_ref.dtype)
    xbc_ref[0] = _dot(hb, w_ref[:, COL_XBC:COL_DT]).astype(xbc_ref.dtype)
    dt_ref[0] = _dot(hb, w_ref[:, COL_DT:COL_MLA])
    mla_ref[0] = _dot(hb, w_ref[:, COL_MLA:COL_MOBA]).astype(mla_ref.dtype)
    moba_ref[0] = _dot(hb, w_ref[:, COL_MOBA:N_IN_PAD]).astype(moba_ref.dtype)


def _inproj_kernel(x_ref, g_ref, sc_ref, sh_ref, w_ref, *outs):
    _inproj_body(x_ref[0], g_ref, sc_ref, sh_ref, w_ref, *outs)


def _inproj_res_kernel(x_ref, moe_ref, gf_ref, g_ref, sc_ref, sh_ref, w_ref, xo_ref, *outs):
    x = x_ref[0] + gf_ref[0] * moe_ref[0].astype(F32)
    xo_ref[0] = x
    _inproj_body(x, g_ref, sc_ref, sh_ref, w_ref, *outs)


def _inproj(x, moe_prev, gf_prev, g, sc, sh, w_pad, tm):
    bsz, s, d = x.shape
    tok = lambda w: pl.BlockSpec((1, tm, w), lambda b, i: (b, i, 0))
    per_b = pl.BlockSpec((1, 1, d), lambda b, i: (b, 0, 0))
    out_shapes = [jax.ShapeDtypeStruct((bsz, s, SSD_WIDTH), BF16),
                  jax.ShapeDtypeStruct((bsz, s, SSD_CONV_DIM), BF16),
                  jax.ShapeDtypeStruct((bsz, s, LANES), F32),
                  jax.ShapeDtypeStruct((bsz, s, MLA_IN_W), BF16),
                  jax.ShapeDtypeStruct((bsz, s, MOBA_IN_W), BF16)]
    out_specs = [tok(SSD_WIDTH), tok(SSD_CONV_DIM), tok(LANES), tok(MLA_IN_W), tok(MOBA_IN_W)]
    common_specs = [pl.BlockSpec((1, d), lambda b, i: (0, 0)), per_b, per_b,
                    pl.BlockSpec((d, N_IN_PAD), lambda b, i: (0, 0))]
    common = (g.reshape(1, d), sc, sh, w_pad)
    if moe_prev is None:
        outs = pl.pallas_call(
            _inproj_kernel, grid=(bsz, s // tm),
            in_specs=[tok(d)] + common_specs, out_specs=out_specs, out_shape=out_shapes,
            compiler_params=_params("arbitrary", "arbitrary"), name="in_proj",
        )(x, *common)
        return x, outs
    outs = pl.pallas_call(
        _inproj_res_kernel, grid=(bsz, s // tm),
        in_specs=[tok(d), tok(d), per_b] + common_specs,
        out_specs=[tok(d)] + out_specs,
        out_shape=[jax.ShapeDtypeStruct((bsz, s, d), F32)] + out_shapes,
        compiler_params=_params("arbitrary", "arbitrary"), name="in_proj_res",
    )(x, moe_prev, gf_prev, *common)
    return outs[0], outs[1:]


SSD_STEP_CHUNKS = 2
CONV_HALO = 8


def _ssd_kernel(z_ref, xbc_ref, dt_ref, cw_ref, cb_ref, dtb_ref, alog_ref, dskip_ref, ng_ref,
                expand_ref, o_ref, state_ref, ext_ref):
    L, G, N, P = SSD_CHUNK, SSD_GROUPS, SSD_STATE, SSD_HEAD_DIM
    R = SSD_HEADS // G
    GW = R * P
    first = pl.program_id(1) == 0

    @pl.when(first)
    def _():
        state_ref[...] = jnp.zeros_like(state_ref)
        ext_ref[0:CONV_HALO, :] = jnp.zeros((CONV_HALO, SSD_CONV_DIM), F32)

    row = lax.broadcasted_iota(jnp.int32, (L, L), 0)
    col = lax.broadcasted_iota(jnp.int32, (L, L), 1)
    causal = col <= row
    tril = causal.astype(F32)
    a_row = -jnp.exp(alog_ref[...])

    for ck in range(SSD_STEP_CHUNKS):
        r0 = ck * L
        ext_ref[CONV_HALO:CONV_HALO + L, :] = xbc_ref[0, r0:r0 + L, :].astype(F32)
        conv = cb_ref[...]
        for k in range(SSD_CONV):
            off = CONV_HALO - (SSD_CONV - 1) + k
            conv = conv + cw_ref[k:k + 1, :] * ext_ref[off:off + L, :]
        ext_ref[0:CONV_HALO, :] = ext_ref[L:L + CONV_HALO, :]
        xc = _silu(conv)
        xs = xc[:, :SSD_WIDTH]

        dt = _softplus(dt_ref[0, r0:r0 + L, :] + dtb_ref[...])
        a_cum = _dot_exact(tril, dt * a_row)
        a_cum_t = a_cum.T
        a_last = a_cum[L - 1:L, :]
        stacked = jnp.concatenate(
            [dt, jnp.exp(a_cum), jnp.exp(a_last - a_cum),
             jnp.broadcast_to(jnp.exp(a_last), (8, LANES))], axis=0)
        wide = _dot_exact(stacked, expand_ref[...])
        dt_w, ea_w, te_w, cd_w = wide[0:L], wide[L:2 * L], wide[2 * L:3 * L], wide[3 * L:3 * L + 1]
        xdt = xs * dt_w
        xdt_b = xdt.astype(BF16)
        xw_b = (xdt * te_w).astype(BF16)

        y_parts = []
        for g in range(G):
            bm = xc[:, SSD_WIDTH + g * N:SSD_WIDTH + (g + 1) * N]
            cm = xc[:, SSD_WIDTH + (G + g) * N:SSD_WIDTH + (G + g + 1) * N]
            bm_b, cm_b = bm.astype(BF16), cm.astype(BF16)
            cb = _dot_nt(cm_b, bm_b)
            st = state_ref[g]
            y_off = _dot(cm_b, st.astype(BF16)) * ea_w[:, g * GW:(g + 1) * GW]
            diag = []
            for r in range(R):
                h = g * R + r
                seg = a_cum[:, h:h + 1] - a_cum_t[h:h + 1, :]
                m = (cb * jnp.exp(jnp.where(causal, seg, -jnp.inf))).astype(BF16)
                diag.append(_dot(m, xdt_b[:, h * P:(h + 1) * P]))
            y_parts.append(jnp.concatenate(diag, axis=1) + y_off)
            state_ref[g] = st * cd_w[:, g * GW:(g + 1) * GW] + _dot(
                bm.T.astype(BF16), xw_b[:, g * GW:(g + 1) * GW])

        y = jnp.concatenate(y_parts, axis=1) + xs * dskip_ref[...]
        y = y * _silu(z_ref[0, r0:r0 + L, :].astype(F32))
        outs = []
        for g in range(G):
            yg = y[:, g * GW:(g + 1) * GW]
            outs.append(yg * lax.rsqrt(jnp.mean(yg * yg, axis=-1, keepdims=True) + EPS))
        o_ref[0, r0:r0 + L, :] = (jnp.concatenate(outs, axis=1) * ng_ref[...]).astype(o_ref.dtype)


def _ssd(z, xbc, dt, conv_w, conv_b, dt_bias, a_log, d_skip, norm_g):
    bsz, s, _ = z.shape
    ts = SSD_STEP_CHUNKS * SSD_CHUNK
    pad_h = lambda v: jnp.pad(v.astype(F32), (0, LANES - SSD_HEADS)).reshape(1, LANES)
    expand = jnp.repeat(jnp.eye(LANES, SSD_HEADS, dtype=F32), SSD_HEAD_DIM, axis=1)
    tok = lambda w: pl.BlockSpec((1, ts, w), lambda b, i: (b, i, 0))
    full = lambda a: pl.BlockSpec(a.shape, lambda b, i: (0,) * a.ndim)
    params = (conv_w, conv_b.reshape(1, -1), pad_h(dt_bias), pad_h(a_log),
              jnp.repeat(d_skip, SSD_HEAD_DIM).reshape(1, -1), norm_g.reshape(1, -1), expand)
    return pl.pallas_call(
        _ssd_kernel, grid=(bsz, s // ts),
        in_specs=[tok(SSD_WIDTH), tok(SSD_CONV_DIM), tok(LANES)] + [full(p) for p in params],
        out_specs=tok(SSD_WIDTH),
        out_shape=jax.ShapeDtypeStruct((bsz, s, SSD_WIDTH), BF16),
        scratch_shapes=[pltpu.VMEM((SSD_GROUPS, SSD_STATE, SSD_WIDTH // SSD_GROUPS), F32),
                        pltpu.VMEM((CONV_HALO + SSD_CHUNK, SSD_CONV_DIM), F32)],
        compiler_params=_params("arbitrary", "arbitrary"), name="ssd_scan",
    )(z, xbc, dt, *params)


def _augment_v_t(v):
    lane = lax.broadcasted_iota(jnp.int32, (v.shape[0], HEAD_PAD), 1)
    tiles = []
    for p in range(v.shape[1] // HEAD_PAD):
        pt = v[:, p * HEAD_PAD:(p + 1) * HEAD_PAD]
        tiles.append(jnp.where(lane < V_PAD, pt, 1.0))
        tiles.append(jnp.where(lane < V_PAD, pltpu.roll(pt, V_PAD, axis=1), 1.0))
    return jnp.concatenate(tiles, axis=1).T


def _rope_mix(v, cos, sin_hi, sin_lo):
    w = v.shape[1]
    half = MLA_ROPE // 2
    return v * cos + pltpu.roll(v, half, axis=1) * sin_hi + pltpu.roll(v, w - half, axis=1) * sin_lo


def _mla_prep_kernel(in_ref, gq_ref, wq_ref, gkv_ref, wkv_ref, cos_ref, shi_ref, slo_ref,
                     qgain_ref, kgain_ref, q_ref, k_ref, v_ref):
    H = MLA_HEADS
    xin = in_ref[0].astype(F32)
    cq = xin[:, :MLA_Q_LORA]
    ckv = xin[:, MLA_Q_LORA:MLA_Q_LORA + MLA_KV_LORA]
    kpe = xin[:, MLA_Q_LORA + MLA_KV_LORA:]
    cqn = cq * lax.rsqrt(jnp.mean(cq * cq, axis=-1, keepdims=True) + EPS) * gq_ref[...]
    ckvn = ckv * lax.rsqrt(jnp.mean(ckv * ckv, axis=-1, keepdims=True) + EPS) * gkv_ref[...]
    q = _dot(cqn.astype(BF16), wq_ref[...])
    kv = _dot(ckvn.astype(BF16), wkv_ref[...])
    cos, shi, slo = cos_ref[...], shi_ref[...], slo_ref[...]
    tile = lambda t: jnp.concatenate([t] * H, axis=1)
    q = _rope_mix(q, tile(cos), tile(shi), tile(slo))
    kpe = _rope_mix(kpe, cos, shi, slo)
    scale = MLA_QK ** -0.5
    for h in range(H):
        sl = slice(h * HEAD_PAD, (h + 1) * HEAD_PAD)
        qh = q[:, sl]
        qn = qh * lax.rsqrt(jnp.sum(qh * qh, axis=-1, keepdims=True) / MLA_QK + EPS)
        q_ref[0, :, sl] = (qn * (qgain_ref[...] * scale)).astype(q_ref.dtype)
        kh = kv[:, sl] + kpe
        kn = kh * lax.rsqrt(jnp.sum(kh * kh, axis=-1, keepdims=True) / MLA_QK + EPS)
        k_ref[0, :, sl] = (kn * kgain_ref[...]).astype(k_ref.dtype)
    v_ref[0] = _augment_v_t(kv[:, H * HEAD_PAD:]).astype(v_ref.dtype)


def _rope_tables(s):
    half = MLA_ROPE // 2
    inv = ROPE_BASE ** (-2.0 * jnp.arange(half, dtype=F32) / MLA_ROPE)
    ang = jnp.arange(s, dtype=F32)[:, None] * inv[None, :]
    cos, sin = jnp.cos(ang), jnp.sin(ang)
    zeros = lambda n: jnp.zeros((s, n), F32)
    cos_t = jnp.concatenate([jnp.ones((s, MLA_NOPE), F32), cos, cos, zeros(HEAD_PAD - MLA_QK)], axis=1)
    sin_hi = jnp.concatenate([zeros(MLA_NOPE + half), sin, zeros(HEAD_PAD - MLA_QK)], axis=1)
    sin_lo = jnp.concatenate([zeros(MLA_NOPE), -sin, zeros(half + HEAD_PAD - MLA_QK)], axis=1)
    return cos_t, sin_hi, sin_lo


def _pad_heads(w, heads, width):
    lead = w.shape[:-1]
    w = w.reshape(*lead, heads, width)
    w = jnp.pad(w, [(0, 0)] * len(lead) + [(0, 0), (0, HEAD_PAD - width)])
    return w.reshape(*lead, heads * HEAD_PAD)


def _mla_prep(mla_in, gq, w_uq, gkv, w_ukv, q_gain, k_gain, tm):
    bsz, s, _ = mla_in.shape
    H = MLA_HEADS
    wq = _pad_heads(w_uq, H, MLA_QK).astype(BF16)
    wkv = w_ukv.reshape(MLA_KV_LORA, H, MLA_NOPE + MLA_V)
    wk = _pad_heads(wkv[:, :, :MLA_NOPE].reshape(MLA_KV_LORA, H * MLA_NOPE), H, MLA_NOPE)
    wv = wkv[:, :, MLA_NOPE:].reshape(MLA_KV_LORA, H * MLA_V)
    wkv_p = jnp.concatenate([wk, wv], axis=1).astype(BF16)
    cos_t, sin_hi, sin_lo = _rope_tables(s)
    pad_g = lambda v: jnp.pad(v, (0, HEAD_PAD - MLA_QK)).reshape(1, HEAD_PAD)
    tok = lambda w: pl.BlockSpec((1, tm, w), lambda b, i: (b, i, 0))
    full = lambda a: pl.BlockSpec(a.shape, lambda b, i: (0,) * a.ndim)
    pos = pl.BlockSpec((tm, HEAD_PAD), lambda b, i: (i, 0))
    gq2, gkv2, qg, kg = gq.reshape(1, -1), gkv.reshape(1, -1), pad_g(q_gain), pad_g(k_gain)
    return pl.pallas_call(
        _mla_prep_kernel, grid=(bsz, s // tm),
        in_specs=[tok(MLA_IN_W), full(gq2), full(wq), full(gkv2), full(wkv_p), pos, pos, pos,
                  full(qg), full(kg)],
        out_specs=[tok(H * HEAD_PAD)] * 2 + [pl.BlockSpec((1, H * HEAD_PAD, tm), lambda b, i: (b, 0, i))],
        out_shape=[jax.ShapeDtypeStruct((bsz, s, H * HEAD_PAD), BF16)] * 2 + [
            jax.ShapeDtypeStruct((bsz, H * HEAD_PAD, s), BF16)],
        compiler_params=_params("arbitrary", "arbitrary"), name="mla_prep",
    )(mla_in, gq2, wq, gkv2, wkv_p, cos_t, sin_hi, sin_lo, qg, kg)


GATE_LANE0 = MOBA_HEAD_DIM


def _moba_prep_kernel(in_ref, qgain_ref, kgain_ref, q_ref, k_ref, v_ref, km_ref):
    H, W = MOBA_HEADS, MOBA_HEADS * HEAD_PAD
    blk = pl.program_id(1)
    lane = lax.broadcasted_iota(jnp.int32, (1, HEAD_PAD), 1)
    tag = (lane == GATE_LANE0 + blk).astype(F32)
    scale = MOBA_HEAD_DIM ** -0.5
    for h in range(H):
        sl = slice(h * HEAD_PAD, (h + 1) * HEAD_PAD)
        qh = in_ref[0, :, sl].astype(F32)
        qn = qh * lax.rsqrt(jnp.sum(qh * qh, axis=-1, keepdims=True) / MOBA_HEAD_DIM + EPS)
        q_ref[0, :, sl] = (qn * (qgain_ref[...] * scale)).astype(q_ref.dtype)
        kh = in_ref[0, :, W + h * HEAD_PAD:W + (h + 1) * HEAD_PAD].astype(F32)
        kn = kh * lax.rsqrt(jnp.sum(kh * kh, axis=-1, keepdims=True) / MOBA_HEAD_DIM + EPS)
        kn = kn * kgain_ref[...]
        km_ref[0, 0, :, sl] = jnp.mean(kn, axis=0, keepdims=True)
        k_ref[0, :, sl] = (kn + tag).astype(k_ref.dtype)
    v_ref[0] = _augment_v_t(in_ref[0, :, 2 * W:].astype(F32)).astype(v_ref.dtype)


def _moba_prep(moba_in, q_gain, k_gain):
    bsz, s, _ = moba_in.shape
    H = MOBA_HEADS
    tm = MOBA_BLOCK
    nb = s // tm
    pad_g = lambda v: jnp.pad(v, (0, HEAD_PAD - MOBA_HEAD_DIM)).reshape(1, HEAD_PAD)
    tok = lambda w: pl.BlockSpec((1, tm, w), lambda b, i: (b, i, 0))
    full = lambda a: pl.BlockSpec(a.shape, lambda b, i: (0,) * a.ndim)
    qg, kg = pad_g(q_gain), pad_g(k_gain)
    return pl.pallas_call(
        _moba_prep_kernel, grid=(bsz, nb),
        in_specs=[tok(MOBA_IN_W), full(qg), full(kg)],
        out_specs=[tok(H * HEAD_PAD)] * 2 + [
            pl.BlockSpec((1, H * HEAD_PAD, tm), lambda b, i: (b, 0, i)),
            pl.BlockSpec((1, 1, 1, H * HEAD_PAD), lambda b, i: (b, i, 0, 0))],
        out_shape=[jax.ShapeDtypeStruct((bsz, s, H * HEAD_PAD), BF16)] * 2 + [
            jax.ShapeDtypeStruct((bsz, H * HEAD_PAD, s), BF16),
            jax.ShapeDtypeStruct((bsz, nb, 1, H * HEAD_PAD), F32)],
        compiler_params=_params("arbitrary", "arbitrary"), name="moba_prep",
    )(moba_in, qg, kg)


def _attn_kernel(*refs, gated):
    if gated:
        q_ref, k_ref, vt_ref, b0_ref, b1_ref, kmr_ref, og_ref, o_ref, m_ref, acc_ref, qt_ref, qpt_ref = refs
    else:
        q_ref, k_ref, vt_ref, b0_ref, og_ref, o_ref, m_ref, acc_ref, qt_ref = refs
        qpt_ref = qt_ref
    T = ATT_BLOCK
    H = ATT_HEADS
    j = pl.program_id(1)
    hsl = lambda h: slice(h * HEAD_PAD, (h + 1) * HEAD_PAD)

    def step(r, rows, qts_ref, bias_ref=None, first=False):
        sts = []
        for h in range(H):
            st = _dot(k_ref[0, pl.ds(r, rows), hsl(h)], qts_ref[h])
            sts.append(st if bias_ref is None else st + bias_ref[h])
        for h in range(H):
            vt = vt_ref[0, hsl(h), pl.ds(r, rows)]
            mx = jnp.max(sts[h], axis=0, keepdims=True)
            if first:
                m_ref[h] = mx
                acc_ref[h] = _dot(vt, jnp.exp(sts[h] - mx).astype(BF16))
            else:
                m_old = m_ref[h]
                m_new = jnp.maximum(m_old, mx)
                m_ref[h] = m_new
                acc_ref[h] = (jnp.exp(m_old - m_new) * acc_ref[h]
                              + _dot(vt, jnp.exp(sts[h] - m_new).astype(BF16)))

    for h in range(H):
        qt = q_ref[0, :, hsl(h)].astype(F32).T
        qt_b = qt.astype(BF16)
        qt_ref[h] = qt_b
        if gated:
            nbp = -(-(k_ref.shape[1] // T) // 8) * 8
            g = _dot(kmr_ref[0, h], qt_b)[GATE_LANE0:GATE_LANE0 + nbp]
            row = lax.broadcasted_iota(jnp.int32, (nbp, T), 0)
            g = jnp.where(row < j, g, -jnp.inf)
            chosen = jnp.zeros((nbp, T), jnp.bool_)
            for _ in range(MOBA_TOPK):
                mx = jnp.max(g, axis=0, keepdims=True)
                hit = (g == mx) & (mx > -jnp.inf)
                first = jnp.min(jnp.where(hit, row, nbp), axis=0, keepdims=True)
                pick = row == first
                chosen = chosen | pick
                g = jnp.where(pick, -jnp.inf, g)
            neg = jnp.where(chosen, 0.0, MASK_NEG)
            qpt_ref[h] = jnp.concatenate(
                [qt[:GATE_LANE0], qt[GATE_LANE0:GATE_LANE0 + nbp] + neg, qt[GATE_LANE0 + nbp:]],
                axis=0).astype(BF16)

    step(pl.multiple_of(j * T, T), T, qt_ref, b0_ref, first=True)
    if gated:
        @pl.when(j >= 1)
        def _():
            step(pl.multiple_of((j - 1) * T, T), T, qpt_ref, b1_ref)
        n_far = jnp.maximum(j - 1, 0)
    else:
        n_far = j

    def pair_body(i, carry):
        step(pl.multiple_of(i * (2 * T), 2 * T), 2 * T, qpt_ref)
        return carry

    lax.fori_loop(0, lax.shift_right_logical(n_far, 1), pair_body, 0)

    @pl.when((n_far & 1) == 1)
    def _():
        step(pl.multiple_of((n_far - 1) * T, T), T, qpt_ref)

    outs = []
    for h in range(H):
        a = acc_ref[h]
        outs.append(a[:V_PAD] * (1.0 / a[V_PAD:]))
    y = jnp.concatenate(outs, axis=0).T
    y = y * lax.rsqrt(jnp.mean(y * y, axis=-1, keepdims=True) + EPS) * og_ref[...]
    o_ref[0] = y.astype(o_ref.dtype)


def _attention(q, k, v, bias0, out_gain, bias1=None, kmr=None):
    bsz, s, w = q.shape
    T = ATT_BLOCK
    gated = kmr is not None
    qspec = pl.BlockSpec((1, T, w), lambda b, i: (b, i, 0))
    seq = lambda a: pl.BlockSpec((1,) + a.shape[1:], lambda b, i: (b,) + (0,) * (a.ndim - 1))
    full = lambda a: pl.BlockSpec(a.shape, lambda b, i: (0,) * a.ndim)
    og = out_gain.reshape(1, -1)
    if gated:
        args = (q, k, v, bias0, bias1, kmr, og)
        specs = [qspec, seq(k), seq(v), full(bias0), full(bias1), seq(kmr), full(og)]
    else:
        args = (q, k, v, bias0, og)
        specs = [qspec, seq(k), seq(v), full(bias0), full(og)]
    wo = ATT_HEADS * V_PAD
    scratch = [pltpu.VMEM((ATT_HEADS, 1, T), F32), pltpu.VMEM((ATT_HEADS, HEAD_PAD, T), F32),
               pltpu.VMEM((ATT_HEADS, HEAD_PAD, T), BF16)]
    if gated:
        scratch.append(pltpu.VMEM((ATT_HEADS, HEAD_PAD, T), BF16))
    return pl.pallas_call(
        functools.partial(_attn_kernel, gated=gated), grid=(bsz, s // T),
        in_specs=specs, out_specs=pl.BlockSpec((1, T, wo), lambda b, i: (b, i, 0)),
        out_shape=jax.ShapeDtypeStruct((bsz, s, wo), BF16),
        scratch_shapes=scratch,
        compiler_params=_params("arbitrary", "arbitrary"),
        name="moba_attention" if gated else "mla_attention",
    )(*args)


def _rel_bucket(n):
    max_exact = REL_BUCKETS // 2
    nf = jnp.maximum(n, max_exact).astype(F32)
    large = max_exact + (jnp.log(nf / max_exact) / math.log(REL_MAX_DIST / max_exact)
                         * (REL_BUCKETS - max_exact)).astype(jnp.int32)
    large = jnp.minimum(large, REL_BUCKETS - 1)
    return jnp.where(n < max_exact, n, large)


def _moba_bias_tiles(rel_bias):
    T = ATT_BLOCK
    i = jnp.arange(T)[:, None]
    c = jnp.arange(T)[None, :]
    bias_t = rel_bias.T
    far = bias_t[:, REL_BUCKETS - 1][:, None, None]
    d0 = i - c
    b0 = jnp.where((d0 >= 0)[None], bias_t[:, _rel_bucket(jnp.maximum(d0, 0))] - far, MASK_NEG)
    b1 = bias_t[:, _rel_bucket(T + i - c)] - far
    return b0.astype(F32).transpose(0, 2, 1), b1.astype(F32).transpose(0, 2, 1)


def _causal_tile(heads):
    T = ATT_BLOCK
    i = jnp.arange(T)[:, None]
    c = jnp.arange(T)[None, :]
    return jnp.broadcast_to(jnp.where(c <= i, 0.0, MASK_NEG).astype(F32).T, (heads, T, T))


def _outproj_kernel(ys_ref, ya_ref, yb_ref, x_ref, w_ref, gm_ref, g_ref, sc_ref, sh_ref, rw_ref,
                    rb_ref, xo_ref, hf_ref, route_ref):
    w0, w1 = SSD_WIDTH, SSD_WIDTH + MLA_HEADS * MLA_V
    y = (_dot(ys_ref[0], w_ref[0:w0, :]) + _dot(ya_ref[0], w_ref[w0:w1, :])
         + _dot(yb_ref[0], w_ref[w1:, :]))
    x = x_ref[0] + gm_ref[0] * y
    xo_ref[0] = x
    ms = jnp.mean(x * x, axis=-1, keepdims=True)
    hf = (x * lax.rsqrt(ms + EPS) * g_ref[...]) * (1.0 + sc_ref[0]) + sh_ref[0]
    hb = hf.astype(BF16)
    hf_ref[0] = hb
    logits_t = _dot(hb, rw_ref[...]).T
    score = _sigmoid(logits_t[0:N_EXPERTS, :])
    biased = score + rb_ref[...]
    E = EXPERTS_PER_GROUP
    gsum = []
    for g in range(N_EXPERT_GROUPS):
        r = [biased[g * E + e:g * E + e + 1, :] for e in range(E)]
        best = r[0] + r[1]
        for lo, hi in zip(PAIR_LO[1:], PAIR_HI[1:]):
            best = jnp.maximum(best, r[lo] + r[hi])
        gsum.append(best)
    gmax = functools.reduce(jnp.maximum, gsum)
    taken = jnp.zeros_like(gmax, dtype=jnp.bool_)
    gid = jnp.zeros_like(gmax)
    vb = [jnp.zeros_like(gmax) for _ in range(E)]
    vs = [jnp.zeros_like(gmax) for _ in range(E)]
    for g in range(N_EXPERT_GROUPS):
        is_g = (gsum[g] == gmax) & jnp.logical_not(taken)
        taken = taken | is_g
        gid = jnp.where(is_g, float(g), gid)
        for e in range(E):
            vb[e] = jnp.where(is_g, biased[g * E + e:g * E + e + 1, :], vb[e])
            vs[e] = jnp.where(is_g, score[g * E + e:g * E + e + 1, :], vs[e])
    sel = []
    for e in range(E):
        rank = jnp.zeros_like(gmax)
        for o in range(E):
            if o < e:
                rank = rank + (vb[o] >= vb[e]).astype(F32)
            elif o > e:
                rank = rank + (vb[o] > vb[e]).astype(F32)
        sel.append(rank < 2.0)
    pidx = jnp.zeros_like(gmax)
    w_lo = jnp.zeros_like(gmax)
    w_hi = jnp.zeros_like(gmax)
    for p, (lo, hi) in enumerate(zip(PAIR_LO, PAIR_HI)):
        is_p = sel[lo] & sel[hi]
        pidx = jnp.where(is_p, float(p), pidx)
        w_lo = jnp.where(is_p, vs[lo], w_lo)
        w_hi = jnp.where(is_p, vs[hi], w_hi)
    tot = w_lo + w_hi
    rows = lax.broadcasted_iota(jnp.int32, (8, gmax.shape[1]), 0)
    route_ref[0] = jnp.where(rows == 0, gid * N_PAIRS + pidx,
                             jnp.where(rows == 1, w_lo / tot, jnp.where(rows == 2, w_hi / tot, 0.0)))


def _outproj(y_ssd, y_mla, y_moba, x, w_out, gm, g, sc, sh, router_w, router_bias, tm):
    bsz, s, d = x.shape
    tok = lambda w: pl.BlockSpec((1, tm, w), lambda b, i: (b, i, 0))
    per_b = pl.BlockSpec((1, 1, d), lambda b, i: (b, 0, 0))
    full = lambda a: pl.BlockSpec(a.shape, lambda b, i: (0,) * a.ndim)
    rw = jnp.pad(router_w, ((0, 0), (0, LANES - N_EXPERTS))).astype(BF16)
    rb = router_bias.astype(F32).reshape(N_EXPERTS, 1)
    g2 = g.reshape(1, d)
    return pl.pallas_call(
        _outproj_kernel, grid=(bsz, s // tm),
        in_specs=[tok(y_ssd.shape[-1]), tok(y_mla.shape[-1]), tok(y_moba.shape[-1]), tok(d),
                  full(w_out), per_b, full(g2), per_b, per_b, full(rw), full(rb)],
        out_specs=[tok(d), tok(d), pl.BlockSpec((1, 8, tm), lambda b, i: (b, 0, i))],
        out_shape=[jax.ShapeDtypeStruct((bsz, s, d), F32), jax.ShapeDtypeStruct((bsz, s, d), BF16),
                   jax.ShapeDtypeStruct((bsz, 8, s), F32)],
        compiler_params=_params("arbitrary", "arbitrary"), name="out_proj_router",
    )(y_ssd, y_mla, y_moba, x, w_out, gm, g2, sc, sh, rw, rb)


def _moe_kernel(ea_ref, eb_ref, nv_ref, x_ref, w_ref, gua_ref, gub_ref, da_ref, db_ref, o_ref):
    i = pl.program_id(0)

    @pl.when(nv_ref[i] > 0)
    def _():
        xb = x_ref[...]
        out = None
        for gu_ref, d_ref, col in ((gua_ref, da_ref, 0), (gub_ref, db_ref, 1)):
            gu = _dot(xb, gu_ref[0])
            hid = (_silu(gu[:, :D_EXPERT]) * gu[:, D_EXPERT:]).astype(BF16)
            y = _dot(hid, d_ref[0]) * w_ref[:, col:col + 1]
            out = y if out is None else out + y
        o_ref[...] = out.astype(o_ref.dtype)

    @pl.when(nv_ref[i] <= 0)
    def _():
        o_ref[...] = jnp.zeros_like(o_ref)


def _moe(xs, ws, tile_ea, tile_eb, tile_nv, w_gu, w_down):
    nslot, d = xs.shape
    nt = nslot // MOE_TILE
    grid_spec = pltpu.PrefetchScalarGridSpec(
        num_scalar_prefetch=3, grid=(nt,),
        in_specs=[pl.BlockSpec((MOE_TILE, d), lambda i, ea, eb, nv: (i, 0)),
                  pl.BlockSpec((MOE_TILE, ws.shape[1]), lambda i, ea, eb, nv: (i, 0)),
                  pl.BlockSpec((1, d, 2 * D_EXPERT), lambda i, ea, eb, nv: (ea[i], 0, 0)),
                  pl.BlockSpec((1, d, 2 * D_EXPERT), lambda i, ea, eb, nv: (eb[i], 0, 0)),
                  pl.BlockSpec((1, D_EXPERT, d), lambda i, ea, eb, nv: (ea[i], 0, 0)),
                  pl.BlockSpec((1, D_EXPERT, d), lambda i, ea, eb, nv: (eb[i], 0, 0))],
        out_specs=pl.BlockSpec((MOE_TILE, d), lambda i, ea, eb, nv: (i, 0)))
    return pl.pallas_call(
        _moe_kernel, grid_spec=grid_spec,
        out_shape=jax.ShapeDtypeStruct((nslot, d), BF16),
        compiler_params=_params("arbitrary"), name="moe_ffn",
    )(tile_ea, tile_eb, tile_nv, xs, ws, w_gu, w_gu, w_down, w_down)


def _bucket_layout(bucket):
    t = bucket.shape[0]
    nslot = t + N_BUCKETS * MOE_TILE
    nt = nslot // MOE_TILE
    order = jnp.argsort(bucket, stable=True).astype(jnp.int32)
    counts = jnp.sum(bucket[:, None] == jnp.arange(N_BUCKETS)[None, :], axis=0).astype(jnp.int32)
    starts = jnp.cumsum(counts) - counts
    padded = ((counts + MOE_TILE - 1) // MOE_TILE) * MOE_TILE
    pends = jnp.cumsum(padded)
    pstarts = pends - padded
    sorted_bucket = bucket[order]
    slot = pstarts[sorted_bucket] + jnp.arange(t, dtype=jnp.int32) - starts[sorted_bucket]
    slot_token = jnp.zeros((nslot,), jnp.int32).at[slot].set(order)
    token_slot = jnp.zeros((t,), jnp.int32).at[order].set(slot)
    tile_start = jnp.arange(nt, dtype=jnp.int32) * MOE_TILE
    tile_bucket = jnp.sum(tile_start[:, None] >= pends[None, :], axis=1).astype(jnp.int32)
    live = tile_bucket < N_BUCKETS
    tb = jnp.minimum(tile_bucket, N_BUCKETS - 1)
    tile_nv = jnp.where(live, jnp.clip(pstarts[tb] + counts[tb] - tile_start, 0, MOE_TILE), 0)
    last_live = jnp.max(jnp.where(live, tb, 0))
    tb = jnp.where(live, tb, last_live)
    grp, pair = tb // N_PAIRS, tb % N_PAIRS
    tile_ea = grp * EXPERTS_PER_GROUP + jnp.asarray(PAIR_LO, jnp.int32)[pair]
    tile_eb = grp * EXPERTS_PER_GROUP + jnp.asarray(PAIR_HI, jnp.int32)[pair]
    return slot_token, token_slot, tile_ea.astype(jnp.int32), tile_eb.astype(jnp.int32), \
        tile_nv.astype(jnp.int32)


def _final_kernel(x_ref, moe_ref, gf_ref, o_ref):
    o_ref[0] = x_ref[0] + gf_ref[0] * moe_ref[0].astype(F32)


def _final_residual(x, moe, gf, tm):
    bsz, s, d = x.shape
    tok = pl.BlockSpec((1, tm, d), lambda b, i: (b, i, 0))
    return pl.pallas_call(
        _final_kernel, grid=(bsz, s // tm),
        in_specs=[tok, tok, pl.BlockSpec((1, 1, d), lambda b, i: (b, 0, 0))],
        out_specs=tok, out_shape=jax.ShapeDtypeStruct((bsz, s, d), F32),
        compiler_params=_params("arbitrary", "arbitrary"), name="final_residual",
    )(x, moe, gf)


def _pad_w_in(w_in):
    d = w_in.shape[0]
    sizes = (SSD_WIDTH, SSD_CONV_DIM, SSD_HEADS, MLA_Q_LORA, MLA_KV_LORA, MLA_ROPE,
             MOBA_HEADS * MOBA_HEAD_DIM, MOBA_HEADS * MOBA_HEAD_DIM, MOBA_HEADS * MOBA_HEAD_DIM)
    offs = np.cumsum((0,) + sizes)
    z, xbc, dtw, cq, ckv, krope, mq, mk, mv = [w_in[:, offs[i]:offs[i + 1]] for i in range(9)]
    zc = lambda n: jnp.zeros((d, n), w_in.dtype)
    kpe = jnp.concatenate([zc(MLA_NOPE), krope, zc(HEAD_PAD - MLA_QK)], axis=1)
    cols = [z, xbc, dtw, zc(LANES - SSD_HEADS), cq, ckv, kpe,
            _pad_heads(mq, MOBA_HEADS, MOBA_HEAD_DIM), _pad_heads(mk, MOBA_HEADS, MOBA_HEAD_DIM), mv]
    return jnp.concatenate(cols, axis=1).astype(BF16)


def kernel(x, c, ada_w, ada_b, norm_mix_g, norm_ffn_g, w_in, ssd_conv_w, ssd_conv_b, ssd_dt_bias,
           ssd_a_log, ssd_d, ssd_norm_g, mla_q_norm_g, mla_w_uq, mla_kv_norm_g, mla_w_ukv, mla_q_gain,
           mla_k_gain, mla_out_g, moba_q_gain, moba_k_gain, moba_out_g, rel_bias, w_out, router_w,
           router_bias, moe_w_gate, moe_w_up, moe_w_down):
    bsz, s, d = x.shape
    depth = ada_w.shape[0]
    tm = min(512, s)
    nb = s // MOBA_BLOCK
    mod = _modulation(c, ada_w, ada_b).reshape(depth, bsz, 6, 1, d)
    b0_moba, b1_moba = _moba_bias_tiles(rel_bias)
    b0_mla = _causal_tile(MLA_HEADS)
    moe_prev, gf_prev = None, None
    for l in range(depth):
        sh_m, sc_m, g_m, sh_f, sc_f, g_f = [mod[l, :, i] for i in range(6)]
        x, (z, xbc, dt, mla_in, moba_in) = _inproj(
            x, moe_prev, gf_prev, norm_mix_g[l], sc_m, sh_m, _pad_w_in(w_in[l]), tm)
        y_ssd = _ssd(z, xbc, dt, ssd_conv_w[l], ssd_conv_b[l], ssd_dt_bias[l], ssd_a_log[l],
                     ssd_d[l], ssd_norm_g[l])
        q, k, v = _mla_prep(mla_in, mla_q_norm_g[l], mla_w_uq[l], mla_kv_norm_g[l], mla_w_ukv[l],
                            mla_q_gain[l], mla_k_gain[l], tm)
        y_mla = _attention(q, k, v, b0_mla, mla_out_g[l])
        mq, mk, mv, kmean = _moba_prep(moba_in, moba_q_gain[l], moba_k_gain[l])
        km = kmean.reshape(bsz, nb, MOBA_HEADS, HEAD_PAD).transpose(0, 2, 1, 3)
        kmr = jnp.pad(km, ((0, 0), (0, 0), (GATE_LANE0, HEAD_PAD - GATE_LANE0 - nb), (0, 0))).astype(BF16)
        y_moba = _attention(mq, mk, mv, b0_moba, moba_out_g[l], b1_moba, kmr)
        x, hf, route = _outproj(y_ssd, y_mla, y_moba, x, w_out[l].astype(BF16), g_m, norm_ffn_g[l],
                                sc_f, sh_f, router_w, router_bias, tm)
        bucket = route[:, 0, :].reshape(-1).astype(jnp.int32)
        slot_token, token_slot, tile_ea, tile_eb, tile_nv = _bucket_layout(bucket)
        ws = jnp.stack([route[:, 1, :].reshape(-1), route[:, 2, :].reshape(-1)], axis=1)
        xs = hf.reshape(-1, d)[slot_token]
        w_gu = jnp.concatenate([moe_w_gate[l], moe_w_up[l]], axis=-1).astype(BF16)
        ys = _moe(xs, ws[slot_token], tile_ea, tile_eb, tile_nv, w_gu, moe_w_down[l].astype(BF16))
        moe_prev = ys[token_slot].reshape(bsz, s, d)
        gf_prev = g_f
    return _final_residual(x, moe_prev, gf_prev, tm)
```

```python
import functools
import math

import numpy as np
import jax
import jax.numpy as jnp
from jax import lax
from jax.experimental import pallas as pl
from jax.experimental.pallas import tpu as pltpu

D_MODEL = 1024
DEPTH = 2
SSD_HEADS = 8
SSD_HEAD_DIM = 64
SSD_WIDTH = SSD_HEADS * SSD_HEAD_DIM
SSD_GROUPS = 2
SSD_STATE = 128
SSD_CONV = 4
SSD_CHUNK = 128
SSD_CONV_DIM = SSD_WIDTH + 2 * SSD_GROUPS * SSD_STATE
MLA_HEADS = 4
MLA_Q_LORA = 256
MLA_KV_LORA = 128
MLA_NOPE = 64
MLA_ROPE = 32
MLA_V = 64
MLA_QK = MLA_NOPE + MLA_ROPE
ROPE_BASE = 10000.0
MOBA_HEADS = 4
MOBA_HEAD_DIM = 64
MOBA_BLOCK = 256
MOBA_TOPK = 3
REL_BUCKETS = 32
REL_MAX_DIST = 128
N_EXPERTS = 16
N_EXPERT_GROUPS = 4
EXPERTS_PER_GROUP = 4
D_EXPERT = 512
EPS = 1e-6

LANES = 128
HEAD_PAD = 128
ATT_BLOCK = 256
ATT_HEADS = 4
V_PAD = 64
VMEM_LIMIT = 56 * 1024 * 1024
MASK_NEG = -1e9
N_PAIRS = 6
N_BUCKETS = N_EXPERT_GROUPS * N_PAIRS
COUNT_ROWS = 32
MOE_TILE = 256
PAIR_LO = (0, 0, 0, 1, 1, 2)
PAIR_HI = (1, 2, 3, 2, 3, 3)

COL_Z = 0
COL_XBC = COL_Z + SSD_WIDTH
COL_DT = COL_XBC + SSD_CONV_DIM
COL_MLA = COL_DT + LANES
MLA_IN_W = MLA_Q_LORA + MLA_KV_LORA + LANES
COL_MOBA = COL_MLA + MLA_IN_W
MOBA_IN_W = 2 * MOBA_HEADS * HEAD_PAD + MOBA_HEADS * MOBA_HEAD_DIM
N_IN_PAD = COL_MOBA + MOBA_IN_W

F32 = jnp.float32
BF16 = jnp.bfloat16


def _params(*sem):
    return pltpu.CompilerParams(dimension_semantics=sem, vmem_limit_bytes=VMEM_LIMIT)


def _dot(a, b):
    return jnp.dot(a, b, preferred_element_type=F32)


def _dot_nt(a, b):
    return lax.dot_general(a, b, (((1,), (1,)), ((), ())), preferred_element_type=F32)


def _dot_exact(a, b):
    return jnp.dot(a, b, preferred_element_type=F32, precision=lax.Precision.HIGHEST)


def _bf16_pieces(v, n):
    pieces = []
    for _ in range(n):
        p = v.astype(BF16)
        pieces.append(p)
        v = v - p.astype(F32)
    return pieces


def _dot_select_rows(sel01, v, n):
    sel = sel01.astype(BF16)
    return sum(_dot(sel, p) for p in _bf16_pieces(v, n))


def _dot_select_cols(v, sel01, n):
    sel = sel01.astype(BF16)
    return sum(_dot(p, sel) for p in _bf16_pieces(v, n))


def _silu(v):
    return v * (1.0 / (1.0 + jnp.exp(-v)))


def _sigmoid(v):
    return 1.0 / (1.0 + jnp.exp(-v))


def _softplus(v):
    return jnp.maximum(v, 0.0) + jnp.log1p(jnp.exp(-jnp.abs(v)))


def _ada_kernel(c_ref, w_ref, b_ref, o_ref):
    c = c_ref[...]
    o_ref[0] = _dot_exact(_silu(c), w_ref[0]) + b_ref[0]


def _modulation(c, ada_w, ada_b):
    depth, d, n = ada_w.shape
    bsz = c.shape[0]
    tn = d
    return pl.pallas_call(
        _ada_kernel,
        grid=(depth, n // tn),
        in_specs=[pl.BlockSpec((bsz, d), lambda l, j: (0, 0)),
                  pl.BlockSpec((1, d, tn), lambda l, j: (l, 0, j)),
                  pl.BlockSpec((1, 1, tn), lambda l, j: (l, 0, j))],
        out_specs=pl.BlockSpec((1, bsz, tn), lambda l, j: (l, 0, j)),
        out_shape=jax.ShapeDtypeStruct((depth, bsz, n), F32),
        compiler_params=_params("arbitrary", "arbitrary"),
        name="ada_modulation",
    )(c, ada_w, ada_b.reshape(depth, 1, n))


def _inproj_body(x, g_ref, sc_ref, sh_ref, w_ref, z_ref, xbc_ref, dt_ref, mla_ref, moba_ref):
    ms = jnp.mean(x * x, axis=-1, keepdims=True)
    hm = (x * lax.rsqrt(ms + EPS) * g_ref[...]) * (1.0 + sc_ref[0]) + sh_ref[0]
    hb = hm.astype(BF16)
    z_ref[0] = _dot(hb, w_ref[:, COL_Z:COL_XBC]).astype(z_ref.dtype)
    xbc_ref[0] = _dot(hb, w_ref[:, COL_XBC:COL_DT]).astype(xbc_ref.dtype)
    dt_ref[0] = _dot(hb, w_ref[:, COL_DT:COL_MLA])
    mla_ref[0] = _dot(hb, w_ref[:, COL_MLA:COL_MOBA]).astype(mla_ref.dtype)
    moba_ref[0] = _dot(hb, w_ref[:, COL_MOBA:N_IN_PAD]).astype(moba_ref.dtype)


def _inproj_kernel(x_ref, g_ref, sc_ref, sh_ref, w_ref, *outs):
    _inproj_body(x_ref[0], g_ref, sc_ref, sh_ref, w_ref, *outs)


def _inproj_res_kernel(x_ref, moe_ref, gf_ref, g_ref, sc_ref, sh_ref, w_ref, xo_ref, *outs):
    x = x_ref[0] + gf_ref[0] * moe_ref[0].astype(F32)
    xo_ref[0] = x
    _inproj_body(x, g_ref, sc_ref, sh_ref, w_ref, *outs)


def _inproj(x, moe_prev, gf_prev, g, sc, sh, w_pad, tm):
    bsz, s, d = x.shape
    tok = lambda w: pl.BlockSpec((1, tm, w), lambda b, i: (b, i, 0))
    per_b = pl.BlockSpec((1, 1, d), lambda b, i: (b, 0, 0))
    out_shapes = [jax.ShapeDtypeStruct((bsz, s, SSD_WIDTH), BF16),
                  jax.ShapeDtypeStruct((bsz, s, SSD_CONV_DIM), BF16),
                  jax.ShapeDtypeStruct((bsz, s, LANES), F32),
                  jax.ShapeDtypeStruct((bsz, s, MLA_IN_W), BF16),
                  jax.ShapeDtypeStruct((bsz, s, MOBA_IN_W), BF16)]
    out_specs = [tok(SSD_WIDTH), tok(SSD_CONV_DIM), tok(LANES), tok(MLA_IN_W), tok(MOBA_IN_W)]
    common_specs = [pl.BlockSpec((1, d), lambda b, i: (0, 0)), per_b, per_b,
                    pl.BlockSpec((d, N_IN_PAD), lambda b, i: (0, 0))]
    common = (g.reshape(1, d), sc, sh, w_pad)
    if moe_prev is None:
        outs = pl.pallas_call(
            _inproj_kernel, grid=(bsz, s // tm),
            in_specs=[tok(d)] + common_specs, out_specs=out_specs, out_shape=out_shapes,
            compiler_params=_params("arbitrary", "arbitrary"), name="in_proj",
        )(x, *common)
        return x, outs
    outs = pl.pallas_call(
        _inproj_res_kernel, grid=(bsz, s // tm),
        in_specs=[tok(d), tok(d), per_b] + common_specs,
        out_specs=[tok(d)] + out_specs,
        out_shape=[jax.ShapeDtypeStruct((bsz, s, d), F32)] + out_shapes,
        compiler_params=_params("arbitrary", "arbitrary"), name="in_proj_res",
    )(x, moe_prev, gf_prev, *common)
    return outs[0], outs[1:]


SSD_STEP_CHUNKS = 2
CONV_HALO = 8


def _ssd_kernel(z_ref, xbc_ref, dt_ref, cw_ref, cb_ref, dtb_ref, alog_ref, dskip_ref, ng_ref,
                expand_ref, o_ref, state_ref, ext_ref):
    L, G, N, P = SSD_CHUNK, SSD_GROUPS, SSD_STATE, SSD_HEAD_DIM
    R = SSD_HEADS // G
    GW = R * P
    first = pl.program_id(1) == 0

    @pl.when(first)
    def _():
        state_ref[...] = jnp.zeros_like(state_ref)
        ext_ref[0:CONV_HALO, :] = jnp.zeros((CONV_HALO, SSD_CONV_DIM), F32)

    row = lax.broadcasted_iota(jnp.int32, (L, L), 0)
    col = lax.broadcasted_iota(jnp.int32, (L, L), 1)
    causal = col <= row
    tril = jnp.where(causal, 1.0, 0.0).astype(BF16)
    a_row = -jnp.exp(alog_ref[...])

    for ck in range(SSD_STEP_CHUNKS):
        r0 = ck * L
        ext_ref[CONV_HALO:CONV_HALO + L, :] = xbc_ref[0, r0:r0 + L, :].astype(F32)
        conv = cb_ref[...]
        for k in range(SSD_CONV):
            off = CONV_HALO - (SSD_CONV - 1) + k
            conv = conv + cw_ref[k:k + 1, :] * ext_ref[off:off + L, :]
        ext_ref[0:CONV_HALO, :] = ext_ref[L:L + CONV_HALO, :]
        xc = _silu(conv)
        xs = xc[:, :SSD_WIDTH]

        dt = _softplus(dt_ref[0, r0:r0 + L, :] + dtb_ref[...])
        a_cum = _dot_select_rows(tril, dt * a_row, 3)
        a_cum_t = a_cum.T
        a_last = a_cum[L - 1:L, :]
        stacked = jnp.concatenate(
            [dt, jnp.exp(a_cum), jnp.exp(a_last - a_cum),
             jnp.broadcast_to(jnp.exp(a_last), (8, LANES))], axis=0)
        wide = _dot_select_cols(stacked, expand_ref[...], 2)
        dt_w, ea_w, te_w, cd_w = wide[0:L], wide[L:2 * L], wide[2 * L:3 * L], wide[3 * L:3 * L + 1]
        xdt = xs * dt_w
        xdt_b = xdt.astype(BF16)
        xw_b = (xdt * te_w).astype(BF16)

        y_parts = []
        for g in range(G):
            bm = xc[:, SSD_WIDTH + g * N:SSD_WIDTH + (g + 1) * N]
            cm = xc[:, SSD_WIDTH + (G + g) * N:SSD_WIDTH + (G + g + 1) * N]
            bm_b, cm_b = bm.astype(BF16), cm.astype(BF16)
            cb = _dot_nt(cm_b, bm_b)
            st = state_ref[g]
            y_off = _dot(cm_b, st.astype(BF16)) * ea_w[:, g * GW:(g + 1) * GW]
            diag = []
            for r in range(R):
                h = g * R + r
                seg = a_cum[:, h:h + 1] - a_cum_t[h:h + 1, :]
                m = (cb * jnp.exp(jnp.where(causal, seg, -jnp.inf))).astype(BF16)
                diag.append(_dot(m, xdt_b[:, h * P:(h + 1) * P]))
            y_parts.append(jnp.concatenate(diag, axis=1) + y_off)
            state_ref[g] = st * cd_w[:, g * GW:(g + 1) * GW] + _dot(
                bm.T.astype(BF16), xw_b[:, g * GW:(g + 1) * GW])

        y = jnp.concatenate(y_parts, axis=1) + xs * dskip_ref[...]
        y = y * _silu(z_ref[0, r0:r0 + L, :].astype(F32))
        outs = []
        for g in range(G):
            yg = y[:, g * GW:(g + 1) * GW]
            outs.append(yg * lax.rsqrt(jnp.mean(yg * yg, axis=-1, keepdims=True) + EPS))
        o_ref[0, r0:r0 + L, :] = (jnp.concatenate(outs, axis=1) * ng_ref[...]).astype(o_ref.dtype)


def _ssd(z, xbc, dt, conv_w, conv_b, dt_bias, a_log, d_skip, norm_g):
    bsz, s, _ = z.shape
    ts = SSD_STEP_CHUNKS * SSD_CHUNK
    pad_h = lambda v: jnp.pad(v.astype(F32), (0, LANES - SSD_HEADS)).reshape(1, LANES)
    expand = jnp.repeat(jnp.eye(LANES, SSD_HEADS, dtype=BF16), SSD_HEAD_DIM, axis=1)
    tok = lambda w: pl.BlockSpec((1, ts, w), lambda b, i: (b, i, 0))
    full = lambda a: pl.BlockSpec(a.shape, lambda b, i: (0,) * a.ndim)
    params = (conv_w, conv_b.reshape(1, -1), pad_h(dt_bias), pad_h(a_log),
              jnp.repeat(d_skip, SSD_HEAD_DIM).reshape(1, -1), norm_g.reshape(1, -1), expand)
    return pl.pallas_call(
        _ssd_kernel, grid=(bsz, s // ts),
        in_specs=[tok(SSD_WIDTH), tok(SSD_CONV_DIM), tok(LANES)] + [full(p) for p in params],
        out_specs=tok(SSD_WIDTH),
        out_shape=jax.ShapeDtypeStruct((bsz, s, SSD_WIDTH), BF16),
        scratch_shapes=[pltpu.VMEM((SSD_GROUPS, SSD_STATE, SSD_WIDTH // SSD_GROUPS), F32),
                        pltpu.VMEM((CONV_HALO + SSD_CHUNK, SSD_CONV_DIM), F32)],
        compiler_params=_params("arbitrary", "arbitrary"), name="ssd_scan",
    )(z, xbc, dt, *params)


def _augment_v_t(v):
    lane = lax.broadcasted_iota(jnp.int32, (v.shape[0], HEAD_PAD), 1)
    tiles = []
    for p in range(v.shape[1] // HEAD_PAD):
        pt = v[:, p * HEAD_PAD:(p + 1) * HEAD_PAD]
        tiles.append(jnp.where(lane < V_PAD, pt, 1.0))
        tiles.append(jnp.where(lane < V_PAD, pltpu.roll(pt, V_PAD, axis=1), 1.0))
    return jnp.concatenate(tiles, axis=1).T


def _rope_mix(v, cos, sin_hi, sin_lo):
    w = v.shape[1]
    half = MLA_ROPE // 2
    return v * cos + pltpu.roll(v, half, axis=1) * sin_hi + pltpu.roll(v, w - half, axis=1) * sin_lo


def _mla_prep_kernel(in_ref, gq_ref, wq_ref, gkv_ref, wkv_ref, cos_ref, shi_ref, slo_ref,
                     qgain_ref, kgain_ref, q_ref, k_ref, v_ref):
    H = MLA_HEADS
    xin = in_ref[0].astype(F32)
    cq = xin[:, :MLA_Q_LORA]
    ckv = xin[:, MLA_Q_LORA:MLA_Q_LORA + MLA_KV_LORA]
    kpe = xin[:, MLA_Q_LORA + MLA_KV_LORA:]
    cqn = cq * lax.rsqrt(jnp.mean(cq * cq, axis=-1, keepdims=True) + EPS) * gq_ref[...]
    ckvn = ckv * lax.rsqrt(jnp.mean(ckv * ckv, axis=-1, keepdims=True) + EPS) * gkv_ref[...]
    q = _dot(cqn.astype(BF16), wq_ref[...])
    kv = _dot(ckvn.astype(BF16), wkv_ref[...])
    cos, shi, slo = cos_ref[...], shi_ref[...], slo_ref[...]
    tile = lambda t: jnp.concatenate([t] * H, axis=1)
    q = _rope_mix(q, tile(cos), tile(shi), tile(slo))
    kpe = _rope_mix(kpe, cos, shi, slo)
    scale = MLA_QK ** -0.5
    for h in range(H):
        sl = slice(h * HEAD_PAD, (h + 1) * HEAD_PAD)
        qh = q[:, sl]
        qn = qh * lax.rsqrt(jnp.sum(qh * qh, axis=-1, keepdims=True) / MLA_QK + EPS)
        q_ref[0, :, sl] = (qn * (qgain_ref[...] * scale)).astype(q_ref.dtype)
        kh = kv[:, sl] + kpe
        kn = kh * lax.rsqrt(jnp.sum(kh * kh, axis=-1, keepdims=True) / MLA_QK + EPS)
        k_ref[0, :, sl] = (kn * kgain_ref[...]).astype(k_ref.dtype)
    v_ref[0] = _augment_v_t(kv[:, H * HEAD_PAD:]).astype(v_ref.dtype)


def _rope_tables(s):
    half = MLA_ROPE // 2
    inv = ROPE_BASE ** (-2.0 * jnp.arange(half, dtype=F32) / MLA_ROPE)
    ang = jnp.arange(s, dtype=F32)[:, None] * inv[None, :]
    cos, sin = jnp.cos(ang), jnp.sin(ang)
    zeros = lambda n: jnp.zeros((s, n), F32)
    cos_t = jnp.concatenate([jnp.ones((s, MLA_NOPE), F32), cos, cos, zeros(HEAD_PAD - MLA_QK)], axis=1)
    sin_hi = jnp.concatenate([zeros(MLA_NOPE + half), sin, zeros(HEAD_PAD - MLA_QK)], axis=1)
    sin_lo = jnp.concatenate([zeros(MLA_NOPE), -sin, zeros(half + HEAD_PAD - MLA_QK)], axis=1)
    return cos_t, sin_hi, sin_lo


def _pad_heads(w, heads, width):
    lead = w.shape[:-1]
    w = w.reshape(*lead, heads, width)
    w = jnp.pad(w, [(0, 0)] * len(lead) + [(0, 0), (0, HEAD_PAD - width)])
    return w.reshape(*lead, heads * HEAD_PAD)


def _mla_prep(mla_in, gq, w_uq, gkv, w_ukv, q_gain, k_gain, tm):
    bsz, s, _ = mla_in.shape
    H = MLA_HEADS
    wq = _pad_heads(w_uq, H, MLA_QK).astype(BF16)
    wkv = w_ukv.reshape(MLA_KV_LORA, H, MLA_NOPE + MLA_V)
    wk = _pad_heads(wkv[:, :, :MLA_NOPE].reshape(MLA_KV_LORA, H * MLA_NOPE), H, MLA_NOPE)
    wv = wkv[:, :, MLA_NOPE:].reshape(MLA_KV_LORA, H * MLA_V)
    wkv_p = jnp.concatenate([wk, wv], axis=1).astype(BF16)
    cos_t, sin_hi, sin_lo = _rope_tables(s)
    pad_g = lambda v: jnp.pad(v, (0, HEAD_PAD - MLA_QK)).reshape(1, HEAD_PAD)
    tok = lambda w: pl.BlockSpec((1, tm, w), lambda b, i: (b, i, 0))
    full = lambda a: pl.BlockSpec(a.shape, lambda b, i: (0,) * a.ndim)
    pos = pl.BlockSpec((tm, HEAD_PAD), lambda b, i: (i, 0))
    gq2, gkv2, qg, kg = gq.reshape(1, -1), gkv.reshape(1, -1), pad_g(q_gain), pad_g(k_gain)
    return pl.pallas_call(
        _mla_prep_kernel, grid=(bsz, s // tm),
        in_specs=[tok(MLA_IN_W), full(gq2), full(wq), full(gkv2), full(wkv_p), pos, pos, pos,
                  full(qg), full(kg)],
        out_specs=[tok(H * HEAD_PAD)] * 2 + [pl.BlockSpec((1, H * HEAD_PAD, tm), lambda b, i: (b, 0, i))],
        out_shape=[jax.ShapeDtypeStruct((bsz, s, H * HEAD_PAD), BF16)] * 2 + [
            jax.ShapeDtypeStruct((bsz, H * HEAD_PAD, s), BF16)],
        compiler_params=_params("arbitrary", "arbitrary"), name="mla_prep",
    )(mla_in, gq2, wq, gkv2, wkv_p, cos_t, sin_hi, sin_lo, qg, kg)


GATE_LANE0 = MOBA_HEAD_DIM


def _moba_prep_kernel(in_ref, qgain_ref, kgain_ref, q_ref, k_ref, v_ref, km_ref):
    H, W = MOBA_HEADS, MOBA_HEADS * HEAD_PAD
    blk = pl.program_id(1)
    lane = lax.broadcasted_iota(jnp.int32, (1, HEAD_PAD), 1)
    tag = (lane == GATE_LANE0 + blk).astype(F32)
    scale = MOBA_HEAD_DIM ** -0.5
    for h in range(H):
        sl = slice(h * HEAD_PAD, (h + 1) * HEAD_PAD)
        qh = in_ref[0, :, sl].astype(F32)
        qn = qh * lax.rsqrt(jnp.sum(qh * qh, axis=-1, keepdims=True) / MOBA_HEAD_DIM + EPS)
        q_ref[0, :, sl] = (qn * (qgain_ref[...] * scale)).astype(q_ref.dtype)
        kh = in_ref[0, :, W + h * HEAD_PAD:W + (h + 1) * HEAD_PAD].astype(F32)
        kn = kh * lax.rsqrt(jnp.sum(kh * kh, axis=-1, keepdims=True) / MOBA_HEAD_DIM + EPS)
        kn = kn * kgain_ref[...]
        km_ref[0, 0, :, sl] = jnp.mean(kn, axis=0, keepdims=True)
        k_ref[0, :, sl] = (kn + tag).astype(k_ref.dtype)
    v_ref[0] = _augment_v_t(in_ref[0, :, 2 * W:].astype(F32)).astype(v_ref.dtype)


def _moba_prep(moba_in, q_gain, k_gain):
    bsz, s, _ = moba_in.shape
    H = MOBA_HEADS
    tm = MOBA_BLOCK
    nb = s // tm
    pad_g = lambda v: jnp.pad(v, (0, HEAD_PAD - MOBA_HEAD_DIM)).reshape(1, HEAD_PAD)
    tok = lambda w: pl.BlockSpec((1, tm, w), lambda b, i: (b, i, 0))
    full = lambda a: pl.BlockSpec(a.shape, lambda b, i: (0,) * a.ndim)
    qg, kg = pad_g(q_gain), pad_g(k_gain)
    return pl.pallas_call(
        _moba_prep_kernel, grid=(bsz, nb),
        in_specs=[tok(MOBA_IN_W), full(qg), full(kg)],
        out_specs=[tok(H * HEAD_PAD)] * 2 + [
            pl.BlockSpec((1, H * HEAD_PAD, tm), lambda b, i: (b, 0, i)),
            pl.BlockSpec((1, 1, 1, H * HEAD_PAD), lambda b, i: (b, i, 0, 0))],
        out_shape=[jax.ShapeDtypeStruct((bsz, s, H * HEAD_PAD), BF16)] * 2 + [
            jax.ShapeDtypeStruct((bsz, H * HEAD_PAD, s), BF16),
            jax.ShapeDtypeStruct((bsz, nb, 1, H * HEAD_PAD), F32)],
        compiler_params=_params("arbitrary", "arbitrary"), name="moba_prep",
    )(moba_in, qg, kg)


def _attn_kernel(*refs, gated):
    if gated:
        q_ref, k_ref, vt_ref, b0_ref, b1_ref, kmr_ref, og_ref, o_ref, m_ref, acc_ref, qt_ref, qpt_ref = refs
    else:
        q_ref, k_ref, vt_ref, b0_ref, og_ref, o_ref, m_ref, acc_ref, qt_ref = refs
        qpt_ref = qt_ref
    T = ATT_BLOCK
    H = ATT_HEADS
    j = pl.program_id(1)
    hsl = lambda h: slice(h * HEAD_PAD, (h + 1) * HEAD_PAD)

    def step(r, rows, qts_ref, bias_ref=None, first=False):
        sts = []
        for h in range(H):
            st = _dot(k_ref[0, pl.ds(r, rows), hsl(h)], qts_ref[h])
            sts.append(st if bias_ref is None else st + bias_ref[h])
        for h in range(H):
            vt = vt_ref[0, hsl(h), pl.ds(r, rows)]
            mx = jnp.max(sts[h], axis=0, keepdims=True)
            if first:
                m_ref[h] = mx
                acc_ref[h] = _dot(vt, jnp.exp(sts[h] - mx).astype(BF16))
            else:
                m_old = m_ref[h]
                m_new = jnp.maximum(m_old, mx)
                m_ref[h] = m_new
                acc_ref[h] = (jnp.exp(m_old - m_new) * acc_ref[h]
                              + _dot(vt, jnp.exp(sts[h] - m_new).astype(BF16)))

    for h in range(H):
        qt = q_ref[0, :, hsl(h)].astype(F32).T
        qt_b = qt.astype(BF16)
        qt_ref[h] = qt_b
        if gated:
            nbp = -(-(k_ref.shape[1] // T) // 8) * 8
            g = _dot(kmr_ref[0, h], qt_b)[GATE_LANE0:GATE_LANE0 + nbp]
            row = lax.broadcasted_iota(jnp.int32, (nbp, T), 0)
            g = jnp.where(row < j, g, -jnp.inf)
            chosen = jnp.zeros((nbp, T), jnp.bool_)
            for _ in range(MOBA_TOPK):
                mx = jnp.max(g, axis=0, keepdims=True)
                hit = (g == mx) & (mx > -jnp.inf)
                first = jnp.min(jnp.where(hit, row, nbp), axis=0, keepdims=True)
                pick = row == first
                chosen = chosen | pick
                g = jnp.where(pick, -jnp.inf, g)
            neg = jnp.where(chosen, 0.0, MASK_NEG)
            qpt_ref[h] = jnp.concatenate(
                [qt[:GATE_LANE0], qt[GATE_LANE0:GATE_LANE0 + nbp] + neg, qt[GATE_LANE0 + nbp:]],
                axis=0).astype(BF16)

    step(pl.multiple_of(j * T, T), T, qt_ref, b0_ref, first=True)
    if gated:
        @pl.when(j >= 1)
        def _():
            step(pl.multiple_of((j - 1) * T, T), T, qpt_ref, b1_ref)
        n_far = jnp.maximum(j - 1, 0)
    else:
        n_far = j

    def pair_body(i, carry):
        step(pl.multiple_of(i * (2 * T), 2 * T), 2 * T, qpt_ref)
        return carry

    lax.fori_loop(0, lax.shift_right_logical(n_far, 1), pair_body, 0)

    @pl.when((n_far & 1) == 1)
    def _():
        step(pl.multiple_of((n_far - 1) * T, T), T, qpt_ref)

    outs = []
    for h in range(H):
        a = acc_ref[h]
        outs.append(a[:V_PAD] * (1.0 / a[V_PAD:]))
    y = jnp.concatenate(outs, axis=0).T
    y = y * lax.rsqrt(jnp.mean(y * y, axis=-1, keepdims=True) + EPS) * og_ref[...]
    o_ref[0] = y.astype(o_ref.dtype)


def _attention(q, k, v, bias0, out_gain, bias1=None, kmr=None):
    bsz, s, w = q.shape
    T = ATT_BLOCK
    gated = kmr is not None
    qspec = pl.BlockSpec((1, T, w), lambda b, i: (b, i, 0))
    seq = lambda a: pl.BlockSpec((1,) + a.shape[1:], lambda b, i: (b,) + (0,) * (a.ndim - 1))
    full = lambda a: pl.BlockSpec(a.shape, lambda b, i: (0,) * a.ndim)
    og = out_gain.reshape(1, -1)
    if gated:
        args = (q, k, v, bias0, bias1, kmr, og)
        specs = [qspec, seq(k), seq(v), full(bias0), full(bias1), seq(kmr), full(og)]
    else:
        args = (q, k, v, bias0, og)
        specs = [qspec, seq(k), seq(v), full(bias0), full(og)]
    wo = ATT_HEADS * V_PAD
    scratch = [pltpu.VMEM((ATT_HEADS, 1, T), F32), pltpu.VMEM((ATT_HEADS, HEAD_PAD, T), F32),
               pltpu.VMEM((ATT_HEADS, HEAD_PAD, T), BF16)]
    if gated:
        scratch.append(pltpu.VMEM((ATT_HEADS, HEAD_PAD, T), BF16))
    return pl.pallas_call(
        functools.partial(_attn_kernel, gated=gated), grid=(bsz, s // T),
        in_specs=specs, out_specs=pl.BlockSpec((1, T, wo), lambda b, i: (b, i, 0)),
        out_shape=jax.ShapeDtypeStruct((bsz, s, wo), BF16),
        scratch_shapes=scratch,
        compiler_params=_params("arbitrary", "arbitrary"),
        name="moba_attention" if gated else "mla_attention",
    )(*args)


def _rel_bucket(n):
    max_exact = REL_BUCKETS // 2
    nf = jnp.maximum(n, max_exact).astype(F32)
    large = max_exact + (jnp.log(nf / max_exact) / math.log(REL_MAX_DIST / max_exact)
                         * (REL_BUCKETS - max_exact)).astype(jnp.int32)
    large = jnp.minimum(large, REL_BUCKETS - 1)
    return jnp.where(n < max_exact, n, large)


def _moba_bias_tiles(rel_bias):
    T = ATT_BLOCK
    i = jnp.arange(T)[:, None]
    c = jnp.arange(T)[None, :]
    bias_t = rel_bias.T
    far = bias_t[:, REL_BUCKETS - 1][:, None, None]
    d0 = i - c
    b0 = jnp.where((d0 >= 0)[None], bias_t[:, _rel_bucket(jnp.maximum(d0, 0))] - far, MASK_NEG)
    b1 = bias_t[:, _rel_bucket(T + i - c)] - far
    return b0.astype(F32).transpose(0, 2, 1), b1.astype(F32).transpose(0, 2, 1)


def _causal_tile(heads):
    T = ATT_BLOCK
    i = jnp.arange(T)[:, None]
    c = jnp.arange(T)[None, :]
    return jnp.broadcast_to(jnp.where(c <= i, 0.0, MASK_NEG).astype(F32).T, (heads, T, T))


def _outproj_kernel(ys_ref, ya_ref, yb_ref, x_ref, w_ref, gm_ref, g_ref, sc_ref, sh_ref, rw_ref,
                    rb_ref, upper_ref, xo_ref, hf_ref, route_ref, counts_out_ref, count_ref):
    w0, w1 = SSD_WIDTH, SSD_WIDTH + MLA_HEADS * MLA_V
    y = (_dot(ys_ref[0], w_ref[0:w0, :]) + _dot(ya_ref[0], w_ref[w0:w1, :])
         + _dot(yb_ref[0], w_ref[w1:, :]))
    x = x_ref[0] + gm_ref[0] * y
    xo_ref[0] = x
    ms = jnp.mean(x * x, axis=-1, keepdims=True)
    hf = (x * lax.rsqrt(ms + EPS) * g_ref[...]) * (1.0 + sc_ref[0]) + sh_ref[0]
    hb = hf.astype(BF16)
    hf_ref[0] = hb
    logits_t = _dot(hb, rw_ref[...]).T
    score = _sigmoid(logits_t[0:N_EXPERTS, :])
    biased = score + rb_ref[...]
    E = EXPERTS_PER_GROUP
    gsum = []
    for g in range(N_EXPERT_GROUPS):
        r = [biased[g * E + e:g * E + e + 1, :] for e in range(E)]
        best = r[0] + r[1]
        for lo, hi in zip(PAIR_LO[1:], PAIR_HI[1:]):
            best = jnp.maximum(best, r[lo] + r[hi])
        gsum.append(best)
    gmax = functools.reduce(jnp.maximum, gsum)
    taken = jnp.zeros_like(gmax, dtype=jnp.bool_)
    gid = jnp.zeros_like(gmax)
    vb = [jnp.zeros_like(gmax) for _ in range(E)]
    for g in range(N_EXPERT_GROUPS):
        is_g = (gsum[g] == gmax) & jnp.logical_not(taken)
        taken = taken | is_g
        gid = jnp.where(is_g, float(g), gid)
        for e in range(E):
            vb[e] = jnp.where(is_g, biased[g * E + e:g * E + e + 1, :], vb[e])
    sel = []
    for e in range(E):
        rank = jnp.zeros_like(gmax)
        for o in range(E):
            if o < e:
                rank = rank + (vb[o] >= vb[e]).astype(F32)
            elif o > e:
                rank = rank + (vb[o] > vb[e]).astype(F32)
        sel.append(rank < 2.0)
    pidx = jnp.zeros_like(gmax)
    for p, (lo, hi) in enumerate(zip(PAIR_LO, PAIR_HI)):
        pidx = jnp.where(sel[lo] & sel[hi], float(p), pidx)
    bucket = gid * N_PAIRS + pidx
    @pl.when((pl.program_id(0) == 0) & (pl.program_id(1) == 0))
    def _():
        count_ref[...] = jnp.zeros_like(count_ref)

    tm = bucket.shape[1]
    brow = lax.broadcasted_iota(jnp.int32, (COUNT_ROWS, tm), 0).astype(F32)
    onehot = jnp.where(brow == bucket, 1.0, 0.0)
    prefix = _dot(onehot.astype(BF16), upper_ref[...])
    base = count_ref[...]
    rank = jnp.sum(onehot * (prefix - 1.0 + base[:, 0:1]), axis=0, keepdims=True)
    count_ref[...] = base + jnp.sum(onehot, axis=1, keepdims=True)
    counts_out_ref[...] = count_ref[...]
    rows = lax.broadcasted_iota(jnp.int32, (8, tm), 0)
    route_ref[0] = jnp.where(rows == 0, bucket, jnp.where(rows == 1, rank, 0.0))


def _outproj(y_ssd, y_mla, y_moba, x, w_out, gm, g, sc, sh, rw, router_bias, tm):
    bsz, s, d = x.shape
    tok = lambda w: pl.BlockSpec((1, tm, w), lambda b, i: (b, i, 0))
    per_b = pl.BlockSpec((1, 1, d), lambda b, i: (b, 0, 0))
    full = lambda a: pl.BlockSpec(a.shape, lambda b, i: (0,) * a.ndim)
    rb = router_bias.astype(F32).reshape(N_EXPERTS, 1)
    g2 = g.reshape(1, d)
    upper = (jnp.arange(tm)[:, None] <= jnp.arange(tm)[None, :]).astype(BF16)
    return pl.pallas_call(
        _outproj_kernel, grid=(bsz, s // tm),
        in_specs=[tok(y_ssd.shape[-1]), tok(y_mla.shape[-1]), tok(y_moba.shape[-1]), tok(d),
                  full(w_out), per_b, full(g2), per_b, per_b, full(rw), full(rb), full(upper)],
        out_specs=[tok(d), tok(d), pl.BlockSpec((1, 8, tm), lambda b, i: (b, 0, i)),
                   pl.BlockSpec((COUNT_ROWS, LANES), lambda b, i: (0, 0))],
        out_shape=[jax.ShapeDtypeStruct((bsz, s, d), F32), jax.ShapeDtypeStruct((bsz, s, d), BF16),
                   jax.ShapeDtypeStruct((bsz, 8, s), F32),
                   jax.ShapeDtypeStruct((COUNT_ROWS, LANES), F32)],
        scratch_shapes=[pltpu.VMEM((COUNT_ROWS, LANES), F32)],
        compiler_params=_params("arbitrary", "arbitrary"), name="out_proj_router",
    )(y_ssd, y_mla, y_moba, x, w_out, gm, g2, sc, sh, rw, rb, upper)


def _moe_kernel(ea_ref, eb_ref, nv_ref, x_ref, rw_ref, gua_ref, gub_ref, da_ref, db_ref, o_ref):
    i = pl.program_id(0)

    @pl.when(nv_ref[i] > 0)
    def _():
        xb = x_ref[...]
        logits = _dot(xb, rw_ref[...])
        lane = lax.broadcasted_iota(jnp.int32, logits.shape, 1)
        pick = lambda e: _sigmoid(jnp.sum(jnp.where(lane == e, logits, 0.0), axis=-1, keepdims=True))
        s_a, s_b = pick(ea_ref[i]), pick(eb_ref[i])
        tot = s_a + s_b
        out = None
        for gu_ref, d_ref, gate in ((gua_ref, da_ref, s_a / tot), (gub_ref, db_ref, s_b / tot)):
            gu = _dot(xb, gu_ref[0])
            hid = (_silu(gu[:, :D_EXPERT]) * gu[:, D_EXPERT:]).astype(BF16)
            y = _dot(hid, d_ref[0]) * gate
            out = y if out is None else out + y
        o_ref[...] = out.astype(o_ref.dtype)

    @pl.when(nv_ref[i] <= 0)
    def _():
        o_ref[...] = jnp.zeros_like(o_ref)


def _moe(xs, rw, tile_ea, tile_eb, tile_nv, w_gu, w_down):
    nslot, d = xs.shape
    nt = nslot // MOE_TILE
    grid_spec = pltpu.PrefetchScalarGridSpec(
        num_scalar_prefetch=3, grid=(nt,),
        in_specs=[pl.BlockSpec((MOE_TILE, d), lambda i, ea, eb, nv: (i, 0)),
                  pl.BlockSpec(rw.shape, lambda i, ea, eb, nv: (0, 0)),
                  pl.BlockSpec((1, d, 2 * D_EXPERT), lambda i, ea, eb, nv: (ea[i], 0, 0)),
                  pl.BlockSpec((1, d, 2 * D_EXPERT), lambda i, ea, eb, nv: (eb[i], 0, 0)),
                  pl.BlockSpec((1, D_EXPERT, d), lambda i, ea, eb, nv: (ea[i], 0, 0)),
                  pl.BlockSpec((1, D_EXPERT, d), lambda i, ea, eb, nv: (eb[i], 0, 0))],
        out_specs=pl.BlockSpec((MOE_TILE, d), lambda i, ea, eb, nv: (i, 0)))
    return pl.pallas_call(
        _moe_kernel, grid_spec=grid_spec,
        out_shape=jax.ShapeDtypeStruct((nslot, d), BF16),
        compiler_params=_params("arbitrary"), name="moe_ffn",
    )(tile_ea, tile_eb, tile_nv, xs, rw, w_gu, w_gu, w_down, w_down)


ROW_MOVE_STEP = 512


def _row_move_kernel(idx_ref, src_ref, *rest, scatter):
    dst_ref, sem = rest[-2], rest[-1]
    i = pl.program_id(0)
    n = pl.num_programs(0)
    base = i * ROW_MOVE_STEP

    def row_copy(src_row, dst_row):
        return pltpu.make_async_copy(src_ref.at[pl.ds(src_row, 1)], dst_ref.at[pl.ds(dst_row, 1)], sem)

    def issue(t, carry):
        other = idx_ref[0, 0, t]
        if scatter:
            row_copy(base + t, other).start()
        else:
            row_copy(other, base + t).start()
        return carry

    def drain(t, carry):
        row_copy(0, 0).wait()
        return carry

    lax.fori_loop(0, ROW_MOVE_STEP, issue, 0, unroll=8)

    @pl.when(i > 0)
    def _():
        lax.fori_loop(0, ROW_MOVE_STEP, drain, 0, unroll=8)

    @pl.when(i == n - 1)
    def _():
        lax.fori_loop(0, ROW_MOVE_STEP, drain, 0, unroll=8)


def _row_move(src, idx, n_dst, scatter):
    n_idx = idx.shape[0]
    d = src.shape[1]
    src = src.reshape(src.shape[0], 1, d)
    steps = n_idx // ROW_MOVE_STEP
    idx3 = idx.reshape(steps, 1, ROW_MOVE_STEP)
    idx_spec = pl.BlockSpec((1, 1, ROW_MOVE_STEP), lambda i: (i, 0, 0), memory_space=pltpu.SMEM)
    any_spec = pl.BlockSpec(memory_space=pl.ANY)
    kern = functools.partial(_row_move_kernel, scatter=scatter)
    common = dict(grid=(steps,), out_specs=any_spec,
                  out_shape=jax.ShapeDtypeStruct((n_dst, 1, d), src.dtype),
                  scratch_shapes=[pltpu.SemaphoreType.DMA(())],
                  compiler_params=_params("arbitrary"))
    if scatter:
        return pl.pallas_call(kern, in_specs=[idx_spec, any_spec, any_spec],
                              input_output_aliases={2: 0}, name="expert_dispatch", **common)(
            idx3, src, jnp.zeros((n_dst, 1, d), src.dtype)).reshape(n_dst, d)
    return pl.pallas_call(kern, in_specs=[idx_spec, any_spec], name="expert_return", **common)(
        idx3, src).reshape(n_dst, d)


def _bucket_layout(bucket, rank, counts, n_tokens):
    nslot = n_tokens + N_BUCKETS * MOE_TILE
    nt = nslot // MOE_TILE
    padded = ((counts + MOE_TILE - 1) // MOE_TILE) * MOE_TILE
    pends = jnp.cumsum(padded)
    pstarts = pends - padded
    onehot = bucket[:, None] == jnp.arange(N_BUCKETS, dtype=jnp.int32)[None, :]
    token_slot = jnp.sum(jnp.where(onehot, pstarts[None, :], 0), axis=1).astype(jnp.int32) + rank
    tile_start = jnp.arange(nt, dtype=jnp.int32) * MOE_TILE
    tile_bucket = jnp.sum(tile_start[:, None] >= pends[None, :], axis=1).astype(jnp.int32)
    live = tile_bucket < N_BUCKETS
    tb = jnp.minimum(tile_bucket, N_BUCKETS - 1)
    tile_nv = jnp.where(live, jnp.clip(pstarts[tb] + counts[tb] - tile_start, 0, MOE_TILE), 0)
    last_live = jnp.max(jnp.where(live, tb, 0))
    tb = jnp.where(live, tb, last_live)
    grp, pair = tb // N_PAIRS, tb % N_PAIRS
    tile_ea = grp * EXPERTS_PER_GROUP + jnp.asarray(PAIR_LO, jnp.int32)[pair]
    tile_eb = grp * EXPERTS_PER_GROUP + jnp.asarray(PAIR_HI, jnp.int32)[pair]
    return nslot, token_slot, tile_ea.astype(jnp.int32), tile_eb.astype(jnp.int32), tile_nv.astype(jnp.int32)


def _final_kernel(x_ref, moe_ref, gf_ref, o_ref):
    o_ref[0] = x_ref[0] + gf_ref[0] * moe_ref[0].astype(F32)


def _final_residual(x, moe, gf, tm):
    bsz, s, d = x.shape
    tok = pl.BlockSpec((1, tm, d), lambda b, i: (b, i, 0))
    return pl.pallas_call(
        _final_kernel, grid=(bsz, s // tm),
        in_specs=[tok, tok, pl.BlockSpec((1, 1, d), lambda b, i: (b, 0, 0))],
        out_specs=tok, out_shape=jax.ShapeDtypeStruct((bsz, s, d), F32),
        compiler_params=_params("arbitrary", "arbitrary"), name="final_residual",
    )(x, moe, gf)


def _pad_w_in(w_in):
    d = w_in.shape[0]
    sizes = (SSD_WIDTH, SSD_CONV_DIM, SSD_HEADS, MLA_Q_LORA, MLA_KV_LORA, MLA_ROPE,
             MOBA_HEADS * MOBA_HEAD_DIM, MOBA_HEADS * MOBA_HEAD_DIM, MOBA_HEADS * MOBA_HEAD_DIM)
    offs = np.cumsum((0,) + sizes)
    z, xbc, dtw, cq, ckv, krope, mq, mk, mv = [w_in[:, offs[i]:offs[i + 1]] for i in range(9)]
    zc = lambda n: jnp.zeros((d, n), w_in.dtype)
    kpe = jnp.concatenate([zc(MLA_NOPE), krope, zc(HEAD_PAD - MLA_QK)], axis=1)
    cols = [z, xbc, dtw, zc(LANES - SSD_HEADS), cq, ckv, kpe,
            _pad_heads(mq, MOBA_HEADS, MOBA_HEAD_DIM), _pad_heads(mk, MOBA_HEADS, MOBA_HEAD_DIM), mv]
    return jnp.concatenate(cols, axis=1).astype(BF16)


def kernel(x, c, ada_w, ada_b, norm_mix_g, norm_ffn_g, w_in, ssd_conv_w, ssd_conv_b, ssd_dt_bias,
           ssd_a_log, ssd_d, ssd_norm_g, mla_q_norm_g, mla_w_uq, mla_kv_norm_g, mla_w_ukv, mla_q_gain,
           mla_k_gain, mla_out_g, moba_q_gain, moba_k_gain, moba_out_g, rel_bias, w_out, router_w,
           router_bias, moe_w_gate, moe_w_up, moe_w_down):
    bsz, s, d = x.shape
    depth = ada_w.shape[0]
    tm = min(512, s)
    nb = s // MOBA_BLOCK
    mod = _modulation(c, ada_w, ada_b).reshape(depth, bsz, 6, 1, d)
    b0_moba, b1_moba = _moba_bias_tiles(rel_bias)
    b0_mla = _causal_tile(MLA_HEADS)
    rw = jnp.pad(router_w, ((0, 0), (0, LANES - N_EXPERTS))).astype(BF16)
    moe_prev, gf_prev = None, None
    for l in range(depth):
        sh_m, sc_m, g_m, sh_f, sc_f, g_f = [mod[l, :, i] for i in range(6)]
        x, (z, xbc, dt, mla_in, moba_in) = _inproj(
            x, moe_prev, gf_prev, norm_mix_g[l], sc_m, sh_m, _pad_w_in(w_in[l]), tm)
        y_ssd = _ssd(z, xbc, dt, ssd_conv_w[l], ssd_conv_b[l], ssd_dt_bias[l], ssd_a_log[l],
                     ssd_d[l], ssd_norm_g[l])
        q, k, v = _mla_prep(mla_in, mla_q_norm_g[l], mla_w_uq[l], mla_kv_norm_g[l], mla_w_ukv[l],
                            mla_q_gain[l], mla_k_gain[l], tm)
        y_mla = _attention(q, k, v, b0_mla, mla_out_g[l])
        mq, mk, mv, kmean = _moba_prep(moba_in, moba_q_gain[l], moba_k_gain[l])
        km = kmean.reshape(bsz, nb, MOBA_HEADS, HEAD_PAD).transpose(0, 2, 1, 3)
        kmr = jnp.pad(km, ((0, 0), (0, 0), (GATE_LANE0, HEAD_PAD - GATE_LANE0 - nb), (0, 0))).astype(BF16)
        y_moba = _attention(mq, mk, mv, b0_moba, moba_out_g[l], b1_moba, kmr)
        x, hf, route, counts = _outproj(y_ssd, y_mla, y_moba, x, w_out[l].astype(BF16), g_m,
                                        norm_ffn_g[l], sc_f, sh_f, rw, router_bias, tm)
        bucket = route[:, 0, :].reshape(-1).astype(jnp.int32)
        rank = route[:, 1, :].reshape(-1).astype(jnp.int32)
        nslot, token_slot, tile_ea, tile_eb, tile_nv = _bucket_layout(
            bucket, rank, counts[:N_BUCKETS, 0].astype(jnp.int32), bsz * s)
        xs = _row_move(hf.reshape(-1, d), token_slot, nslot, scatter=True)
        w_gu = jnp.concatenate([moe_w_gate[l], moe_w_up[l]], axis=-1).astype(BF16)
        ys = _moe(xs, rw, tile_ea, tile_eb, tile_nv, w_gu, moe_w_down[l].astype(BF16))
        moe_prev = _row_move(ys, token_slot, bsz * s, scatter=False).reshape(bsz, s, d)
        gf_prev = g_f
    return _final_residual(x, moe_prev, gf_prev, tm)
```

```python
import functools
import math

import numpy as np
import jax
import jax.numpy as jnp
from jax import lax
from jax.experimental import pallas as pl
from jax.experimental.pallas import tpu as pltpu

D_MODEL = 1024
DEPTH = 2
SSD_HEADS = 8
SSD_HEAD_DIM = 64
SSD_WIDTH = SSD_HEADS * SSD_HEAD_DIM
SSD_GROUPS = 2
SSD_STATE = 128
SSD_CONV = 4
SSD_CHUNK = 128
SSD_CONV_DIM = SSD_WIDTH + 2 * SSD_GROUPS * SSD_STATE
MLA_HEADS = 4
MLA_Q_LORA = 256
MLA_KV_LORA = 128
MLA_NOPE = 64
MLA_ROPE = 32
MLA_V = 64
MLA_QK = MLA_NOPE + MLA_ROPE
ROPE_BASE = 10000.0
MOBA_HEADS = 4
MOBA_HEAD_DIM = 64
MOBA_BLOCK = 256
MOBA_TOPK = 3
REL_BUCKETS = 32
REL_MAX_DIST = 128
N_EXPERTS = 16
N_EXPERT_GROUPS = 4
EXPERTS_PER_GROUP = 4
D_EXPERT = 512
EPS = 1e-6

LANES = 128
HEAD_PAD = 128
ATT_BLOCK = 256
ATT_HEADS = 4
V_PAD = 64
VMEM_LIMIT = 56 * 1024 * 1024
MASK_NEG = -1e9
N_PAIRS = 6
N_BUCKETS = N_EXPERT_GROUPS * N_PAIRS
COUNT_ROWS = 32
MOE_TILE = 256
PAIR_LO = (0, 0, 0, 1, 1, 2)
PAIR_HI = (1, 2, 3, 2, 3, 3)

COL_Z = 0
COL_XBC = COL_Z + SSD_WIDTH
COL_DT = COL_XBC + SSD_CONV_DIM
COL_MLA = COL_DT + LANES
MLA_IN_W = MLA_Q_LORA + MLA_KV_LORA + LANES
COL_MOBA = COL_MLA + MLA_IN_W
MOBA_IN_W = 2 * MOBA_HEADS * HEAD_PAD + MOBA_HEADS * MOBA_HEAD_DIM
N_IN_PAD = COL_MOBA + MOBA_IN_W

F32 = jnp.float32
BF16 = jnp.bfloat16


def _params(*sem):
    return pltpu.CompilerParams(dimension_semantics=sem, vmem_limit_bytes=VMEM_LIMIT)


def _dot(a, b):
    return jnp.dot(a, b, preferred_element_type=F32)


def _dot_nt(a, b):
    return lax.dot_general(a, b, (((1,), (1,)), ((), ())), preferred_element_type=F32)


def _dot_exact(a, b):
    return jnp.dot(a, b, preferred_element_type=F32, precision=lax.Precision.HIGHEST)


def _bf16_pieces(v, n):
    pieces = []
    for _ in range(n):
        p = v.astype(BF16)
        pieces.append(p)
        v = v - p.astype(F32)
    return pieces


def _dot_select_rows(sel01, v, n):
    sel = sel01.astype(BF16)
    return sum(_dot(sel, p) for p in _bf16_pieces(v, n))


def _dot_select_cols(v, sel01, n):
    sel = sel01.astype(BF16)
    return sum(_dot(p, sel) for p in _bf16_pieces(v, n))


def _silu(v):
    return v * (1.0 / (1.0 + jnp.exp(-v)))


def _sigmoid(v):
    return 1.0 / (1.0 + jnp.exp(-v))


def _softplus(v):
    return jnp.maximum(v, 0.0) + jnp.log1p(jnp.exp(-jnp.abs(v)))


def _ada_kernel(c_ref, w_ref, b_ref, o_ref):
    c = c_ref[...]
    o_ref[0] = _dot_exact(_silu(c), w_ref[0]) + b_ref[0]


def _modulation(c, ada_w, ada_b):
    depth, d, n = ada_w.shape
    bsz = c.shape[0]
    tn = d
    return pl.pallas_call(
        _ada_kernel,
        grid=(depth, n // tn),
        in_specs=[pl.BlockSpec((bsz, d), lambda l, j: (0, 0)),
                  pl.BlockSpec((1, d, tn), lambda l, j: (l, 0, j)),
                  pl.BlockSpec((1, 1, tn), lambda l, j: (l, 0, j))],
        out_specs=pl.BlockSpec((1, bsz, tn), lambda l, j: (l, 0, j)),
        out_shape=jax.ShapeDtypeStruct((depth, bsz, n), F32),
        compiler_params=_params("arbitrary", "arbitrary"),
        name="ada_modulation",
    )(c, ada_w, ada_b.reshape(depth, 1, n))


def _inproj_body(x, g_ref, sc_ref, sh_ref, w_ref, z_ref, xbc_ref, dt_ref, mla_ref, moba_ref):
    ms = jnp.mean(x * x, axis=-1, keepdims=True)
    hm = (x * lax.rsqrt(ms + EPS) * g_ref[...]) * (1.0 + sc_ref[0]) + sh_ref[0]
    hb = hm.astype(BF16)
    z_ref[0] = _dot(hb, w_ref[:, COL_Z:COL_XBC]).astype(z_ref.dtype)
    xbc_ref[0] = _dot(hb, w_ref[:, COL_XBC:COL_DT]).astype(xbc_ref.dtype)
    dt_ref[0] = _dot(hb, w_ref[:, COL_DT:COL_MLA])
    mla_ref[0] = _dot(hb, w_ref[:, COL_MLA:COL_MOBA]).astype(mla_ref.dtype)
    moba_ref[0] = _dot(hb, w_ref[:, COL_MOBA:N_IN_PAD]).astype(moba_ref.dtype)


def _inproj_kernel(x_ref, g_ref, sc_ref, sh_ref, w_ref, *outs):
    _inproj_body(x_ref[0], g_ref, sc_ref, sh_ref, w_ref, *outs)


def _inproj_res_kernel(x_ref, moe_ref, gf_ref, g_ref, sc_ref, sh_ref, w_ref, xo_ref, *outs):
    x = x_ref[0] + gf_ref[0] * moe_ref[0].astype(F32)
    xo_ref[0] = x
    _inproj_body(x, g_ref, sc_ref, sh_ref, w_ref, *outs)


def _inproj(x, moe_prev, gf_prev, g, sc, sh, w_pad, tm):
    bsz, s, d = x.shape
    tok = lambda w: pl.BlockSpec((1, tm, w), lambda b, i: (b, i, 0))
    per_b = pl.BlockSpec((1, 1, d), lambda b, i: (b, 0, 0))
    out_shapes = [jax.ShapeDtypeStruct((bsz, s, SSD_WIDTH), BF16),
                  jax.ShapeDtypeStruct((bsz, s, SSD_CONV_DIM), BF16),
                  jax.ShapeDtypeStruct((bsz, s, LANES), F32),
                  jax.ShapeDtypeStruct((bsz, s, MLA_IN_W), BF16),
                  jax.ShapeDtypeStruct((bsz, s, MOBA_IN_W), BF16)]
    out_specs = [tok(SSD_WIDTH), tok(SSD_CONV_DIM), tok(LANES), tok(MLA_IN_W), tok(MOBA_IN_W)]
    common_specs = [pl.BlockSpec((1, d), lambda b, i: (0, 0)), per_b, per_b,
                    pl.BlockSpec((d, N_IN_PAD), lambda b, i: (0, 0))]
    common = (g.reshape(1, d), sc, sh, w_pad)
    if moe_prev is None:
        outs = pl.pallas_call(
            _inproj_kernel, grid=(bsz, s // tm),
            in_specs=[tok(d)] + common_specs, out_specs=out_specs, out_shape=out_shapes,
            compiler_params=_params("arbitrary", "arbitrary"), name="in_proj",
        )(x, *common)
        return x, outs
    outs = pl.pallas_call(
        _inproj_res_kernel, grid=(bsz, s // tm),
        in_specs=[tok(d), tok(d), per_b] + common_specs,
        out_specs=[tok(d)] + out_specs,
        out_shape=[jax.ShapeDtypeStruct((bsz, s, d), F32)] + out_shapes,
        compiler_params=_params("arbitrary", "arbitrary"), name="in_proj_res",
    )(x, moe_prev, gf_prev, *common)
    return outs[0], outs[1:]


SSD_STEP_CHUNKS = 2
CONV_HALO = 8


def _ssd_kernel(z_ref, xbc_ref, dt_ref, cw_ref, cb_ref, dtb_ref, alog_ref, dskip_ref, ng_ref,
                expand_ref, o_ref, state_ref, ext_ref):
    L, G, N, P = SSD_CHUNK, SSD_GROUPS, SSD_STATE, SSD_HEAD_DIM
    R = SSD_HEADS // G
    GW = R * P
    first = pl.program_id(1) == 0

    @pl.when(first)
    def _():
        state_ref[...] = jnp.zeros_like(state_ref)
        ext_ref[0:CONV_HALO, :] = jnp.zeros((CONV_HALO, SSD_CONV_DIM), F32)

    row = lax.broadcasted_iota(jnp.int32, (L, L), 0)
    col = lax.broadcasted_iota(jnp.int32, (L, L), 1)
    causal = col <= row
    tril = jnp.where(causal, 1.0, 0.0).astype(BF16)
    a_row = -jnp.exp(alog_ref[...])

    for ck in range(SSD_STEP_CHUNKS):
        r0 = ck * L
        ext_ref[CONV_HALO:CONV_HALO + L, :] = xbc_ref[0, r0:r0 + L, :].astype(F32)
        conv = cb_ref[...]
        for k in range(SSD_CONV):
            off = CONV_HALO - (SSD_CONV - 1) + k
            conv = conv + cw_ref[k:k + 1, :] * ext_ref[off:off + L, :]
        ext_ref[0:CONV_HALO, :] = ext_ref[L:L + CONV_HALO, :]
        xc = _silu(conv)
        xs = xc[:, :SSD_WIDTH]

        dt = _softplus(dt_ref[0, r0:r0 + L, :] + dtb_ref[...])
        a_cum = _dot_select_rows(tril, dt * a_row, 3)
        a_cum_t = a_cum.T
        a_last = a_cum[L - 1:L, :]
        stacked = jnp.concatenate(
            [dt, jnp.exp(a_cum), jnp.exp(a_last - a_cum),
             jnp.broadcast_to(jnp.exp(a_last), (8, LANES))], axis=0)
        wide = _dot_select_cols(stacked, expand_ref[...], 2)
        dt_w, ea_w, te_w, cd_w = wide[0:L], wide[L:2 * L], wide[2 * L:3 * L], wide[3 * L:3 * L + 1]
        xdt = xs * dt_w
        xdt_b = xdt.astype(BF16)
        xw_b = (xdt * te_w).astype(BF16)

        y_parts = []
        for g in range(G):
            bm = xc[:, SSD_WIDTH + g * N:SSD_WIDTH + (g + 1) * N]
            cm = xc[:, SSD_WIDTH + (G + g) * N:SSD_WIDTH + (G + g + 1) * N]
            bm_b, cm_b = bm.astype(BF16), cm.astype(BF16)
            cb = _dot_nt(cm_b, bm_b)
            st = state_ref[g]
            y_off = _dot(cm_b, st.astype(BF16)) * ea_w[:, g * GW:(g + 1) * GW]
            diag = []
            for r in range(R):
                h = g * R + r
                seg = a_cum[:, h:h + 1] - a_cum_t[h:h + 1, :]
                m = (cb * jnp.exp(jnp.where(causal, seg, -jnp.inf))).astype(BF16)
                diag.append(_dot(m, xdt_b[:, h * P:(h + 1) * P]))
            y_parts.append(jnp.concatenate(diag, axis=1) + y_off)
            state_ref[g] = st * cd_w[:, g * GW:(g + 1) * GW] + _dot(
                bm.T.astype(BF16), xw_b[:, g * GW:(g + 1) * GW])

        y = jnp.concatenate(y_parts, axis=1) + xs * dskip_ref[...]
        y = y * _silu(z_ref[0, r0:r0 + L, :].astype(F32))
        outs = []
        for g in range(G):
            yg = y[:, g * GW:(g + 1) * GW]
            outs.append(yg * lax.rsqrt(jnp.mean(yg * yg, axis=-1, keepdims=True) + EPS))
        o_ref[0, r0:r0 + L, :] = (jnp.concatenate(outs, axis=1) * ng_ref[...]).astype(o_ref.dtype)


def _ssd(z, xbc, dt, conv_w, conv_b, dt_bias, a_log, d_skip, norm_g):
    bsz, s, _ = z.shape
    ts = SSD_STEP_CHUNKS * SSD_CHUNK
    pad_h = lambda v: jnp.pad(v.astype(F32), (0, LANES - SSD_HEADS)).reshape(1, LANES)
    expand = jnp.repeat(jnp.eye(LANES, SSD_HEADS, dtype=BF16), SSD_HEAD_DIM, axis=1)
    tok = lambda w: pl.BlockSpec((1, ts, w), lambda b, i: (b, i, 0))
    full = lambda a: pl.BlockSpec(a.shape, lambda b, i: (0,) * a.ndim)
    params = (conv_w, conv_b.reshape(1, -1), pad_h(dt_bias), pad_h(a_log),
              jnp.repeat(d_skip, SSD_HEAD_DIM).reshape(1, -1), norm_g.reshape(1, -1), expand)
    return pl.pallas_call(
        _ssd_kernel, grid=(bsz, s // ts),
        in_specs=[tok(SSD_WIDTH), tok(SSD_CONV_DIM), tok(LANES)] + [full(p) for p in params],
        out_specs=tok(SSD_WIDTH),
        out_shape=jax.ShapeDtypeStruct((bsz, s, SSD_WIDTH), BF16),
        scratch_shapes=[pltpu.VMEM((SSD_GROUPS, SSD_STATE, SSD_WIDTH // SSD_GROUPS), F32),
                        pltpu.VMEM((CONV_HALO + SSD_CHUNK, SSD_CONV_DIM), F32)],
        compiler_params=_params("arbitrary", "arbitrary"), name="ssd_scan",
    )(z, xbc, dt, *params)


def _augment_v_t(v):
    lane = lax.broadcasted_iota(jnp.int32, (v.shape[0], HEAD_PAD), 1)
    tiles = []
    for p in range(v.shape[1] // HEAD_PAD):
        pt = v[:, p * HEAD_PAD:(p + 1) * HEAD_PAD]
        tiles.append(jnp.where(lane < V_PAD, pt, 1.0))
        tiles.append(jnp.where(lane < V_PAD, pltpu.roll(pt, V_PAD, axis=1), 1.0))
    return jnp.concatenate(tiles, axis=1).T


def _rope_mix(v, cos, sin_hi, sin_lo):
    w = v.shape[1]
    half = MLA_ROPE // 2
    return v * cos + pltpu.roll(v, half, axis=1) * sin_hi + pltpu.roll(v, w - half, axis=1) * sin_lo


def _mla_prep_kernel(in_ref, gq_ref, wq_ref, gkv_ref, wkv_ref, cos_ref, shi_ref, slo_ref,
                     qgain_ref, kgain_ref, q_ref, k_ref, v_ref):
    H = MLA_HEADS
    xin = in_ref[0].astype(F32)
    cq = xin[:, :MLA_Q_LORA]
    ckv = xin[:, MLA_Q_LORA:MLA_Q_LORA + MLA_KV_LORA]
    kpe = xin[:, MLA_Q_LORA + MLA_KV_LORA:]
    cqn = cq * lax.rsqrt(jnp.mean(cq * cq, axis=-1, keepdims=True) + EPS) * gq_ref[...]
    ckvn = ckv * lax.rsqrt(jnp.mean(ckv * ckv, axis=-1, keepdims=True) + EPS) * gkv_ref[...]
    q = _dot(cqn.astype(BF16), wq_ref[...])
    kv = _dot(ckvn.astype(BF16), wkv_ref[...])
    cos, shi, slo = cos_ref[...], shi_ref[...], slo_ref[...]
    tile = lambda t: jnp.concatenate([t] * H, axis=1)
    q = _rope_mix(q, tile(cos), tile(shi), tile(slo))
    kpe = _rope_mix(kpe, cos, shi, slo)
    scale = MLA_QK ** -0.5
    for h in range(H):
        sl = slice(h * HEAD_PAD, (h + 1) * HEAD_PAD)
        qh = q[:, sl]
        qn = qh * lax.rsqrt(jnp.sum(qh * qh, axis=-1, keepdims=True) / MLA_QK + EPS)
        q_ref[0, :, sl] = (qn * (qgain_ref[...] * scale)).astype(q_ref.dtype)
        kh = kv[:, sl] + kpe
        kn = kh * lax.rsqrt(jnp.sum(kh * kh, axis=-1, keepdims=True) / MLA_QK + EPS)
        k_ref[0, :, sl] = (kn * kgain_ref[...]).astype(k_ref.dtype)
    v_ref[0] = _augment_v_t(kv[:, H * HEAD_PAD:]).astype(v_ref.dtype)


def _rope_tables(s):
    half = MLA_ROPE // 2
    inv = ROPE_BASE ** (-2.0 * jnp.arange(half, dtype=F32) / MLA_ROPE)
    ang = jnp.arange(s, dtype=F32)[:, None] * inv[None, :]
    cos, sin = jnp.cos(ang), jnp.sin(ang)
    zeros = lambda n: jnp.zeros((s, n), F32)
    cos_t = jnp.concatenate([jnp.ones((s, MLA_NOPE), F32), cos, cos, zeros(HEAD_PAD - MLA_QK)], axis=1)
    sin_hi = jnp.concatenate([zeros(MLA_NOPE + half), sin, zeros(HEAD_PAD - MLA_QK)], axis=1)
    sin_lo = jnp.concatenate([zeros(MLA_NOPE), -sin, zeros(half + HEAD_PAD - MLA_QK)], axis=1)
    return cos_t, sin_hi, sin_lo


def _pad_heads(w, heads, width):
    lead = w.shape[:-1]
    w = w.reshape(*lead, heads, width)
    w = jnp.pad(w, [(0, 0)] * len(lead) + [(0, 0), (0, HEAD_PAD - width)])
    return w.reshape(*lead, heads * HEAD_PAD)


def _mla_prep(mla_in, gq, w_uq, gkv, w_ukv, q_gain, k_gain, tm):
    bsz, s, _ = mla_in.shape
    H = MLA_HEADS
    wq = _pad_heads(w_uq, H, MLA_QK).astype(BF16)
    wkv = w_ukv.reshape(MLA_KV_LORA, H, MLA_NOPE + MLA_V)
    wk = _pad_heads(wkv[:, :, :MLA_NOPE].reshape(MLA_KV_LORA, H * MLA_NOPE), H, MLA_NOPE)
    wv = wkv[:, :, MLA_NOPE:].reshape(MLA_KV_LORA, H * MLA_V)
    wkv_p = jnp.concatenate([wk, wv], axis=1).astype(BF16)
    cos_t, sin_hi, sin_lo = _rope_tables(s)
    pad_g = lambda v: jnp.pad(v, (0, HEAD_PAD - MLA_QK)).reshape(1, HEAD_PAD)
    tok = lambda w: pl.BlockSpec((1, tm, w), lambda b, i: (b, i, 0))
    full = lambda a: pl.BlockSpec(a.shape, lambda b, i: (0,) * a.ndim)
    pos = pl.BlockSpec((tm, HEAD_PAD), lambda b, i: (i, 0))
    gq2, gkv2, qg, kg = gq.reshape(1, -1), gkv.reshape(1, -1), pad_g(q_gain), pad_g(k_gain)
    return pl.pallas_call(
        _mla_prep_kernel, grid=(bsz, s // tm),
        in_specs=[tok(MLA_IN_W), full(gq2), full(wq), full(gkv2), full(wkv_p), pos, pos, pos,
                  full(qg), full(kg)],
        out_specs=[tok(H * HEAD_PAD)] * 2 + [pl.BlockSpec((1, H * HEAD_PAD, tm), lambda b, i: (b, 0, i))],
        out_shape=[jax.ShapeDtypeStruct((bsz, s, H * HEAD_PAD), BF16)] * 2 + [
            jax.ShapeDtypeStruct((bsz, H * HEAD_PAD, s), BF16)],
        compiler_params=_params("arbitrary", "arbitrary"), name="mla_prep",
    )(mla_in, gq2, wq, gkv2, wkv_p, cos_t, sin_hi, sin_lo, qg, kg)


GATE_LANE0 = MOBA_HEAD_DIM


def _moba_prep_kernel(in_ref, qgain_ref, kgain_ref, q_ref, k_ref, v_ref, km_ref):
    H, W = MOBA_HEADS, MOBA_HEADS * HEAD_PAD
    blk = pl.program_id(1)
    lane = lax.broadcasted_iota(jnp.int32, (1, HEAD_PAD), 1)
    tag = (lane == GATE_LANE0 + blk).astype(F32)
    scale = MOBA_HEAD_DIM ** -0.5
    for h in range(H):
        sl = slice(h * HEAD_PAD, (h + 1) * HEAD_PAD)
        qh = in_ref[0, :, sl].astype(F32)
        qn = qh * lax.rsqrt(jnp.sum(qh * qh, axis=-1, keepdims=True) / MOBA_HEAD_DIM + EPS)
        q_ref[0, :, sl] = (qn * (qgain_ref[...] * scale)).astype(q_ref.dtype)
        kh = in_ref[0, :, W + h * HEAD_PAD:W + (h + 1) * HEAD_PAD].astype(F32)
        kn = kh * lax.rsqrt(jnp.sum(kh * kh, axis=-1, keepdims=True) / MOBA_HEAD_DIM + EPS)
        kn = kn * kgain_ref[...]
        km_ref[0, 0, :, sl] = jnp.mean(kn, axis=0, keepdims=True)
        k_ref[0, :, sl] = (kn + tag).astype(k_ref.dtype)
    v_ref[0] = _augment_v_t(in_ref[0, :, 2 * W:].astype(F32)).astype(v_ref.dtype)


def _moba_prep(moba_in, q_gain, k_gain):
    bsz, s, _ = moba_in.shape
    H = MOBA_HEADS
    tm = MOBA_BLOCK
    nb = s // tm
    pad_g = lambda v: jnp.pad(v, (0, HEAD_PAD - MOBA_HEAD_DIM)).reshape(1, HEAD_PAD)
    tok = lambda w: pl.BlockSpec((1, tm, w), lambda b, i: (b, i, 0))
    full = lambda a: pl.BlockSpec(a.shape, lambda b, i: (0,) * a.ndim)
    qg, kg = pad_g(q_gain), pad_g(k_gain)
    return pl.pallas_call(
        _moba_prep_kernel, grid=(bsz, nb),
        in_specs=[tok(MOBA_IN_W), full(qg), full(kg)],
        out_specs=[tok(H * HEAD_PAD)] * 2 + [
            pl.BlockSpec((1, H * HEAD_PAD, tm), lambda b, i: (b, 0, i)),
            pl.BlockSpec((1, 1, 1, H * HEAD_PAD), lambda b, i: (b, i, 0, 0))],
        out_shape=[jax.ShapeDtypeStruct((bsz, s, H * HEAD_PAD), BF16)] * 2 + [
            jax.ShapeDtypeStruct((bsz, H * HEAD_PAD, s), BF16),
            jax.ShapeDtypeStruct((bsz, nb, 1, H * HEAD_PAD), F32)],
        compiler_params=_params("arbitrary", "arbitrary"), name="moba_prep",
    )(moba_in, qg, kg)


def _attn_kernel(*refs, gated):
    if gated:
        q_ref, k_ref, vt_ref, b0_ref, b1_ref, kmr_ref, og_ref, o_ref, m_ref, acc_ref, qt_ref, qpt_ref = refs
    else:
        q_ref, k_ref, vt_ref, b0_ref, og_ref, o_ref, m_ref, acc_ref, qt_ref = refs
        qpt_ref = qt_ref
    T = ATT_BLOCK
    H = ATT_HEADS
    j = pl.program_id(1)
    hsl = lambda h: slice(h * HEAD_PAD, (h + 1) * HEAD_PAD)

    def step(r, rows, qts_ref, bias_ref=None, first=False):
        sts = []
        for h in range(H):
            st = _dot(k_ref[0, pl.ds(r, rows), hsl(h)], qts_ref[h])
            sts.append(st if bias_ref is None else st + bias_ref[h])
        for h in range(H):
            vt = vt_ref[0, hsl(h), pl.ds(r, rows)]
            mx = jnp.max(sts[h], axis=0, keepdims=True)
            if first:
                m_ref[h] = mx
                acc_ref[h] = _dot(vt, jnp.exp(sts[h] - mx).astype(BF16))
            else:
                m_old = m_ref[h]
                m_new = jnp.maximum(m_old, mx)
                m_ref[h] = m_new
                acc_ref[h] = (jnp.exp(m_old - m_new) * acc_ref[h]
                              + _dot(vt, jnp.exp(sts[h] - m_new).astype(BF16)))

    for h in range(H):
        qt = q_ref[0, :, hsl(h)].astype(F32).T
        qt_b = qt.astype(BF16)
        qt_ref[h] = qt_b
        if gated:
            nbp = -(-(k_ref.shape[1] // T) // 8) * 8
            g = _dot(kmr_ref[0, h], qt_b)[GATE_LANE0:GATE_LANE0 + nbp]
            row = lax.broadcasted_iota(jnp.int32, (nbp, T), 0)
            g = jnp.where(row < j, g, -jnp.inf)
            chosen = jnp.zeros((nbp, T), jnp.bool_)
            for _ in range(MOBA_TOPK):
                mx = jnp.max(g, axis=0, keepdims=True)
                hit = (g == mx) & (mx > -jnp.inf)
                first = jnp.min(jnp.where(hit, row, nbp), axis=0, keepdims=True)
                pick = row == first
                chosen = chosen | pick
                g = jnp.where(pick, -jnp.inf, g)
            neg = jnp.where(chosen, 0.0, MASK_NEG)
            qpt_ref[h] = jnp.concatenate(
                [qt[:GATE_LANE0], qt[GATE_LANE0:GATE_LANE0 + nbp] + neg, qt[GATE_LANE0 + nbp:]],
                axis=0).astype(BF16)

    step(pl.multiple_of(j * T, T), T, qt_ref, b0_ref, first=True)
    if gated:
        @pl.when(j >= 1)
        def _():
            step(pl.multiple_of((j - 1) * T, T), T, qpt_ref, b1_ref)
        n_far = jnp.maximum(j - 1, 0)
    else:
        n_far = j

    def pair_body(i, carry):
        step(pl.multiple_of(i * (2 * T), 2 * T), 2 * T, qpt_ref)
        return carry

    lax.fori_loop(0, lax.shift_right_logical(n_far, 1), pair_body, 0)

    @pl.when((n_far & 1) == 1)
    def _():
        step(pl.multiple_of((n_far - 1) * T, T), T, qpt_ref)

    outs = []
    for h in range(H):
        a = acc_ref[h]
        outs.append(a[:V_PAD] * (1.0 / a[V_PAD:]))
    y = jnp.concatenate(outs, axis=0).T
    y = y * lax.rsqrt(jnp.mean(y * y, axis=-1, keepdims=True) + EPS) * og_ref[...]
    o_ref[0] = y.astype(o_ref.dtype)


def _attention(q, k, v, bias0, out_gain, bias1=None, kmr=None):
    bsz, s, w = q.shape
    T = ATT_BLOCK
    gated = kmr is not None
    qspec = pl.BlockSpec((1, T, w), lambda b, i: (b, i, 0))
    seq = lambda a: pl.BlockSpec((1,) + a.shape[1:], lambda b, i: (b,) + (0,) * (a.ndim - 1))
    full = lambda a: pl.BlockSpec(a.shape, lambda b, i: (0,) * a.ndim)
    og = out_gain.reshape(1, -1)
    if gated:
        args = (q, k, v, bias0, bias1, kmr, og)
        specs = [qspec, seq(k), seq(v), full(bias0), full(bias1), seq(kmr), full(og)]
    else:
        args = (q, k, v, bias0, og)
        specs = [qspec, seq(k), seq(v), full(bias0), full(og)]
    wo = ATT_HEADS * V_PAD
    scratch = [pltpu.VMEM((ATT_HEADS, 1, T), F32), pltpu.VMEM((ATT_HEADS, HEAD_PAD, T), F32),
               pltpu.VMEM((ATT_HEADS, HEAD_PAD, T), BF16)]
    if gated:
        scratch.append(pltpu.VMEM((ATT_HEADS, HEAD_PAD, T), BF16))
    return pl.pallas_call(
        functools.partial(_attn_kernel, gated=gated), grid=(bsz, s // T),
        in_specs=specs, out_specs=pl.BlockSpec((1, T, wo), lambda b, i: (b, i, 0)),
        out_shape=jax.ShapeDtypeStruct((bsz, s, wo), BF16),
        scratch_shapes=scratch,
        compiler_params=_params("arbitrary", "arbitrary"),
        name="moba_attention" if gated else "mla_attention",
    )(*args)


def _rel_bucket(n):
    max_exact = REL_BUCKETS // 2
    nf = jnp.maximum(n, max_exact).astype(F32)
    large = max_exact + (jnp.log(nf / max_exact) / math.log(REL_MAX_DIST / max_exact)
                         * (REL_BUCKETS - max_exact)).astype(jnp.int32)
    large = jnp.minimum(large, REL_BUCKETS - 1)
    return jnp.where(n < max_exact, n, large)


def _moba_bias_tiles(rel_bias):
    T = ATT_BLOCK
    i = jnp.arange(T)[:, None]
    c = jnp.arange(T)[None, :]
    bias_t = rel_bias.T
    far = bias_t[:, REL_BUCKETS - 1][:, None, None]
    d0 = i - c
    b0 = jnp.where((d0 >= 0)[None], bias_t[:, _rel_bucket(jnp.maximum(d0, 0))] - far, MASK_NEG)
    b1 = bias_t[:, _rel_bucket(T + i - c)] - far
    return b0.astype(F32).transpose(0, 2, 1), b1.astype(F32).transpose(0, 2, 1)


def _causal_tile(heads):
    T = ATT_BLOCK
    i = jnp.arange(T)[:, None]
    c = jnp.arange(T)[None, :]
    return jnp.broadcast_to(jnp.where(c <= i, 0.0, MASK_NEG).astype(F32).T, (heads, T, T))


def _outproj_kernel(ys_ref, ya_ref, yb_ref, x_ref, w_ref, gm_ref, g_ref, sc_ref, sh_ref, rw_ref,
                    rb_ref, upper_ref, xo_ref, hf_ref, route_ref, counts_out_ref, count_ref):
    w0, w1 = SSD_WIDTH, SSD_WIDTH + MLA_HEADS * MLA_V
    y = (_dot(ys_ref[0], w_ref[0:w0, :]) + _dot(ya_ref[0], w_ref[w0:w1, :])
         + _dot(yb_ref[0], w_ref[w1:, :]))
    x = x_ref[0] + gm_ref[0] * y
    xo_ref[0] = x
    ms = jnp.mean(x * x, axis=-1, keepdims=True)
    hf = (x * lax.rsqrt(ms + EPS) * g_ref[...]) * (1.0 + sc_ref[0]) + sh_ref[0]
    hb = hf.astype(BF16)
    hf_ref[0] = hb
    logits_t = _dot(hb, rw_ref[...]).T
    score = _sigmoid(logits_t[0:N_EXPERTS, :])
    biased = score + rb_ref[...]
    E = EXPERTS_PER_GROUP
    gsum = []
    for g in range(N_EXPERT_GROUPS):
        r = [biased[g * E + e:g * E + e + 1, :] for e in range(E)]
        best = r[0] + r[1]
        for lo, hi in zip(PAIR_LO[1:], PAIR_HI[1:]):
            best = jnp.maximum(best, r[lo] + r[hi])
        gsum.append(best)
    gmax = functools.reduce(jnp.maximum, gsum)
    taken = jnp.zeros_like(gmax, dtype=jnp.bool_)
    gid = jnp.zeros_like(gmax)
    vb = [jnp.zeros_like(gmax) for _ in range(E)]
    for g in range(N_EXPERT_GROUPS):
        is_g = (gsum[g] == gmax) & jnp.logical_not(taken)
        taken = taken | is_g
        gid = jnp.where(is_g, float(g), gid)
        for e in range(E):
            vb[e] = jnp.where(is_g, biased[g * E + e:g * E + e + 1, :], vb[e])
    sel = []
    for e in range(E):
        rank = jnp.zeros_like(gmax)
        for o in range(E):
            if o < e:
                rank = rank + (vb[o] >= vb[e]).astype(F32)
            elif o > e:
                rank = rank + (vb[o] > vb[e]).astype(F32)
        sel.append(rank < 2.0)
    pidx = jnp.zeros_like(gmax)
    for p, (lo, hi) in enumerate(zip(PAIR_LO, PAIR_HI)):
        pidx = jnp.where(sel[lo] & sel[hi], float(p), pidx)
    bucket = gid * N_PAIRS + pidx
    @pl.when((pl.program_id(0) == 0) & (pl.program_id(1) == 0))
    def _():
        count_ref[...] = jnp.zeros_like(count_ref)

    tm = bucket.shape[1]
    brow = lax.broadcasted_iota(jnp.int32, (COUNT_ROWS, tm), 0).astype(F32)
    onehot = jnp.where(brow == bucket, 1.0, 0.0)
    prefix = _dot(onehot.astype(BF16), upper_ref[...])
    base = count_ref[...]
    rank = jnp.sum(onehot * (prefix - 1.0 + base[:, 0:1]), axis=0, keepdims=True)
    count_ref[...] = base + jnp.sum(onehot, axis=1, keepdims=True)
    counts_out_ref[...] = count_ref[...]
    rows = lax.broadcasted_iota(jnp.int32, (8, tm), 0)
    route_ref[0] = jnp.where(rows == 0, bucket, jnp.where(rows == 1, rank, 0.0))


def _outproj(y_ssd, y_mla, y_moba, x, w_out, gm, g, sc, sh, rw, router_bias, tm):
    bsz, s, d = x.shape
    tok = lambda w: pl.BlockSpec((1, tm, w), lambda b, i: (b, i, 0))
    per_b = pl.BlockSpec((1, 1, d), lambda b, i: (b, 0, 0))
    full = lambda a: pl.BlockSpec(a.shape, lambda b, i: (0,) * a.ndim)
    rb = router_bias.astype(F32).reshape(N_EXPERTS, 1)
    g2 = g.reshape(1, d)
    upper = (jnp.arange(tm)[:, None] <= jnp.arange(tm)[None, :]).astype(BF16)
    return pl.pallas_call(
        _outproj_kernel, grid=(bsz, s // tm),
        in_specs=[tok(y_ssd.shape[-1]), tok(y_mla.shape[-1]), tok(y_moba.shape[-1]), tok(d),
                  full(w_out), per_b, full(g2), per_b, per_b, full(rw), full(rb), full(upper)],
        out_specs=[tok(d), tok(d), pl.BlockSpec((1, 8, tm), lambda b, i: (b, 0, i)),
                   pl.BlockSpec((COUNT_ROWS, LANES), lambda b, i: (0, 0))],
        out_shape=[jax.ShapeDtypeStruct((bsz, s, d), F32), jax.ShapeDtypeStruct((bsz, s, d), BF16),
                   jax.ShapeDtypeStruct((bsz, 8, s), F32),
                   jax.ShapeDtypeStruct((COUNT_ROWS, LANES), F32)],
        scratch_shapes=[pltpu.VMEM((COUNT_ROWS, LANES), F32)],
        compiler_params=_params("arbitrary", "arbitrary"), name="out_proj_router",
    )(y_ssd, y_mla, y_moba, x, w_out, gm, g2, sc, sh, rw, rb, upper)


def _moe_kernel(ea_ref, eb_ref, nv_ref, x_ref, rw_ref, gua_ref, gub_ref, da_ref, db_ref, o_ref):
    i = pl.program_id(0)

    @pl.when(nv_ref[i] > 0)
    def _():
        xb = x_ref[...]
        logits = _dot(xb, rw_ref[...])
        lane = lax.broadcasted_iota(jnp.int32, logits.shape, 1)
        pick = lambda e: _sigmoid(jnp.sum(jnp.where(lane == e, logits, 0.0), axis=-1, keepdims=True))
        s_a, s_b = pick(ea_ref[i]), pick(eb_ref[i])
        tot = s_a + s_b
        out = None
        for gu_ref, d_ref, gate in ((gua_ref, da_ref, s_a / tot), (gub_ref, db_ref, s_b / tot)):
            gu = _dot(xb, gu_ref[0])
            hid = (_silu(gu[:, :D_EXPERT]) * gu[:, D_EXPERT:]).astype(BF16)
            y = _dot(hid, d_ref[0]) * gate
            out = y if out is None else out + y
        o_ref[...] = out.astype(o_ref.dtype)

    @pl.when(nv_ref[i] <= 0)
    def _():
        o_ref[...] = jnp.zeros_like(o_ref)


def _moe(xs, rw, tile_ea, tile_eb, tile_nv, w_gu, w_down):
    nslot, d = xs.shape
    nt = nslot // MOE_TILE
    grid_spec = pltpu.PrefetchScalarGridSpec(
        num_scalar_prefetch=3, grid=(nt,),
        in_specs=[pl.BlockSpec((MOE_TILE, d), lambda i, ea, eb, nv: (i, 0)),
                  pl.BlockSpec(rw.shape, lambda i, ea, eb, nv: (0, 0)),
                  pl.BlockSpec((1, d, 2 * D_EXPERT), lambda i, ea, eb, nv: (ea[i], 0, 0)),
                  pl.BlockSpec((1, d, 2 * D_EXPERT), lambda i, ea, eb, nv: (eb[i], 0, 0)),
                  pl.BlockSpec((1, D_EXPERT, d), lambda i, ea, eb, nv: (ea[i], 0, 0)),
                  pl.BlockSpec((1, D_EXPERT, d), lambda i, ea, eb, nv: (eb[i], 0, 0))],
        out_specs=pl.BlockSpec((MOE_TILE, d), lambda i, ea, eb, nv: (i, 0)))
    return pl.pallas_call(
        _moe_kernel, grid_spec=grid_spec,
        out_shape=jax.ShapeDtypeStruct((nslot, d), BF16),
        compiler_params=_params("arbitrary"), name="moe_ffn",
    )(tile_ea, tile_eb, tile_nv, xs, rw, w_gu, w_gu, w_down, w_down)


def _bucket_layout(bucket, rank, counts, n_tokens):
    nslot = n_tokens + N_BUCKETS * MOE_TILE
    nt = nslot // MOE_TILE
    padded = ((counts + MOE_TILE - 1) // MOE_TILE) * MOE_TILE
    pends = jnp.cumsum(padded)
    pstarts = pends - padded
    onehot = bucket[:, None] == jnp.arange(N_BUCKETS, dtype=jnp.int32)[None, :]
    token_slot = jnp.sum(jnp.where(onehot, pstarts[None, :], 0), axis=1).astype(jnp.int32) + rank
    tile_start = jnp.arange(nt, dtype=jnp.int32) * MOE_TILE
    tile_bucket = jnp.sum(tile_start[:, None] >= pends[None, :], axis=1).astype(jnp.int32)
    live = tile_bucket < N_BUCKETS
    tb = jnp.minimum(tile_bucket, N_BUCKETS - 1)
    tile_nv = jnp.where(live, jnp.clip(pstarts[tb] + counts[tb] - tile_start, 0, MOE_TILE), 0)
    order = jnp.argsort(token_slot).astype(jnp.int32)
    starts = jnp.cumsum(counts) - counts
    shift = jnp.repeat(starts[tb] - pstarts[tb], MOE_TILE)
    slot_token = order[jnp.clip(jnp.arange(nslot, dtype=jnp.int32) + shift, 0, n_tokens - 1)]
    last_live = jnp.max(jnp.where(live, tb, 0))
    tb = jnp.where(live, tb, last_live)
    grp, pair = tb // N_PAIRS, tb % N_PAIRS
    tile_ea = grp * EXPERTS_PER_GROUP + jnp.asarray(PAIR_LO, jnp.int32)[pair]
    tile_eb = grp * EXPERTS_PER_GROUP + jnp.asarray(PAIR_HI, jnp.int32)[pair]
    return slot_token, token_slot, tile_ea.astype(jnp.int32), tile_eb.astype(jnp.int32), \
        tile_nv.astype(jnp.int32)


def _final_kernel(x_ref, moe_ref, gf_ref, o_ref):
    o_ref[0] = x_ref[0] + gf_ref[0] * moe_ref[0].astype(F32)


def _final_residual(x, moe, gf, tm):
    bsz, s, d = x.shape
    tok = pl.BlockSpec((1, tm, d), lambda b, i: (b, i, 0))
    return pl.pallas_call(
        _final_kernel, grid=(bsz, s // tm),
        in_specs=[tok, tok, pl.BlockSpec((1, 1, d), lambda b, i: (b, 0, 0))],
        out_specs=tok, out_shape=jax.ShapeDtypeStruct((bsz, s, d), F32),
        compiler_params=_params("arbitrary", "arbitrary"), name="final_residual",
    )(x, moe, gf)


def _pad_w_in(w_in):
    d = w_in.shape[0]
    sizes = (SSD_WIDTH, SSD_CONV_DIM, SSD_HEADS, MLA_Q_LORA, MLA_KV_LORA, MLA_ROPE,
             MOBA_HEADS * MOBA_HEAD_DIM, MOBA_HEADS * MOBA_HEAD_DIM, MOBA_HEADS * MOBA_HEAD_DIM)
    offs = np.cumsum((0,) + sizes)
    z, xbc, dtw, cq, ckv, krope, mq, mk, mv = [w_in[:, offs[i]:offs[i + 1]] for i in range(9)]
    zc = lambda n: jnp.zeros((d, n), w_in.dtype)
    kpe = jnp.concatenate([zc(MLA_NOPE), krope, zc(HEAD_PAD - MLA_QK)], axis=1)
    cols = [z, xbc, dtw, zc(LANES - SSD_HEADS), cq, ckv, kpe,
            _pad_heads(mq, MOBA_HEADS, MOBA_HEAD_DIM), _pad_heads(mk, MOBA_HEADS, MOBA_HEAD_DIM), mv]
    return jnp.concatenate(cols, axis=1).astype(BF16)


def kernel(x, c, ada_w, ada_b, norm_mix_g, norm_ffn_g, w_in, ssd_conv_w, ssd_conv_b, ssd_dt_bias,
           ssd_a_log, ssd_d, ssd_norm_g, mla_q_norm_g, mla_w_uq, mla_kv_norm_g, mla_w_ukv, mla_q_gain,
           mla_k_gain, mla_out_g, moba_q_gain, moba_k_gain, moba_out_g, rel_bias, w_out, router_w,
           router_bias, moe_w_gate, moe_w_up, moe_w_down):
    bsz, s, d = x.shape
    depth = ada_w.shape[0]
    tm = min(512, s)
    nb = s // MOBA_BLOCK
    mod = _modulation(c, ada_w, ada_b).reshape(depth, bsz, 6, 1, d)
    b0_moba, b1_moba = _moba_bias_tiles(rel_bias)
    b0_mla = _causal_tile(MLA_HEADS)
    rw = jnp.pad(router_w, ((0, 0), (0, LANES - N_EXPERTS))).astype(BF16)
    moe_prev, gf_prev = None, None
    for l in range(depth):
        sh_m, sc_m, g_m, sh_f, sc_f, g_f = [mod[l, :, i] for i in range(6)]
        x, (z, xbc, dt, mla_in, moba_in) = _inproj(
            x, moe_prev, gf_prev, norm_mix_g[l], sc_m, sh_m, _pad_w_in(w_in[l]), tm)
        y_ssd = _ssd(z, xbc, dt, ssd_conv_w[l], ssd_conv_b[l], ssd_dt_bias[l], ssd_a_log[l],
                     ssd_d[l], ssd_norm_g[l])
        q, k, v = _mla_prep(mla_in, mla_q_norm_g[l], mla_w_uq[l], mla_kv_norm_g[l], mla_w_ukv[l],
                            mla_q_gain[l], mla_k_gain[l], tm)
        y_mla = _attention(q, k, v, b0_mla, mla_out_g[l])
        mq, mk, mv, kmean = _moba_prep(moba_in, moba_q_gain[l], moba_k_gain[l])
        km = kmean.reshape(bsz, nb, MOBA_HEADS, HEAD_PAD).transpose(0, 2, 1, 3)
        kmr = jnp.pad(km, ((0, 0), (0, 0), (GATE_LANE0, HEAD_PAD - GATE_LANE0 - nb), (0, 0))).astype(BF16)
        y_moba = _attention(mq, mk, mv, b0_moba, moba_out_g[l], b1_moba, kmr)
        x, hf, route, counts = _outproj(y_ssd, y_mla, y_moba, x, w_out[l].astype(BF16), g_m,
                                        norm_ffn_g[l], sc_f, sh_f, rw, router_bias, tm)
        bucket = route[:, 0, :].reshape(-1).astype(jnp.int32)
        rank = route[:, 1, :].reshape(-1).astype(jnp.int32)
        slot_token, token_slot, tile_ea, tile_eb, tile_nv = _bucket_layout(
            bucket, rank, counts[:N_BUCKETS, 0].astype(jnp.int32), bsz * s)
        xs = hf.reshape(-1, d)[slot_token]
        w_gu = jnp.concatenate([moe_w_gate[l], moe_w_up[l]], axis=-1).astype(BF16)
        ys = _moe(xs, rw, tile_ea, tile_eb, tile_nv, w_gu, moe_w_down[l].astype(BF16))
        moe_prev = ys[token_slot].reshape(bsz, s, d)
        gf_prev = g_f
    return _final_residual(x, moe_prev, gf_prev, tm)
```

```python
import functools
import math

import numpy as np
import jax
import jax.numpy as jnp
from jax import lax
from jax.experimental import pallas as pl
from jax.experimental.pallas import tpu as pltpu

D_MODEL = 1024
DEPTH = 2
SSD_HEADS = 8
SSD_HEAD_DIM = 64
SSD_WIDTH = SSD_HEADS * SSD_HEAD_DIM
SSD_GROUPS = 2
SSD_STATE = 128
SSD_CONV = 4
SSD_CHUNK = 128
SSD_CONV_DIM = SSD_WIDTH + 2 * SSD_GROUPS * SSD_STATE
MLA_HEADS = 4
MLA_Q_LORA = 256
MLA_KV_LORA = 128
MLA_NOPE = 64
MLA_ROPE = 32
MLA_V = 64
MLA_QK = MLA_NOPE + MLA_ROPE
ROPE_BASE = 10000.0
MOBA_HEADS = 4
MOBA_HEAD_DIM = 64
MOBA_BLOCK = 256
MOBA_TOPK = 3
REL_BUCKETS = 32
REL_MAX_DIST = 128
N_EXPERTS = 16
N_EXPERT_GROUPS = 4
EXPERTS_PER_GROUP = 4
D_EXPERT = 512
EPS = 1e-6

LANES = 128
HEAD_PAD = 128
ATT_BLOCK = 256
ATT_HEADS = 4
V_PAD = 64
VMEM_LIMIT = 56 * 1024 * 1024
MASK_NEG = -1e9
N_PAIRS = 6
N_BUCKETS = N_EXPERT_GROUPS * N_PAIRS
COUNT_ROWS = 32
MOE_TILE = 256
PAIR_LO = (0, 0, 0, 1, 1, 2)
PAIR_HI = (1, 2, 3, 2, 3, 3)

COL_Z = 0
COL_XBC = COL_Z + SSD_WIDTH
COL_DT = COL_XBC + SSD_CONV_DIM
COL_MLA = COL_DT + LANES
MLA_IN_W = MLA_Q_LORA + MLA_KV_LORA + LANES
COL_MOBA = COL_MLA + MLA_IN_W
MOBA_IN_W = 2 * MOBA_HEADS * HEAD_PAD + MOBA_HEADS * MOBA_HEAD_DIM
N_IN_PAD = COL_MOBA + MOBA_IN_W

F32 = jnp.float32
BF16 = jnp.bfloat16


def _params(*sem):
    return pltpu.CompilerParams(dimension_semantics=sem, vmem_limit_bytes=VMEM_LIMIT)


def _dot(a, b):
    return jnp.dot(a, b, preferred_element_type=F32)


def _dot_nt(a, b):
    return lax.dot_general(a, b, (((1,), (1,)), ((), ())), preferred_element_type=F32)


def _dot_exact(a, b):
    return jnp.dot(a, b, preferred_element_type=F32, precision=lax.Precision.HIGHEST)


def _bf16_pieces(v, n):
    pieces = []
    for _ in range(n):
        p = v.astype(BF16)
        pieces.append(p)
        v = v - p.astype(F32)
    return pieces


def _dot_select_rows(sel01, v, n):
    sel = sel01.astype(BF16)
    return sum(_dot(sel, p) for p in _bf16_pieces(v, n))


def _dot_select_cols(v, sel01, n):
    sel = sel01.astype(BF16)
    return sum(_dot(p, sel) for p in _bf16_pieces(v, n))


def _silu(v):
    return v * (1.0 / (1.0 + jnp.exp(-v)))


def _sigmoid(v):
    return 1.0 / (1.0 + jnp.exp(-v))


def _softplus(v):
    return jnp.maximum(v, 0.0) + jnp.log1p(jnp.exp(-jnp.abs(v)))


def _ada_kernel(c_ref, w_ref, b_ref, o_ref):
    c = c_ref[...]
    o_ref[0] = _dot_exact(_silu(c), w_ref[0]) + b_ref[0]


def _modulation(c, ada_w, ada_b):
    depth, d, n = ada_w.shape
    bsz = c.shape[0]
    tn = d
    return pl.pallas_call(
        _ada_kernel,
        grid=(depth, n // tn),
        in_specs=[pl.BlockSpec((bsz, d), lambda l, j: (0, 0)),
                  pl.BlockSpec((1, d, tn), lambda l, j: (l, 0, j)),
                  pl.BlockSpec((1, 1, tn), lambda l, j: (l, 0, j))],
        out_specs=pl.BlockSpec((1, bsz, tn), lambda l, j: (l, 0, j)),
        out_shape=jax.ShapeDtypeStruct((depth, bsz, n), F32),
        compiler_params=_params("arbitrary", "arbitrary"),
        name="ada_modulation",
    )(c, ada_w, ada_b.reshape(depth, 1, n))


N_MLA_PARAMS = 9
N_MOBA_PARAMS = 2


def _inproj_body(x, g_ref, sc_ref, sh_ref, w_ref, prep_refs, out_refs):
    z_ref, xbc_ref, dt_ref = out_refs[:3]
    mla_outs, moba_outs = out_refs[3:6], out_refs[6:10]
    mla_params, moba_params = prep_refs[:N_MLA_PARAMS], prep_refs[N_MLA_PARAMS:]
    ms = jnp.mean(x * x, axis=-1, keepdims=True)
    hm = (x * lax.rsqrt(ms + EPS) * g_ref[...]) * (1.0 + sc_ref[0]) + sh_ref[0]
    hb = hm.astype(BF16)
    mla_in = _dot(hb, w_ref[:, COL_MLA:COL_MOBA])
    q, kv, kpe = _mla_up_proj(mla_in, *mla_params[:4])
    moba_in = _dot(hb, w_ref[:, COL_MOBA:N_IN_PAD])
    dt_ref[0] = _dot(hb, w_ref[:, COL_DT:COL_MLA])
    _mla_prep_math(q, kv, kpe, *mla_params[4:], *mla_outs)
    _moba_prep_math(moba_in, *moba_params, *moba_outs)
    z_ref[0] = _dot(hb, w_ref[:, COL_Z:COL_XBC]).astype(z_ref.dtype)
    xbc_ref[0] = _dot(hb, w_ref[:, COL_XBC:COL_DT]).astype(xbc_ref.dtype)


def _inproj_kernel(x_ref, g_ref, sc_ref, sh_ref, w_ref, *refs):
    n = N_MLA_PARAMS + N_MOBA_PARAMS
    _inproj_body(x_ref[0], g_ref, sc_ref, sh_ref, w_ref, refs[:n], refs[n:])


def _inproj_res_kernel(x_ref, moe_ref, gf_ref, g_ref, sc_ref, sh_ref, w_ref, *refs):
    n = N_MLA_PARAMS + N_MOBA_PARAMS
    xo_ref = refs[n]
    x = x_ref[0] + gf_ref[0] * moe_ref[0].astype(F32)
    xo_ref[0] = x
    _inproj_body(x, g_ref, sc_ref, sh_ref, w_ref, refs[:n], refs[n + 1:])


def _inproj(x, moe_prev, gf_prev, g, sc, sh, w_pad, mla_params, moba_params, tm):
    bsz, s, d = x.shape
    nb_tile = tm // MOBA_BLOCK
    hw = ATT_HEADS * HEAD_PAD
    tok = lambda w: pl.BlockSpec((1, tm, w), lambda b, i: (b, i, 0))
    per_b = pl.BlockSpec((1, 1, d), lambda b, i: (b, 0, 0))
    full = lambda a: pl.BlockSpec(a.shape, lambda b, i: (0,) * a.ndim)
    pos = pl.BlockSpec((tm, HEAD_PAD), lambda b, i: (i, 0))
    vt_spec = pl.BlockSpec((1, hw, tm), lambda b, i: (b, 0, i))
    tok_shape = lambda w, dt: jax.ShapeDtypeStruct((bsz, s, w), dt)
    vt_shape = jax.ShapeDtypeStruct((bsz, hw, s), BF16)
    out_shapes = [tok_shape(SSD_WIDTH, BF16), tok_shape(SSD_CONV_DIM, BF16), tok_shape(LANES, F32),
                  tok_shape(hw, BF16), tok_shape(hw, BF16), vt_shape,
                  tok_shape(hw, BF16), tok_shape(hw, BF16), vt_shape,
                  jax.ShapeDtypeStruct((bsz, s // MOBA_BLOCK, 1, hw), F32)]
    out_specs = [tok(SSD_WIDTH), tok(SSD_CONV_DIM), tok(LANES),
                 tok(hw), tok(hw), vt_spec, tok(hw), tok(hw), vt_spec,
                 pl.BlockSpec((1, nb_tile, 1, hw), lambda b, i: (b, i, 0, 0))]
    mla_specs = [full(a) for a in mla_params[:4]] + [pos] * 3 + [full(a) for a in mla_params[7:]]
    prep = tuple(mla_params) + tuple(moba_params)
    prep_specs = mla_specs + [full(a) for a in moba_params]
    common_specs = [pl.BlockSpec((1, d), lambda b, i: (0, 0)), per_b, per_b,
                    pl.BlockSpec((d, N_IN_PAD), lambda b, i: (0, 0))] + prep_specs
    common = (g.reshape(1, d), sc, sh, w_pad) + prep
    if moe_prev is None:
        outs = pl.pallas_call(
            _inproj_kernel, grid=(bsz, s // tm),
            in_specs=[tok(d)] + common_specs, out_specs=out_specs, out_shape=out_shapes,
            compiler_params=_params("arbitrary", "arbitrary"), name="in_proj",
        )(x, *common)
        return x, outs
    outs = pl.pallas_call(
        _inproj_res_kernel, grid=(bsz, s // tm),
        in_specs=[tok(d), tok(d), per_b] + common_specs,
        out_specs=[tok(d)] + out_specs,
        out_shape=[jax.ShapeDtypeStruct((bsz, s, d), F32)] + out_shapes,
        compiler_params=_params("arbitrary", "arbitrary"), name="in_proj_res",
    )(x, moe_prev, gf_prev, *common)
    return outs[0], outs[1:]


SSD_STEP_CHUNKS = 2
CONV_HALO = 8


def _ssd_kernel(z_ref, xbc_ref, dt_ref, cw_ref, cb_ref, dtb_ref, alog_ref, dskip_ref, ng_ref,
                expand_ref, o_ref, state_ref, ext_ref):
    L, G, N, P = SSD_CHUNK, SSD_GROUPS, SSD_STATE, SSD_HEAD_DIM
    R = SSD_HEADS // G
    GW = R * P
    first = pl.program_id(1) == 0

    @pl.when(first)
    def _():
        state_ref[...] = jnp.zeros_like(state_ref)
        ext_ref[0:CONV_HALO, :] = jnp.zeros((CONV_HALO, SSD_CONV_DIM), F32)

    row = lax.broadcasted_iota(jnp.int32, (L, L), 0)
    col = lax.broadcasted_iota(jnp.int32, (L, L), 1)
    causal = col <= row
    tril = jnp.where(causal, 1.0, 0.0).astype(BF16)
    a_row = -jnp.exp(alog_ref[...])

    for ck in range(SSD_STEP_CHUNKS):
        r0 = ck * L
        ext_ref[CONV_HALO:CONV_HALO + L, :] = xbc_ref[0, r0:r0 + L, :].astype(F32)
        conv = cb_ref[...]
        for k in range(SSD_CONV):
            off = CONV_HALO - (SSD_CONV - 1) + k
            conv = conv + cw_ref[k:k + 1, :] * ext_ref[off:off + L, :]
        ext_ref[0:CONV_HALO, :] = ext_ref[L:L + CONV_HALO, :]
        xc = _silu(conv)
        xs = xc[:, :SSD_WIDTH]

        dt = _softplus(dt_ref[0, r0:r0 + L, :] + dtb_ref[...])
        a_cum = _dot_select_rows(tril, dt * a_row, 3)
        a_cum_t = a_cum.T
        a_last = a_cum[L - 1:L, :]
        stacked = jnp.concatenate(
            [dt, jnp.exp(a_cum), jnp.exp(a_last - a_cum),
             jnp.broadcast_to(jnp.exp(a_last), (8, LANES))], axis=0)
        wide = _dot_select_cols(stacked, expand_ref[...], 2)
        dt_w, ea_w, te_w, cd_w = wide[0:L], wide[L:2 * L], wide[2 * L:3 * L], wide[3 * L:3 * L + 1]
        xdt = xs * dt_w
        xdt_b = xdt.astype(BF16)
        xw_b = (xdt * te_w).astype(BF16)

        y_parts = []
        for g in range(G):
            bm = xc[:, SSD_WIDTH + g * N:SSD_WIDTH + (g + 1) * N]
            cm = xc[:, SSD_WIDTH + (G + g) * N:SSD_WIDTH + (G + g + 1) * N]
            bm_b, cm_b = bm.astype(BF16), cm.astype(BF16)
            cb = _dot_nt(cm_b, bm_b)
            st = state_ref[g]
            y_off = _dot(cm_b, st.astype(BF16)) * ea_w[:, g * GW:(g + 1) * GW]
            diag = []
            for r in range(R):
                h = g * R + r
                seg = a_cum[:, h:h + 1] - a_cum_t[h:h + 1, :]
                m = (cb * jnp.exp(jnp.where(causal, seg, -jnp.inf))).astype(BF16)
                diag.append(_dot(m, xdt_b[:, h * P:(h + 1) * P]))
            y_parts.append(jnp.concatenate(diag, axis=1) + y_off)
            state_ref[g] = st * cd_w[:, g * GW:(g + 1) * GW] + _dot(
                bm.T.astype(BF16), xw_b[:, g * GW:(g + 1) * GW])

        y = jnp.concatenate(y_parts, axis=1) + xs * dskip_ref[...]
        y = y * _silu(z_ref[0, r0:r0 + L, :].astype(F32))
        outs = []
        for g in range(G):
            yg = y[:, g * GW:(g + 1) * GW]
            outs.append(yg * lax.rsqrt(jnp.mean(yg * yg, axis=-1, keepdims=True) + EPS))
        o_ref[0, r0:r0 + L, :] = (jnp.concatenate(outs, axis=1) * ng_ref[...]).astype(o_ref.dtype)


def _ssd(z, xbc, dt, conv_w, conv_b, dt_bias, a_log, d_skip, norm_g):
    bsz, s, _ = z.shape
    ts = SSD_STEP_CHUNKS * SSD_CHUNK
    pad_h = lambda v: jnp.pad(v.astype(F32), (0, LANES - SSD_HEADS)).reshape(1, LANES)
    expand = jnp.repeat(jnp.eye(LANES, SSD_HEADS, dtype=BF16), SSD_HEAD_DIM, axis=1)
    tok = lambda w: pl.BlockSpec((1, ts, w), lambda b, i: (b, i, 0))
    full = lambda a: pl.BlockSpec(a.shape, lambda b, i: (0,) * a.ndim)
    params = (conv_w, conv_b.reshape(1, -1), pad_h(dt_bias), pad_h(a_log),
              jnp.repeat(d_skip, SSD_HEAD_DIM).reshape(1, -1), norm_g.reshape(1, -1), expand)
    return pl.pallas_call(
        _ssd_kernel, grid=(bsz, s // ts),
        in_specs=[tok(SSD_WIDTH), tok(SSD_CONV_DIM), tok(LANES)] + [full(p) for p in params],
        out_specs=tok(SSD_WIDTH),
        out_shape=jax.ShapeDtypeStruct((bsz, s, SSD_WIDTH), BF16),
        scratch_shapes=[pltpu.VMEM((SSD_GROUPS, SSD_STATE, SSD_WIDTH // SSD_GROUPS), F32),
                        pltpu.VMEM((CONV_HALO + SSD_CHUNK, SSD_CONV_DIM), F32)],
        compiler_params=_params("arbitrary", "arbitrary"), name="ssd_scan",
    )(z, xbc, dt, *params)


def _augment_v_t(v):
    lane = lax.broadcasted_iota(jnp.int32, (v.shape[0], HEAD_PAD), 1)
    tiles = []
    for p in range(v.shape[1] // HEAD_PAD):
        pt = v[:, p * HEAD_PAD:(p + 1) * HEAD_PAD]
        tiles.append(jnp.where(lane < V_PAD, pt, 1.0))
        tiles.append(jnp.where(lane < V_PAD, pltpu.roll(pt, V_PAD, axis=1), 1.0))
    return jnp.concatenate(tiles, axis=1).T


def _rope_mix(v, cos, sin_hi, sin_lo):
    w = v.shape[1]
    half = MLA_ROPE // 2
    return v * cos + pltpu.roll(v, half, axis=1) * sin_hi + pltpu.roll(v, w - half, axis=1) * sin_lo


def _mla_up_proj(xin, gq_ref, wq_ref, gkv_ref, wkv_ref):
    cq = xin[:, :MLA_Q_LORA]
    ckv = xin[:, MLA_Q_LORA:MLA_Q_LORA + MLA_KV_LORA]
    kpe = xin[:, MLA_Q_LORA + MLA_KV_LORA:]
    cqn = cq * lax.rsqrt(jnp.mean(cq * cq, axis=-1, keepdims=True) + EPS) * gq_ref[...]
    ckvn = ckv * lax.rsqrt(jnp.mean(ckv * ckv, axis=-1, keepdims=True) + EPS) * gkv_ref[...]
    q = _dot(cqn.astype(BF16), wq_ref[...])
    kv = _dot(ckvn.astype(BF16), wkv_ref[...])
    return q, kv, kpe


def _mla_prep_math(q, kv, kpe, cos_ref, shi_ref, slo_ref, qgain_ref, kgain_ref, q_ref, k_ref, v_ref):
    H = MLA_HEADS
    cos, shi, slo = cos_ref[...], shi_ref[...], slo_ref[...]
    tile = lambda t: jnp.concatenate([t] * H, axis=1)
    q = _rope_mix(q, tile(cos), tile(shi), tile(slo))
    kpe = _rope_mix(kpe, cos, shi, slo)
    scale = MLA_QK ** -0.5
    for h in range(H):
        sl = slice(h * HEAD_PAD, (h + 1) * HEAD_PAD)
        qh = q[:, sl]
        qn = qh * lax.rsqrt(jnp.sum(qh * qh, axis=-1, keepdims=True) / MLA_QK + EPS)
        q_ref[0, :, sl] = (qn * (qgain_ref[...] * scale)).astype(q_ref.dtype)
        kh = kv[:, sl] + kpe
        kn = kh * lax.rsqrt(jnp.sum(kh * kh, axis=-1, keepdims=True) / MLA_QK + EPS)
        k_ref[0, :, sl] = (kn * kgain_ref[...]).astype(k_ref.dtype)
    v_ref[0] = _augment_v_t(kv[:, H * HEAD_PAD:]).astype(v_ref.dtype)


def _rope_tables(s):
    half = MLA_ROPE // 2
    inv = ROPE_BASE ** (-2.0 * jnp.arange(half, dtype=F32) / MLA_ROPE)
    ang = jnp.arange(s, dtype=F32)[:, None] * inv[None, :]
    cos, sin = jnp.cos(ang), jnp.sin(ang)
    zeros = lambda n: jnp.zeros((s, n), F32)
    cos_t = jnp.concatenate([jnp.ones((s, MLA_NOPE), F32), cos, cos, zeros(HEAD_PAD - MLA_QK)], axis=1)
    sin_hi = jnp.concatenate([zeros(MLA_NOPE + half), sin, zeros(HEAD_PAD - MLA_QK)], axis=1)
    sin_lo = jnp.concatenate([zeros(MLA_NOPE), -sin, zeros(half + HEAD_PAD - MLA_QK)], axis=1)
    return cos_t, sin_hi, sin_lo


def _pad_heads(w, heads, width):
    lead = w.shape[:-1]
    w = w.reshape(*lead, heads, width)
    w = jnp.pad(w, [(0, 0)] * len(lead) + [(0, 0), (0, HEAD_PAD - width)])
    return w.reshape(*lead, heads * HEAD_PAD)


def _mla_prep_params(s, gq, w_uq, gkv, w_ukv, q_gain, k_gain):
    H = MLA_HEADS
    wq = _pad_heads(w_uq, H, MLA_QK).astype(BF16)
    wkv = w_ukv.reshape(MLA_KV_LORA, H, MLA_NOPE + MLA_V)
    wk = _pad_heads(wkv[:, :, :MLA_NOPE].reshape(MLA_KV_LORA, H * MLA_NOPE), H, MLA_NOPE)
    wv = wkv[:, :, MLA_NOPE:].reshape(MLA_KV_LORA, H * MLA_V)
    wkv_p = jnp.concatenate([wk, wv], axis=1).astype(BF16)
    cos_t, sin_hi, sin_lo = _rope_tables(s)
    pad_g = lambda v: jnp.pad(v, (0, HEAD_PAD - MLA_QK)).reshape(1, HEAD_PAD)
    return (gq.reshape(1, -1), wq, gkv.reshape(1, -1), wkv_p, cos_t, sin_hi, sin_lo,
            pad_g(q_gain), pad_g(k_gain))


GATE_LANE0 = MOBA_HEAD_DIM


def _moba_prep_math(xin, qgain_ref, kgain_ref, q_ref, k_ref, v_ref, km_ref):
    H, W, B = MOBA_HEADS, MOBA_HEADS * HEAD_PAD, MOBA_BLOCK
    tm = xin.shape[0]
    blk0 = pl.program_id(1) * (tm // B)
    lane = lax.broadcasted_iota(jnp.int32, (1, HEAD_PAD), 1)
    scale = MOBA_HEAD_DIM ** -0.5
    for h in range(H):
        sl = slice(h * HEAD_PAD, (h + 1) * HEAD_PAD)
        qh = xin[:, sl]
        qn = qh * lax.rsqrt(jnp.sum(qh * qh, axis=-1, keepdims=True) / MOBA_HEAD_DIM + EPS)
        q_ref[0, :, sl] = (qn * (qgain_ref[...] * scale)).astype(q_ref.dtype)
        kh = xin[:, W + h * HEAD_PAD:W + (h + 1) * HEAD_PAD]
        kn = kh * lax.rsqrt(jnp.sum(kh * kh, axis=-1, keepdims=True) / MOBA_HEAD_DIM + EPS)
        kn = kn * kgain_ref[...]
        for j in range(tm // B):
            rows = slice(j * B, (j + 1) * B)
            tag = (lane == GATE_LANE0 + blk0 + j).astype(F32)
            km_ref[0, j, :, sl] = jnp.mean(kn[rows], axis=0, keepdims=True)
            k_ref[0, rows, sl] = (kn[rows] + tag).astype(k_ref.dtype)
    v_ref[0] = _augment_v_t(xin[:, 2 * W:]).astype(v_ref.dtype)


def _moba_prep_params(q_gain, k_gain):
    pad_g = lambda v: jnp.pad(v, (0, HEAD_PAD - MOBA_HEAD_DIM)).reshape(1, HEAD_PAD)
    return pad_g(q_gain), pad_g(k_gain)


def _attn_kernel(*refs, gated):
    if gated:
        q_ref, k_ref, vt_ref, b0_ref, b1_ref, kmr_ref, og_ref, o_ref, m_ref, acc_ref, qt_ref, qpt_ref = refs
    else:
        q_ref, k_ref, vt_ref, b0_ref, og_ref, o_ref, m_ref, acc_ref, qt_ref = refs
        qpt_ref = qt_ref
    T = ATT_BLOCK
    H = ATT_HEADS
    j = pl.program_id(1)
    hsl = lambda h: slice(h * HEAD_PAD, (h + 1) * HEAD_PAD)

    def step(r, rows, qts_ref, bias_ref=None, first=False):
        sts = []
        for h in range(H):
            st = _dot(k_ref[0, pl.ds(r, rows), hsl(h)], qts_ref[h])
            sts.append(st if bias_ref is None else st + bias_ref[h])
        for h in range(H):
            vt = vt_ref[0, hsl(h), pl.ds(r, rows)]
            mx = jnp.max(sts[h], axis=0, keepdims=True)
            if first:
                m_ref[h] = mx
                acc_ref[h] = _dot(vt, jnp.exp(sts[h] - mx).astype(BF16))
            else:
                m_old = m_ref[h]
                m_new = jnp.maximum(m_old, mx)
                m_ref[h] = m_new
                acc_ref[h] = (jnp.exp(m_old - m_new) * acc_ref[h]
                              + _dot(vt, jnp.exp(sts[h] - m_new).astype(BF16)))

    for h in range(H):
        qt = q_ref[0, :, hsl(h)].astype(F32).T
        qt_b = qt.astype(BF16)
        qt_ref[h] = qt_b
        if gated:
            nbp = -(-(k_ref.shape[1] // T) // 8) * 8
            g = _dot(kmr_ref[0, h], qt_b)[GATE_LANE0:GATE_LANE0 + nbp]
            row = lax.broadcasted_iota(jnp.int32, (nbp, T), 0)
            g = jnp.where(row < j, g, -jnp.inf)
            chosen = jnp.zeros((nbp, T), jnp.bool_)
            for _ in range(MOBA_TOPK):
                mx = jnp.max(g, axis=0, keepdims=True)
                hit = (g == mx) & (mx > -jnp.inf)
                first = jnp.min(jnp.where(hit, row, nbp), axis=0, keepdims=True)
                pick = row == first
                chosen = chosen | pick
                g = jnp.where(pick, -jnp.inf, g)
            neg = jnp.where(chosen, 0.0, MASK_NEG)
            qpt_ref[h] = jnp.concatenate(
                [qt[:GATE_LANE0], qt[GATE_LANE0:GATE_LANE0 + nbp] + neg, qt[GATE_LANE0 + nbp:]],
                axis=0).astype(BF16)

    step(pl.multiple_of(j * T, T), T, qt_ref, b0_ref, first=True)
    if gated:
        @pl.when(j >= 1)
        def _():
            step(pl.multiple_of((j - 1) * T, T), T, qpt_ref, b1_ref)
        n_far = jnp.maximum(j - 1, 0)
    else:
        n_far = j

    def pair_body(i, carry):
        step(pl.multiple_of(i * (2 * T), 2 * T), 2 * T, qpt_ref)
        return carry

    lax.fori_loop(0, lax.shift_right_logical(n_far, 1), pair_body, 0)

    @pl.when((n_far & 1) == 1)
    def _():
        step(pl.multiple_of((n_far - 1) * T, T), T, qpt_ref)

    outs = []
    for h in range(H):
        a = acc_ref[h]
        outs.append(a[:V_PAD] * (1.0 / a[V_PAD:]))
    y = jnp.concatenate(outs, axis=0).T
    y = y * lax.rsqrt(jnp.mean(y * y, axis=-1, keepdims=True) + EPS) * og_ref[...]
    o_ref[0] = y.astype(o_ref.dtype)


def _attention(q, k, v, bias0, out_gain, bias1=None, kmr=None):
    bsz, s, w = q.shape
    T = ATT_BLOCK
    gated = kmr is not None
    qspec = pl.BlockSpec((1, T, w), lambda b, i: (b, i, 0))
    seq = lambda a: pl.BlockSpec((1,) + a.shape[1:], lambda b, i: (b,) + (0,) * (a.ndim - 1))
    full = lambda a: pl.BlockSpec(a.shape, lambda b, i: (0,) * a.ndim)
    og = out_gain.reshape(1, -1)
    if gated:
        args = (q, k, v, bias0, bias1, kmr, og)
        specs = [qspec, seq(k), seq(v), full(bias0), full(bias1), seq(kmr), full(og)]
    else:
        args = (q, k, v, bias0, og)
        specs = [qspec, seq(k), seq(v), full(bias0), full(og)]
    wo = ATT_HEADS * V_PAD
    scratch = [pltpu.VMEM((ATT_HEADS, 1, T), F32), pltpu.VMEM((ATT_HEADS, HEAD_PAD, T), F32),
               pltpu.VMEM((ATT_HEADS, HEAD_PAD, T), BF16)]
    if gated:
        scratch.append(pltpu.VMEM((ATT_HEADS, HEAD_PAD, T), BF16))
    return pl.pallas_call(
        functools.partial(_attn_kernel, gated=gated), grid=(bsz, s // T),
        in_specs=specs, out_specs=pl.BlockSpec((1, T, wo), lambda b, i: (b, i, 0)),
        out_shape=jax.ShapeDtypeStruct((bsz, s, wo), BF16),
        scratch_shapes=scratch,
        compiler_params=_params("arbitrary", "arbitrary"),
        name="moba_attention" if gated else "mla_attention",
    )(*args)


def _rel_bucket(n):
    max_exact = REL_BUCKETS // 2
    nf = jnp.maximum(n, max_exact).astype(F32)
    large = max_exact + (jnp.log(nf / max_exact) / math.log(REL_MAX_DIST / max_exact)
                         * (REL_BUCKETS - max_exact)).astype(jnp.int32)
    large = jnp.minimum(large, REL_BUCKETS - 1)
    return jnp.where(n < max_exact, n, large)


def _moba_bias_tiles(rel_bias):
    T = ATT_BLOCK
    i = jnp.arange(T)[:, None]
    c = jnp.arange(T)[None, :]
    bias_t = rel_bias.T
    far = bias_t[:, REL_BUCKETS - 1][:, None, None]
    d0 = i - c

    def lookup(bucket):
        hit = bucket[None, :, :, None] == jnp.arange(REL_BUCKETS)[None, None, None, :]
        return jnp.sum(jnp.where(hit, bias_t[:, None, None, :], 0.0), axis=-1)

    b0 = jnp.where((d0 >= 0)[None], lookup(_rel_bucket(jnp.maximum(d0, 0))) - far, MASK_NEG)
    b1 = lookup(_rel_bucket(T + i - c)) - far
    return b0.astype(F32).transpose(0, 2, 1), b1.astype(F32).transpose(0, 2, 1)


def _causal_tile(heads):
    T = ATT_BLOCK
    i = jnp.arange(T)[:, None]
    c = jnp.arange(T)[None, :]
    return jnp.broadcast_to(jnp.where(c <= i, 0.0, MASK_NEG).astype(F32).T, (heads, T, T))


def _outproj_kernel(ys_ref, ya_ref, yb_ref, x_ref, w_ref, gm_ref, g_ref, sc_ref, sh_ref, rw_ref,
                    rb_ref, upper_ref, xo_ref, hf_ref, route_ref, counts_out_ref, count_ref):
    w0, w1 = SSD_WIDTH, SSD_WIDTH + MLA_HEADS * MLA_V
    y = (_dot(ys_ref[0], w_ref[0:w0, :]) + _dot(ya_ref[0], w_ref[w0:w1, :])
         + _dot(yb_ref[0], w_ref[w1:, :]))
    x = x_ref[0] + gm_ref[0] * y
    xo_ref[0] = x
    ms = jnp.mean(x * x, axis=-1, keepdims=True)
    hf = (x * lax.rsqrt(ms + EPS) * g_ref[...]) * (1.0 + sc_ref[0]) + sh_ref[0]
    hb = hf.astype(BF16)
    hf_ref[0] = hb
    logits_t = _dot(hb, rw_ref[...]).T
    score = _sigmoid(logits_t[0:N_EXPERTS, :])
    biased = score + rb_ref[...]
    E = EXPERTS_PER_GROUP
    gsum = []
    for g in range(N_EXPERT_GROUPS):
        r = [biased[g * E + e:g * E + e + 1, :] for e in range(E)]
        best = r[0] + r[1]
        for lo, hi in zip(PAIR_LO[1:], PAIR_HI[1:]):
            best = jnp.maximum(best, r[lo] + r[hi])
        gsum.append(best)
    gmax = functools.reduce(jnp.maximum, gsum)
    taken = jnp.zeros_like(gmax, dtype=jnp.bool_)
    gid = jnp.zeros_like(gmax)
    vb = [jnp.zeros_like(gmax) for _ in range(E)]
    for g in range(N_EXPERT_GROUPS):
        is_g = (gsum[g] == gmax) & jnp.logical_not(taken)
        taken = taken | is_g
        gid = jnp.where(is_g, float(g), gid)
        for e in range(E):
            vb[e] = jnp.where(is_g, biased[g * E + e:g * E + e + 1, :], vb[e])
    sel = []
    for e in range(E):
        rank = jnp.zeros_like(gmax)
        for o in range(E):
            if o < e:
                rank = rank + (vb[o] >= vb[e]).astype(F32)
            elif o > e:
                rank = rank + (vb[o] > vb[e]).astype(F32)
        sel.append(rank < 2.0)
    pidx = jnp.zeros_like(gmax)
    for p, (lo, hi) in enumerate(zip(PAIR_LO, PAIR_HI)):
        pidx = jnp.where(sel[lo] & sel[hi], float(p), pidx)
    bucket = gid * N_PAIRS + pidx
    @pl.when((pl.program_id(0) == 0) & (pl.program_id(1) == 0))
    def _():
        count_ref[...] = jnp.zeros_like(count_ref)

    tm = bucket.shape[1]
    brow = lax.broadcasted_iota(jnp.int32, (COUNT_ROWS, tm), 0).astype(F32)
    onehot = jnp.where(brow == bucket, 1.0, 0.0)
    prefix = _dot(onehot.astype(BF16), upper_ref[...])
    base = count_ref[...]
    rank = jnp.sum(onehot * (prefix - 1.0 + base[:, 0:1]), axis=0, keepdims=True)
    count_ref[...] = base + jnp.sum(onehot, axis=1, keepdims=True)
    counts_out_ref[...] = count_ref[...]
    rows = lax.broadcasted_iota(jnp.int32, (8, tm), 0)
    route_ref[0] = jnp.where(rows == 0, bucket, jnp.where(rows == 1, rank, 0.0))


def _outproj(y_ssd, y_mla, y_moba, x, w_out, gm, g, sc, sh, rw, router_bias, tm):
    bsz, s, d = x.shape
    tok = lambda w: pl.BlockSpec((1, tm, w), lambda b, i: (b, i, 0))
    per_b = pl.BlockSpec((1, 1, d), lambda b, i: (b, 0, 0))
    full = lambda a: pl.BlockSpec(a.shape, lambda b, i: (0,) * a.ndim)
    rb = router_bias.astype(F32).reshape(N_EXPERTS, 1)
    g2 = g.reshape(1, d)
    upper = (jnp.arange(tm)[:, None] <= jnp.arange(tm)[None, :]).astype(BF16)
    return pl.pallas_call(
        _outproj_kernel, grid=(bsz, s // tm),
        in_specs=[tok(y_ssd.shape[-1]), tok(y_mla.shape[-1]), tok(y_moba.shape[-1]), tok(d),
                  full(w_out), per_b, full(g2), per_b, per_b, full(rw), full(rb), full(upper)],
        out_specs=[tok(d), tok(d), pl.BlockSpec((1, 8, tm), lambda b, i: (b, 0, i)),
                   pl.BlockSpec((COUNT_ROWS, LANES), lambda b, i: (0, 0))],
        out_shape=[jax.ShapeDtypeStruct((bsz, s, d), F32), jax.ShapeDtypeStruct((bsz, s, d), BF16),
                   jax.ShapeDtypeStruct((bsz, 8, s), F32),
                   jax.ShapeDtypeStruct((COUNT_ROWS, LANES), F32)],
        scratch_shapes=[pltpu.VMEM((COUNT_ROWS, LANES), F32)],
        compiler_params=_params("arbitrary", "arbitrary"), name="out_proj_router",
    )(y_ssd, y_mla, y_moba, x, w_out, gm, g2, sc, sh, rw, rb, upper)


def _moe_kernel(ea_ref, eb_ref, nv_ref, x_ref, rw_ref, gua_ref, gub_ref, da_ref, db_ref, o_ref):
    i = pl.program_id(0)

    @pl.when(nv_ref[i] > 0)
    def _():
        xb = x_ref[...]
        logits = _dot(xb, rw_ref[...])
        lane = lax.broadcasted_iota(jnp.int32, logits.shape, 1)
        pick = lambda e: _sigmoid(jnp.sum(jnp.where(lane == e, logits, 0.0), axis=-1, keepdims=True))
        s_a, s_b = pick(ea_ref[i]), pick(eb_ref[i])
        tot = s_a + s_b
        out = None
        for gu_ref, d_ref, gate in ((gua_ref, da_ref, s_a / tot), (gub_ref, db_ref, s_b / tot)):
            gu = _dot(xb, gu_ref[0])
            hid = (_silu(gu[:, :D_EXPERT]) * gu[:, D_EXPERT:]).astype(BF16)
            y = _dot(hid, d_ref[0]) * gate
            out = y if out is None else out + y
        o_ref[...] = out.astype(o_ref.dtype)

    @pl.when(nv_ref[i] <= 0)
    def _():
        o_ref[...] = jnp.zeros_like(o_ref)


def _moe(xs, rw, tile_ea, tile_eb, tile_nv, w_gu, w_down):
    nslot, d = xs.shape
    nt = nslot // MOE_TILE
    grid_spec = pltpu.PrefetchScalarGridSpec(
        num_scalar_prefetch=3, grid=(nt,),
        in_specs=[pl.BlockSpec((MOE_TILE, d), lambda i, ea, eb, nv: (i, 0)),
                  pl.BlockSpec(rw.shape, lambda i, ea, eb, nv: (0, 0)),
                  pl.BlockSpec((1, d, 2 * D_EXPERT), lambda i, ea, eb, nv: (ea[i], 0, 0)),
                  pl.BlockSpec((1, d, 2 * D_EXPERT), lambda i, ea, eb, nv: (eb[i], 0, 0)),
                  pl.BlockSpec((1, D_EXPERT, d), lambda i, ea, eb, nv: (ea[i], 0, 0)),
                  pl.BlockSpec((1, D_EXPERT, d), lambda i, ea, eb, nv: (eb[i], 0, 0))],
        out_specs=pl.BlockSpec((MOE_TILE, d), lambda i, ea, eb, nv: (i, 0)))
    return pl.pallas_call(
        _moe_kernel, grid_spec=grid_spec,
        out_shape=jax.ShapeDtypeStruct((nslot, d), BF16),
        compiler_params=_params("arbitrary"), name="moe_ffn",
    )(tile_ea, tile_eb, tile_nv, xs, rw, w_gu, w_gu, w_down, w_down)


def _bucket_layout(bucket, rank, counts, n_tokens):
    nslot = n_tokens + N_BUCKETS * MOE_TILE
    nt = nslot // MOE_TILE
    padded = ((counts + MOE_TILE - 1) // MOE_TILE) * MOE_TILE
    pends = jnp.cumsum(padded)
    pstarts = pends - padded
    onehot = bucket[:, None] == jnp.arange(N_BUCKETS, dtype=jnp.int32)[None, :]
    token_slot = jnp.sum(jnp.where(onehot, pstarts[None, :], 0), axis=1).astype(jnp.int32) + rank
    tile_start = jnp.arange(nt, dtype=jnp.int32) * MOE_TILE
    tile_bucket = jnp.sum(tile_start[:, None] >= pends[None, :], axis=1).astype(jnp.int32)
    live = tile_bucket < N_BUCKETS
    tb = jnp.minimum(tile_bucket, N_BUCKETS - 1)
    tile_nv = jnp.where(live, jnp.clip(pstarts[tb] + counts[tb] - tile_start, 0, MOE_TILE), 0)
    npad = padded - counts
    pad_ends = jnp.cumsum(npad)
    q = jnp.arange(nslot - n_tokens, dtype=jnp.int32)
    qb = q[:, None] >= pad_ends[None, :]
    q_bucket = jnp.minimum(jnp.sum(qb, axis=1), N_BUCKETS - 1).astype(jnp.int32)
    sel = q_bucket[:, None] == jnp.arange(N_BUCKETS, dtype=jnp.int32)[None, :]
    pick = lambda v: jnp.sum(jnp.where(sel, v[None, :], 0), axis=1)
    pad_slot = pick(pstarts + counts) + q - pick(pad_ends - npad)
    pad_slot = jnp.where(q < pad_ends[-1], pad_slot, pends[-1] + q - pad_ends[-1]).astype(jnp.int32)
    keys = jnp.concatenate([token_slot, pad_slot])
    vals = jnp.concatenate([jnp.arange(n_tokens, dtype=jnp.int32), jnp.zeros((nslot - n_tokens,), jnp.int32)])
    _, slot_token = lax.sort_key_val(keys, vals)
    last_live = jnp.max(jnp.where(live, tb, 0))
    tb = jnp.where(live, tb, last_live)
    grp, pair = tb // N_PAIRS, tb % N_PAIRS
    tile_ea = grp * EXPERTS_PER_GROUP + jnp.asarray(PAIR_LO, jnp.int32)[pair]
    tile_eb = grp * EXPERTS_PER_GROUP + jnp.asarray(PAIR_HI, jnp.int32)[pair]
    return slot_token, token_slot, tile_ea.astype(jnp.int32), tile_eb.astype(jnp.int32), \
        tile_nv.astype(jnp.int32)


def _final_kernel(x_ref, moe_ref, gf_ref, o_ref):
    o_ref[0] = x_ref[0] + gf_ref[0] * moe_ref[0].astype(F32)


def _final_residual(x, moe, gf, tm):
    bsz, s, d = x.shape
    tok = pl.BlockSpec((1, tm, d), lambda b, i: (b, i, 0))
    return pl.pallas_call(
        _final_kernel, grid=(bsz, s // tm),
        in_specs=[tok, tok, pl.BlockSpec((1, 1, d), lambda b, i: (b, 0, 0))],
        out_specs=tok, out_shape=jax.ShapeDtypeStruct((bsz, s, d), F32),
        compiler_params=_params("arbitrary", "arbitrary"), name="final_residual",
    )(x, moe, gf)


def _pad_w_in(w_in):
    d = w_in.shape[0]
    sizes = (SSD_WIDTH, SSD_CONV_DIM, SSD_HEADS, MLA_Q_LORA, MLA_KV_LORA, MLA_ROPE,
             MOBA_HEADS * MOBA_HEAD_DIM, MOBA_HEADS * MOBA_HEAD_DIM, MOBA_HEADS * MOBA_HEAD_DIM)
    offs = np.cumsum((0,) + sizes)
    z, xbc, dtw, cq, ckv, krope, mq, mk, mv = [w_in[:, offs[i]:offs[i + 1]] for i in range(9)]
    zc = lambda n: jnp.zeros((d, n), w_in.dtype)
    kpe = jnp.concatenate([zc(MLA_NOPE), krope, zc(HEAD_PAD - MLA_QK)], axis=1)
    cols = [z, xbc, dtw, zc(LANES - SSD_HEADS), cq, ckv, kpe,
            _pad_heads(mq, MOBA_HEADS, MOBA_HEAD_DIM), _pad_heads(mk, MOBA_HEADS, MOBA_HEAD_DIM), mv]
    return jnp.concatenate(cols, axis=1).astype(BF16)


def kernel(x, c, ada_w, ada_b, norm_mix_g, norm_ffn_g, w_in, ssd_conv_w, ssd_conv_b, ssd_dt_bias,
           ssd_a_log, ssd_d, ssd_norm_g, mla_q_norm_g, mla_w_uq, mla_kv_norm_g, mla_w_ukv, mla_q_gain,
           mla_k_gain, mla_out_g, moba_q_gain, moba_k_gain, moba_out_g, rel_bias, w_out, router_w,
           router_bias, moe_w_gate, moe_w_up, moe_w_down):
    bsz, s, d = x.shape
    depth = ada_w.shape[0]
    tm = min(512, s)
    nb = s // MOBA_BLOCK
    mod = _modulation(c, ada_w, ada_b).reshape(depth, bsz, 6, 1, d)
    b0_moba, b1_moba = _moba_bias_tiles(rel_bias)
    b0_mla = _causal_tile(MLA_HEADS)
    rw = jnp.pad(router_w, ((0, 0), (0, LANES - N_EXPERTS))).astype(BF16)
    moe_prev, gf_prev = None, None
    for l in range(depth):
        sh_m, sc_m, g_m, sh_f, sc_f, g_f = [mod[l, :, i] for i in range(6)]
        mla_params = _mla_prep_params(s, mla_q_norm_g[l], mla_w_uq[l], mla_kv_norm_g[l], mla_w_ukv[l],
                                      mla_q_gain[l], mla_k_gain[l])
        moba_params = _moba_prep_params(moba_q_gain[l], moba_k_gain[l])
        x, (z, xbc, dt, q, k, v, mq, mk, mv, kmean) = _inproj(
            x, moe_prev, gf_prev, norm_mix_g[l], sc_m, sh_m, _pad_w_in(w_in[l]), mla_params,
            moba_params, tm)
        y_ssd = _ssd(z, xbc, dt, ssd_conv_w[l], ssd_conv_b[l], ssd_dt_bias[l], ssd_a_log[l],
                     ssd_d[l], ssd_norm_g[l])
        y_mla = _attention(q, k, v, b0_mla, mla_out_g[l])
        km = kmean.reshape(bsz, nb, MOBA_HEADS, HEAD_PAD).transpose(0, 2, 1, 3)
        kmr = jnp.pad(km, ((0, 0), (0, 0), (GATE_LANE0, HEAD_PAD - GATE_LANE0 - nb), (0, 0))).astype(BF16)
        y_moba = _attention(mq, mk, mv, b0_moba, moba_out_g[l], b1_moba, kmr)
        x, hf, route, counts = _outproj(y_ssd, y_mla, y_moba, x, w_out[l].astype(BF16), g_m,
                                        norm_ffn_g[l], sc_f, sh_f, rw, router_bias, tm)
        bucket = route[:, 0, :].reshape(-1).astype(jnp.int32)
        rank = route[:, 1, :].reshape(-1).astype(jnp.int32)
        slot_token, token_slot, tile_ea, tile_eb, tile_nv = _bucket_layout(
            bucket, rank, counts[:N_BUCKETS, 0].astype(jnp.int32), bsz * s)
        xs = hf.reshape(-1, d)[slot_token]
        w_gu = jnp.concatenate([moe_w_gate[l], moe_w_up[l]], axis=-1).astype(BF16)
        ys = _moe(xs, rw, tile_ea, tile_eb, tile_nv, w_gu, moe_w_down[l].astype(BF16))
        moe_prev = ys[token_slot].reshape(bsz, s, d)
        gf_prev = g_f
    return _final_residual(x, moe_prev, gf_prev, tm)
```

```python
import functools
import math

import numpy as np
import jax
import jax.numpy as jnp
from jax import lax
from jax.experimental import pallas as pl
from jax.experimental.pallas import tpu as pltpu

D_MODEL = 1024
DEPTH = 2
SSD_HEADS = 8
SSD_HEAD_DIM = 64
SSD_WIDTH = SSD_HEADS * SSD_HEAD_DIM
SSD_GROUPS = 2
SSD_STATE = 128
SSD_CONV = 4
SSD_CHUNK = 128
SSD_CONV_DIM = SSD_WIDTH + 2 * SSD_GROUPS * SSD_STATE
MLA_HEADS = 4
MLA_Q_LORA = 256
MLA_KV_LORA = 128
MLA_NOPE = 64
MLA_ROPE = 32
MLA_V = 64
MLA_QK = MLA_NOPE + MLA_ROPE
ROPE_BASE = 10000.0
MOBA_HEADS = 4
MOBA_HEAD_DIM = 64
MOBA_BLOCK = 256
MOBA_TOPK = 3
REL_BUCKETS = 32
REL_MAX_DIST = 128
N_EXPERTS = 16
N_EXPERT_GROUPS = 4
EXPERTS_PER_GROUP = 4
D_EXPERT = 512
EPS = 1e-6

LANES = 128
HEAD_PAD = 128
ATT_BLOCK = 256
ATT_HEADS = 4
V_PAD = 64
VMEM_LIMIT = 56 * 1024 * 1024
MASK_NEG = -1e9
N_PAIRS = 6
N_BUCKETS = N_EXPERT_GROUPS * N_PAIRS
COUNT_ROWS = 32
MOE_TILE = 256
PAIR_LO = (0, 0, 0, 1, 1, 2)
PAIR_HI = (1, 2, 3, 2, 3, 3)

COL_Z = 0
COL_XBC = COL_Z + SSD_WIDTH
COL_DT = COL_XBC + SSD_CONV_DIM
COL_MLA = COL_DT + LANES
MLA_IN_W = MLA_Q_LORA + MLA_KV_LORA + LANES
COL_MOBA = COL_MLA + MLA_IN_W
MOBA_IN_W = 2 * MOBA_HEADS * HEAD_PAD + MOBA_HEADS * MOBA_HEAD_DIM
N_IN_PAD = COL_MOBA + MOBA_IN_W

F32 = jnp.float32
BF16 = jnp.bfloat16


def _params(*sem):
    return pltpu.CompilerParams(dimension_semantics=sem, vmem_limit_bytes=VMEM_LIMIT)


def _dot(a, b):
    return jnp.dot(a, b, preferred_element_type=F32)


def _dot_nt(a, b):
    return lax.dot_general(a, b, (((1,), (1,)), ((), ())), preferred_element_type=F32)


def _dot_exact(a, b):
    return jnp.dot(a, b, preferred_element_type=F32, precision=lax.Precision.HIGHEST)


def _bf16_pieces(v, n):
    pieces = []
    for _ in range(n):
        p = v.astype(BF16)
        pieces.append(p)
        v = v - p.astype(F32)
    return pieces


def _dot_select_rows(sel01, v, n):
    sel = sel01.astype(BF16)
    return sum(_dot(sel, p) for p in _bf16_pieces(v, n))


def _dot_select_cols(v, sel01, n):
    sel = sel01.astype(BF16)
    return sum(_dot(p, sel) for p in _bf16_pieces(v, n))


def _silu(v):
    return v * (1.0 / (1.0 + jnp.exp(-v)))


def _sigmoid(v):
    return 1.0 / (1.0 + jnp.exp(-v))


def _softplus(v):
    return jnp.maximum(v, 0.0) + jnp.log1p(jnp.exp(-jnp.abs(v)))


def _ada_kernel(c_ref, w_ref, b_ref, o_ref):
    c = c_ref[...]
    o_ref[0] = _dot_exact(_silu(c), w_ref[0]) + b_ref[0]


def _modulation(c, ada_w, ada_b):
    depth, d, n = ada_w.shape
    bsz = c.shape[0]
    tn = d
    return pl.pallas_call(
        _ada_kernel,
        grid=(depth, n // tn),
        in_specs=[pl.BlockSpec((bsz, d), lambda l, j: (0, 0)),
                  pl.BlockSpec((1, d, tn), lambda l, j: (l, 0, j)),
                  pl.BlockSpec((1, 1, tn), lambda l, j: (l, 0, j))],
        out_specs=pl.BlockSpec((1, bsz, tn), lambda l, j: (l, 0, j)),
        out_shape=jax.ShapeDtypeStruct((depth, bsz, n), F32),
        compiler_params=_params("arbitrary", "arbitrary"),
        name="ada_modulation",
    )(c, ada_w, ada_b.reshape(depth, 1, n))


N_MLA_PARAMS = 9
N_MOBA_PARAMS = 2


def _inproj_body(x, g_ref, sc_ref, sh_ref, w_ref, prep_refs, out_refs):
    z_ref, xbc_ref, dt_ref = out_refs[:3]
    mla_outs, moba_outs = out_refs[3:6], out_refs[6:10]
    mla_params, moba_params = prep_refs[:N_MLA_PARAMS], prep_refs[N_MLA_PARAMS:]
    ms = jnp.mean(x * x, axis=-1, keepdims=True)
    hm = (x * lax.rsqrt(ms + EPS) * g_ref[...]) * (1.0 + sc_ref[0]) + sh_ref[0]
    hb = hm.astype(BF16)
    mla_in = _dot(hb, w_ref[:, COL_MLA:COL_MOBA])
    q, kv, kpe = _mla_up_proj(mla_in, *mla_params[:4])
    moba_in = _dot(hb, w_ref[:, COL_MOBA:N_IN_PAD])
    dt_ref[0] = _dot(hb, w_ref[:, COL_DT:COL_MLA])
    _mla_prep_math(q, kv, kpe, *mla_params[4:], *mla_outs)
    _moba_prep_math(moba_in, *moba_params, *moba_outs)
    z_ref[0] = _dot(hb, w_ref[:, COL_Z:COL_XBC]).astype(z_ref.dtype)
    xbc_ref[0] = _dot(hb, w_ref[:, COL_XBC:COL_DT]).astype(xbc_ref.dtype)


def _inproj_kernel(x_ref, g_ref, sc_ref, sh_ref, w_ref, *refs):
    n = N_MLA_PARAMS + N_MOBA_PARAMS
    _inproj_body(x_ref[0], g_ref, sc_ref, sh_ref, w_ref, refs[:n], refs[n:])


def _inproj_res_kernel(x_ref, moe_ref, gf_ref, g_ref, sc_ref, sh_ref, w_ref, *refs):
    n = N_MLA_PARAMS + N_MOBA_PARAMS
    xo_ref = refs[n]
    x = x_ref[0] + gf_ref[0] * moe_ref[0].astype(F32)
    xo_ref[0] = x
    _inproj_body(x, g_ref, sc_ref, sh_ref, w_ref, refs[:n], refs[n + 1:])


def _inproj(x, moe_prev, gf_prev, g, sc, sh, w_pad, mla_params, moba_params, tm):
    bsz, s, d = x.shape
    nb_tile = tm // MOBA_BLOCK
    hw = ATT_HEADS * HEAD_PAD
    tok = lambda w: pl.BlockSpec((1, tm, w), lambda b, i: (b, i, 0))
    per_b = pl.BlockSpec((1, 1, d), lambda b, i: (b, 0, 0))
    full = lambda a: pl.BlockSpec(a.shape, lambda b, i: (0,) * a.ndim)
    pos = pl.BlockSpec((tm, HEAD_PAD), lambda b, i: (i, 0))
    vt_spec = pl.BlockSpec((1, hw, tm), lambda b, i: (b, 0, i))
    tok_shape = lambda w, dt: jax.ShapeDtypeStruct((bsz, s, w), dt)
    vt_shape = jax.ShapeDtypeStruct((bsz, hw, s), BF16)
    out_shapes = [tok_shape(SSD_WIDTH, BF16), tok_shape(SSD_CONV_DIM, BF16), tok_shape(LANES, F32),
                  tok_shape(hw, BF16), tok_shape(hw, BF16), vt_shape,
                  tok_shape(hw, BF16), tok_shape(hw, BF16), vt_shape,
                  jax.ShapeDtypeStruct((bsz, s // MOBA_BLOCK, 1, hw), F32)]
    out_specs = [tok(SSD_WIDTH), tok(SSD_CONV_DIM), tok(LANES),
                 tok(hw), tok(hw), vt_spec, tok(hw), tok(hw), vt_spec,
                 pl.BlockSpec((1, nb_tile, 1, hw), lambda b, i: (b, i, 0, 0))]
    mla_specs = [full(a) for a in mla_params[:4]] + [pos] * 3 + [full(a) for a in mla_params[7:]]
    prep = tuple(mla_params) + tuple(moba_params)
    prep_specs = mla_specs + [full(a) for a in moba_params]
    common_specs = [pl.BlockSpec((1, d), lambda b, i: (0, 0)), per_b, per_b,
                    pl.BlockSpec((d, N_IN_PAD), lambda b, i: (0, 0))] + prep_specs
    common = (g.reshape(1, d), sc, sh, w_pad) + prep
    if moe_prev is None:
        outs = pl.pallas_call(
            _inproj_kernel, grid=(bsz, s // tm),
            in_specs=[tok(d)] + common_specs, out_specs=out_specs, out_shape=out_shapes,
            compiler_params=_params("arbitrary", "arbitrary"), name="in_proj",
        )(x, *common)
        return x, outs
    outs = pl.pallas_call(
        _inproj_res_kernel, grid=(bsz, s // tm),
        in_specs=[tok(d), tok(d), per_b] + common_specs,
        out_specs=[tok(d)] + out_specs,
        out_shape=[jax.ShapeDtypeStruct((bsz, s, d), F32)] + out_shapes,
        compiler_params=_params("arbitrary", "arbitrary"), name="in_proj_res",
    )(x, moe_prev, gf_prev, *common)
    return outs[0], outs[1:]


SSD_STEP_CHUNKS = 2
CONV_HALO = 8


def _ssd_kernel(z_ref, xbc_ref, dt_ref, cw_ref, cb_ref, dtb_ref, alog_ref, dskip_ref, ng_ref,
                expand_ref, o_ref, state_ref, ext_ref):
    L, G, N, P = SSD_CHUNK, SSD_GROUPS, SSD_STATE, SSD_HEAD_DIM
    R = SSD_HEADS // G
    GW = R * P
    first = pl.program_id(1) == 0

    @pl.when(first)
    def _():
        state_ref[...] = jnp.zeros_like(state_ref)
        ext_ref[0:CONV_HALO, :] = jnp.zeros((CONV_HALO, SSD_CONV_DIM), F32)

    row = lax.broadcasted_iota(jnp.int32, (L, L), 0)
    col = lax.broadcasted_iota(jnp.int32, (L, L), 1)
    causal = col <= row
    tril = jnp.where(causal, 1.0, 0.0).astype(BF16)
    a_row = -jnp.exp(alog_ref[...])

    for ck in range(SSD_STEP_CHUNKS):
        r0 = ck * L
        ext_ref[CONV_HALO:CONV_HALO + L, :] = xbc_ref[0, r0:r0 + L, :].astype(F32)
        conv = cb_ref[...]
        for k in range(SSD_CONV):
            off = CONV_HALO - (SSD_CONV - 1) + k
            conv = conv + cw_ref[k:k + 1, :] * ext_ref[off:off + L, :]
        ext_ref[0:CONV_HALO, :] = ext_ref[L:L + CONV_HALO, :]
        xc = _silu(conv)
        xs = xc[:, :SSD_WIDTH]

        dt = _softplus(dt_ref[0, r0:r0 + L, :] + dtb_ref[...])
        a_cum = _dot_select_rows(tril, dt * a_row, 3)
        a_cum_t = a_cum.T
        a_last = a_cum[L - 1:L, :]
        stacked = jnp.concatenate(
            [dt, jnp.exp(a_cum), jnp.exp(a_last - a_cum),
             jnp.broadcast_to(jnp.exp(a_last), (8, LANES))], axis=0)
        wide = _dot_select_cols(stacked, expand_ref[...], 2)
        dt_w, ea_w, te_w, cd_w = wide[0:L], wide[L:2 * L], wide[2 * L:3 * L], wide[3 * L:3 * L + 1]
        xdt = xs * dt_w
        xdt_b = xdt.astype(BF16)
        xw_b = (xdt * te_w).astype(BF16)

        y_parts = []
        for g in range(G):
            bm = xc[:, SSD_WIDTH + g * N:SSD_WIDTH + (g + 1) * N]
            cm = xc[:, SSD_WIDTH + (G + g) * N:SSD_WIDTH + (G + g + 1) * N]
            bm_b, cm_b = bm.astype(BF16), cm.astype(BF16)
            cb = _dot_nt(cm_b, bm_b)
            st = state_ref[g]
            y_off = _dot(cm_b, st.astype(BF16)) * ea_w[:, g * GW:(g + 1) * GW]
            diag = []
            for r in range(R):
                h = g * R + r
                seg = a_cum[:, h:h + 1] - a_cum_t[h:h + 1, :]
                m = (cb * jnp.exp(jnp.where(causal, seg, -jnp.inf))).astype(BF16)
                diag.append(_dot(m, xdt_b[:, h * P:(h + 1) * P]))
            y_parts.append(jnp.concatenate(diag, axis=1) + y_off)
            state_ref[g] = st * cd_w[:, g * GW:(g + 1) * GW] + _dot(
                bm.T.astype(BF16), xw_b[:, g * GW:(g + 1) * GW])

        y = jnp.concatenate(y_parts, axis=1) + xs * dskip_ref[...]
        y = y * _silu(z_ref[0, r0:r0 + L, :].astype(F32))
        outs = []
        for g in range(G):
            yg = y[:, g * GW:(g + 1) * GW]
            outs.append(yg * lax.rsqrt(jnp.mean(yg * yg, axis=-1, keepdims=True) + EPS))
        o_ref[0, r0:r0 + L, :] = (jnp.concatenate(outs, axis=1) * ng_ref[...]).astype(o_ref.dtype)


def _ssd(z, xbc, dt, conv_w, conv_b, dt_bias, a_log, d_skip, norm_g):
    bsz, s, _ = z.shape
    ts = SSD_STEP_CHUNKS * SSD_CHUNK
    pad_h = lambda v: jnp.pad(v.astype(F32), (0, LANES - SSD_HEADS)).reshape(1, LANES)
    expand = jnp.repeat(jnp.eye(LANES, SSD_HEADS, dtype=BF16), SSD_HEAD_DIM, axis=1)
    tok = lambda w: pl.BlockSpec((1, ts, w), lambda b, i: (b, i, 0))
    full = lambda a: pl.BlockSpec(a.shape, lambda b, i: (0,) * a.ndim)
    params = (conv_w, conv_b.reshape(1, -1), pad_h(dt_bias), pad_h(a_log),
              jnp.repeat(d_skip, SSD_HEAD_DIM).reshape(1, -1), norm_g.reshape(1, -1), expand)
    return pl.pallas_call(
        _ssd_kernel, grid=(bsz, s // ts),
        in_specs=[tok(SSD_WIDTH), tok(SSD_CONV_DIM), tok(LANES)] + [full(p) for p in params],
        out_specs=tok(SSD_WIDTH),
        out_shape=jax.ShapeDtypeStruct((bsz, s, SSD_WIDTH), BF16),
        scratch_shapes=[pltpu.VMEM((SSD_GROUPS, SSD_STATE, SSD_WIDTH // SSD_GROUPS), F32),
                        pltpu.VMEM((CONV_HALO + SSD_CHUNK, SSD_CONV_DIM), F32)],
        compiler_params=_params("arbitrary", "arbitrary"), name="ssd_scan",
    )(z, xbc, dt, *params)


def _augment_v_t(v):
    lane = lax.broadcasted_iota(jnp.int32, (v.shape[0], HEAD_PAD), 1)
    tiles = []
    for p in range(v.shape[1] // HEAD_PAD):
        pt = v[:, p * HEAD_PAD:(p + 1) * HEAD_PAD]
        tiles.append(jnp.where(lane < V_PAD, pt, 1.0))
        tiles.append(jnp.where(lane < V_PAD, pltpu.roll(pt, V_PAD, axis=1), 1.0))
    return jnp.concatenate(tiles, axis=1).T


def _rope_mix(v, cos, sin_hi, sin_lo):
    w = v.shape[1]
    half = MLA_ROPE // 2
    return v * cos + pltpu.roll(v, half, axis=1) * sin_hi + pltpu.roll(v, w - half, axis=1) * sin_lo


def _mla_up_proj(xin, gq_ref, wq_ref, gkv_ref, wkv_ref):
    cq = xin[:, :MLA_Q_LORA]
    ckv = xin[:, MLA_Q_LORA:MLA_Q_LORA + MLA_KV_LORA]
    kpe = xin[:, MLA_Q_LORA + MLA_KV_LORA:]
    cqn = cq * lax.rsqrt(jnp.mean(cq * cq, axis=-1, keepdims=True) + EPS) * gq_ref[...]
    ckvn = ckv * lax.rsqrt(jnp.mean(ckv * ckv, axis=-1, keepdims=True) + EPS) * gkv_ref[...]
    q = _dot(cqn.astype(BF16), wq_ref[...])
    kv = _dot(ckvn.astype(BF16), wkv_ref[...])
    return q, kv, kpe


def _mla_prep_math(q, kv, kpe, cos_ref, shi_ref, slo_ref, qgain_ref, kgain_ref, q_ref, k_ref, v_ref):
    H = MLA_HEADS
    cos, shi, slo = cos_ref[...], shi_ref[...], slo_ref[...]
    tile = lambda t: jnp.concatenate([t] * H, axis=1)
    q = _rope_mix(q, tile(cos), tile(shi), tile(slo))
    kpe = _rope_mix(kpe, cos, shi, slo)
    scale = MLA_QK ** -0.5
    for h in range(H):
        sl = slice(h * HEAD_PAD, (h + 1) * HEAD_PAD)
        qh = q[:, sl]
        qn = qh * lax.rsqrt(jnp.sum(qh * qh, axis=-1, keepdims=True) / MLA_QK + EPS)
        q_ref[0, :, sl] = (qn * (qgain_ref[...] * scale)).astype(q_ref.dtype)
        kh = kv[:, sl] + kpe
        kn = kh * lax.rsqrt(jnp.sum(kh * kh, axis=-1, keepdims=True) / MLA_QK + EPS)
        k_ref[0, :, sl] = (kn * kgain_ref[...]).astype(k_ref.dtype)
    v_ref[0] = _augment_v_t(kv[:, H * HEAD_PAD:]).astype(v_ref.dtype)


def _rope_tables(s):
    half = MLA_ROPE // 2
    inv = ROPE_BASE ** (-2.0 * jnp.arange(half, dtype=F32) / MLA_ROPE)
    ang = jnp.arange(s, dtype=F32)[:, None] * inv[None, :]
    cos, sin = jnp.cos(ang), jnp.sin(ang)
    zeros = lambda n: jnp.zeros((s, n), F32)
    cos_t = jnp.concatenate([jnp.ones((s, MLA_NOPE), F32), cos, cos, zeros(HEAD_PAD - MLA_QK)], axis=1)
    sin_hi = jnp.concatenate([zeros(MLA_NOPE + half), sin, zeros(HEAD_PAD - MLA_QK)], axis=1)
    sin_lo = jnp.concatenate([zeros(MLA_NOPE), -sin, zeros(half + HEAD_PAD - MLA_QK)], axis=1)
    return cos_t, sin_hi, sin_lo


def _pad_heads(w, heads, width):
    lead = w.shape[:-1]
    w = w.reshape(*lead, heads, width)
    w = jnp.pad(w, [(0, 0)] * len(lead) + [(0, 0), (0, HEAD_PAD - width)])
    return w.reshape(*lead, heads * HEAD_PAD)


def _mla_prep_params(s, gq, w_uq, gkv, w_ukv, q_gain, k_gain):
    H = MLA_HEADS
    wq = _pad_heads(w_uq, H, MLA_QK).astype(BF16)
    wkv = w_ukv.reshape(MLA_KV_LORA, H, MLA_NOPE + MLA_V)
    wk = _pad_heads(wkv[:, :, :MLA_NOPE].reshape(MLA_KV_LORA, H * MLA_NOPE), H, MLA_NOPE)
    wv = wkv[:, :, MLA_NOPE:].reshape(MLA_KV_LORA, H * MLA_V)
    wkv_p = jnp.concatenate([wk, wv], axis=1).astype(BF16)
    cos_t, sin_hi, sin_lo = _rope_tables(s)
    pad_g = lambda v: jnp.pad(v, (0, HEAD_PAD - MLA_QK)).reshape(1, HEAD_PAD)
    return (gq.reshape(1, -1), wq, gkv.reshape(1, -1), wkv_p, cos_t, sin_hi, sin_lo,
            pad_g(q_gain), pad_g(k_gain))


GATE_LANE0 = MOBA_HEAD_DIM


def _moba_prep_math(xin, qgain_ref, kgain_ref, q_ref, k_ref, v_ref, km_ref):
    H, W, B = MOBA_HEADS, MOBA_HEADS * HEAD_PAD, MOBA_BLOCK
    tm = xin.shape[0]
    blk0 = pl.program_id(1) * (tm // B)
    lane = lax.broadcasted_iota(jnp.int32, (1, HEAD_PAD), 1)
    scale = MOBA_HEAD_DIM ** -0.5
    for h in range(H):
        sl = slice(h * HEAD_PAD, (h + 1) * HEAD_PAD)
        qh = xin[:, sl]
        qn = qh * lax.rsqrt(jnp.sum(qh * qh, axis=-1, keepdims=True) / MOBA_HEAD_DIM + EPS)
        q_ref[0, :, sl] = (qn * (qgain_ref[...] * scale)).astype(q_ref.dtype)
        kh = xin[:, W + h * HEAD_PAD:W + (h + 1) * HEAD_PAD]
        kn = kh * lax.rsqrt(jnp.sum(kh * kh, axis=-1, keepdims=True) / MOBA_HEAD_DIM + EPS)
        kn = kn * kgain_ref[...]
        for j in range(tm // B):
            rows = slice(j * B, (j + 1) * B)
            tag = (lane == GATE_LANE0 + blk0 + j).astype(F32)
            km_ref[0, j, :, sl] = jnp.mean(kn[rows], axis=0, keepdims=True)
            k_ref[0, rows, sl] = (kn[rows] + tag).astype(k_ref.dtype)
    v_ref[0] = _augment_v_t(xin[:, 2 * W:]).astype(v_ref.dtype)


def _moba_prep_params(q_gain, k_gain):
    pad_g = lambda v: jnp.pad(v, (0, HEAD_PAD - MOBA_HEAD_DIM)).reshape(1, HEAD_PAD)
    return pad_g(q_gain), pad_g(k_gain)


def _attn_kernel(*refs, gated):
    if gated:
        q_ref, k_ref, vt_ref, b0_ref, b1_ref, kmr_ref, og_ref, o_ref, m_ref, acc_ref, qt_ref, qpt_ref = refs
    else:
        q_ref, k_ref, vt_ref, b0_ref, og_ref, o_ref, m_ref, acc_ref, qt_ref = refs
        qpt_ref = qt_ref
    T = ATT_BLOCK
    H = ATT_HEADS
    j = pl.program_id(1)
    hsl = lambda h: slice(h * HEAD_PAD, (h + 1) * HEAD_PAD)

    def step(r, rows, qts_ref, bias_ref=None, first=False):
        sts = []
        for h in range(H):
            st = _dot(k_ref[0, pl.ds(r, rows), hsl(h)], qts_ref[h])
            sts.append(st if bias_ref is None else st + bias_ref[h])
        for h in range(H):
            vt = vt_ref[0, hsl(h), pl.ds(r, rows)]
            mx = jnp.max(sts[h], axis=0, keepdims=True)
            if first:
                m_ref[h] = mx
                acc_ref[h] = _dot(vt, jnp.exp(sts[h] - mx).astype(BF16))
            else:
                m_old = m_ref[h]
                m_new = jnp.maximum(m_old, mx)
                m_ref[h] = m_new
                acc_ref[h] = (jnp.exp(m_old - m_new) * acc_ref[h]
                              + _dot(vt, jnp.exp(sts[h] - m_new).astype(BF16)))

    for h in range(H):
        qt = q_ref[0, :, hsl(h)].astype(F32).T
        qt_b = qt.astype(BF16)
        qt_ref[h] = qt_b
        if gated:
            nbp = -(-(k_ref.shape[1] // T) // 8) * 8
            g = _dot(kmr_ref[0, h], qt_b)[GATE_LANE0:GATE_LANE0 + nbp]
            row = lax.broadcasted_iota(jnp.int32, (nbp, T), 0)
            g = jnp.where(row < j, g, -jnp.inf)
            chosen = jnp.zeros((nbp, T), jnp.bool_)
            for _ in range(MOBA_TOPK):
                mx = jnp.max(g, axis=0, keepdims=True)
                hit = (g == mx) & (mx > -jnp.inf)
                first = jnp.min(jnp.where(hit, row, nbp), axis=0, keepdims=True)
                pick = row == first
                chosen = chosen | pick
                g = jnp.where(pick, -jnp.inf, g)
            neg = jnp.where(chosen, 0.0, MASK_NEG)
            qpt_ref[h] = jnp.concatenate(
                [qt[:GATE_LANE0], qt[GATE_LANE0:GATE_LANE0 + nbp] + neg, qt[GATE_LANE0 + nbp:]],
                axis=0).astype(BF16)

    step(pl.multiple_of(j * T, T), T, qt_ref, b0_ref, first=True)
    if gated:
        @pl.when(j >= 1)
        def _():
            step(pl.multiple_of((j - 1) * T, T), T, qpt_ref, b1_ref)
        n_far = jnp.maximum(j - 1, 0)
    else:
        n_far = j

    def pair_body(i, carry):
        step(pl.multiple_of(i * (2 * T), 2 * T), 2 * T, qpt_ref)
        return carry

    lax.fori_loop(0, lax.shift_right_logical(n_far, 1), pair_body, 0)

    @pl.when((n_far & 1) == 1)
    def _():
        step(pl.multiple_of((n_far - 1) * T, T), T, qpt_ref)

    outs = []
    for h in range(H):
        a = acc_ref[h]
        outs.append(a[:V_PAD] * (1.0 / a[V_PAD:]))
    y = jnp.concatenate(outs, axis=0).T
    y = y * lax.rsqrt(jnp.mean(y * y, axis=-1, keepdims=True) + EPS) * og_ref[...]
    o_ref[0] = y.astype(o_ref.dtype)


def _attention(q, k, v, bias0, out_gain, bias1=None, kmr=None):
    bsz, s, w = q.shape
    T = ATT_BLOCK
    gated = kmr is not None
    qspec = pl.BlockSpec((1, T, w), lambda b, i: (b, i, 0))
    seq = lambda a: pl.BlockSpec((1,) + a.shape[1:], lambda b, i: (b,) + (0,) * (a.ndim - 1))
    full = lambda a: pl.BlockSpec(a.shape, lambda b, i: (0,) * a.ndim)
    og = out_gain.reshape(1, -1)
    if gated:
        args = (q, k, v, bias0, bias1, kmr, og)
        specs = [qspec, seq(k), seq(v), full(bias0), full(bias1), seq(kmr), full(og)]
    else:
        args = (q, k, v, bias0, og)
        specs = [qspec, seq(k), seq(v), full(bias0), full(og)]
    wo = ATT_HEADS * V_PAD
    scratch = [pltpu.VMEM((ATT_HEADS, 1, T), F32), pltpu.VMEM((ATT_HEADS, HEAD_PAD, T), F32),
               pltpu.VMEM((ATT_HEADS, HEAD_PAD, T), BF16)]
    if gated:
        scratch.append(pltpu.VMEM((ATT_HEADS, HEAD_PAD, T), BF16))
    return pl.pallas_call(
        functools.partial(_attn_kernel, gated=gated), grid=(bsz, s // T),
        in_specs=specs, out_specs=pl.BlockSpec((1, T, wo), lambda b, i: (b, i, 0)),
        out_shape=jax.ShapeDtypeStruct((bsz, s, wo), BF16),
        scratch_shapes=scratch,
        compiler_params=_params("arbitrary", "arbitrary"),
        name="moba_attention" if gated else "mla_attention",
    )(*args)


def _rel_bucket(n):
    max_exact = REL_BUCKETS // 2
    nf = jnp.maximum(n, max_exact).astype(F32)
    large = max_exact + (jnp.log(nf / max_exact) / math.log(REL_MAX_DIST / max_exact)
                         * (REL_BUCKETS - max_exact)).astype(jnp.int32)
    large = jnp.minimum(large, REL_BUCKETS - 1)
    return jnp.where(n < max_exact, n, large)


def _moba_bias_tiles(rel_bias):
    T = ATT_BLOCK
    i = jnp.arange(T)[:, None]
    c = jnp.arange(T)[None, :]
    bias_t = rel_bias.T
    far = bias_t[:, REL_BUCKETS - 1][:, None, None]
    d0 = i - c

    def lookup(bucket):
        hit = bucket[None, :, :, None] == jnp.arange(REL_BUCKETS)[None, None, None, :]
        return jnp.sum(jnp.where(hit, bias_t[:, None, None, :], 0.0), axis=-1)

    b0 = jnp.where((d0 >= 0)[None], lookup(_rel_bucket(jnp.maximum(d0, 0))) - far, MASK_NEG)
    b1 = lookup(_rel_bucket(T + i - c)) - far
    return b0.astype(F32).transpose(0, 2, 1), b1.astype(F32).transpose(0, 2, 1)


def _causal_tile(heads):
    T = ATT_BLOCK
    i = jnp.arange(T)[:, None]
    c = jnp.arange(T)[None, :]
    return jnp.broadcast_to(jnp.where(c <= i, 0.0, MASK_NEG).astype(F32).T, (heads, T, T))


def _outproj_kernel(ys_ref, ya_ref, yb_ref, x_ref, w_ref, gm_ref, g_ref, sc_ref, sh_ref, rw_ref,
                    rb_ref, upper_ref, xo_ref, hf_ref, route_ref, counts_out_ref, count_ref):
    w0, w1 = SSD_WIDTH, SSD_WIDTH + MLA_HEADS * MLA_V
    y = (_dot(ys_ref[0], w_ref[0:w0, :]) + _dot(ya_ref[0], w_ref[w0:w1, :])
         + _dot(yb_ref[0], w_ref[w1:, :]))
    x = x_ref[0] + gm_ref[0] * y
    xo_ref[0] = x
    ms = jnp.mean(x * x, axis=-1, keepdims=True)
    hf = (x * lax.rsqrt(ms + EPS) * g_ref[...]) * (1.0 + sc_ref[0]) + sh_ref[0]
    hb = hf.astype(BF16)
    hf_ref[0] = hb
    logits_t = _dot(hb, rw_ref[...]).T
    score = _sigmoid(logits_t[0:N_EXPERTS, :])
    biased = score + rb_ref[...]
    E = EXPERTS_PER_GROUP
    gsum = []
    for g in range(N_EXPERT_GROUPS):
        r = [biased[g * E + e:g * E + e + 1, :] for e in range(E)]
        best = r[0] + r[1]
        for lo, hi in zip(PAIR_LO[1:], PAIR_HI[1:]):
            best = jnp.maximum(best, r[lo] + r[hi])
        gsum.append(best)
    gmax = functools.reduce(jnp.maximum, gsum)
    taken = jnp.zeros_like(gmax, dtype=jnp.bool_)
    gid = jnp.zeros_like(gmax)
    vb = [jnp.zeros_like(gmax) for _ in range(E)]
    for g in range(N_EXPERT_GROUPS):
        is_g = (gsum[g] == gmax) & jnp.logical_not(taken)
        taken = taken | is_g
        gid = jnp.where(is_g, float(g), gid)
        for e in range(E):
            vb[e] = jnp.where(is_g, biased[g * E + e:g * E + e + 1, :], vb[e])
    sel = []
    for e in range(E):
        rank = jnp.zeros_like(gmax)
        for o in range(E):
            if o < e:
                rank = rank + (vb[o] >= vb[e]).astype(F32)
            elif o > e:
                rank = rank + (vb[o] > vb[e]).astype(F32)
        sel.append(rank < 2.0)
    pidx = jnp.zeros_like(gmax)
    for p, (lo, hi) in enumerate(zip(PAIR_LO, PAIR_HI)):
        pidx = jnp.where(sel[lo] & sel[hi], float(p), pidx)
    bucket = gid * N_PAIRS + pidx
    @pl.when((pl.program_id(0) == 0) & (pl.program_id(1) == 0))
    def _():
        count_ref[...] = jnp.zeros_like(count_ref)

    tm = bucket.shape[1]
    brow = lax.broadcasted_iota(jnp.int32, (COUNT_ROWS, tm), 0).astype(F32)
    onehot = jnp.where(brow == bucket, 1.0, 0.0)
    prefix = _dot(onehot.astype(BF16), upper_ref[...])
    base = count_ref[...]
    rank = jnp.sum(onehot * (prefix - 1.0 + base[:, 0:1]), axis=0, keepdims=True)
    count_ref[...] = base + jnp.sum(onehot, axis=1, keepdims=True)
    counts_out_ref[...] = count_ref[...]
    rows = lax.broadcasted_iota(jnp.int32, (8, tm), 0)
    route_ref[0] = jnp.where(rows == 0, bucket, jnp.where(rows == 1, rank, 0.0))


def _outproj(y_ssd, y_mla, y_moba, x, w_out, gm, g, sc, sh, rw, router_bias, tm):
    bsz, s, d = x.shape
    tok = lambda w: pl.BlockSpec((1, tm, w), lambda b, i: (b, i, 0))
    per_b = pl.BlockSpec((1, 1, d), lambda b, i: (b, 0, 0))
    full = lambda a: pl.BlockSpec(a.shape, lambda b, i: (0,) * a.ndim)
    rb = router_bias.astype(F32).reshape(N_EXPERTS, 1)
    g2 = g.reshape(1, d)
    upper = (jnp.arange(tm)[:, None] <= jnp.arange(tm)[None, :]).astype(BF16)
    return pl.pallas_call(
        _outproj_kernel, grid=(bsz, s // tm),
        in_specs=[tok(y_ssd.shape[-1]), tok(y_mla.shape[-1]), tok(y_moba.shape[-1]), tok(d),
                  full(w_out), per_b, full(g2), per_b, per_b, full(rw), full(rb), full(upper)],
        out_specs=[tok(d), tok(d), pl.BlockSpec((1, 8, tm), lambda b, i: (b, 0, i)),
                   pl.BlockSpec((COUNT_ROWS, LANES), lambda b, i: (0, 0))],
        out_shape=[jax.ShapeDtypeStruct((bsz, s, d), F32), jax.ShapeDtypeStruct((bsz, s, d), BF16),
                   jax.ShapeDtypeStruct((bsz, 8, s), F32),
                   jax.ShapeDtypeStruct((COUNT_ROWS, LANES), F32)],
        scratch_shapes=[pltpu.VMEM((COUNT_ROWS, LANES), F32)],
        compiler_params=_params("arbitrary", "arbitrary"), name="out_proj_router",
    )(y_ssd, y_mla, y_moba, x, w_out, gm, g2, sc, sh, rw, rb, upper)


def _moe_kernel(ea_ref, eb_ref, nv_ref, x_ref, rw_ref, gua_ref, gub_ref, da_ref, db_ref, o_ref):
    i = pl.program_id(0)

    @pl.when(nv_ref[i] > 0)
    def _():
        xb = x_ref[...]
        logits = _dot(xb, rw_ref[...])
        lane = lax.broadcasted_iota(jnp.int32, logits.shape, 1)
        pick = lambda e: _sigmoid(jnp.sum(jnp.where(lane == e, logits, 0.0), axis=-1, keepdims=True))
        s_a, s_b = pick(ea_ref[i]), pick(eb_ref[i])
        tot = s_a + s_b
        out = None
        for gu_ref, d_ref, gate in ((gua_ref, da_ref, s_a / tot), (gub_ref, db_ref, s_b / tot)):
            gu = _dot(xb, gu_ref[0])
            hid = (_silu(gu[:, :D_EXPERT]) * gu[:, D_EXPERT:]).astype(BF16)
            y = _dot(hid, d_ref[0]) * gate
            out = y if out is None else out + y
        o_ref[...] = out.astype(o_ref.dtype)

    @pl.when(nv_ref[i] <= 0)
    def _():
        o_ref[...] = jnp.zeros_like(o_ref)


def _moe(xs, rw, tile_ea, tile_eb, tile_nv, w_gu, w_down):
    nslot, d = xs.shape
    nt = nslot // MOE_TILE
    grid_spec = pltpu.PrefetchScalarGridSpec(
        num_scalar_prefetch=3, grid=(nt,),
        in_specs=[pl.BlockSpec((MOE_TILE, d), lambda i, ea, eb, nv: (i, 0)),
                  pl.BlockSpec(rw.shape, lambda i, ea, eb, nv: (0, 0)),
                  pl.BlockSpec((1, d, 2 * D_EXPERT), lambda i, ea, eb, nv: (ea[i], 0, 0)),
                  pl.BlockSpec((1, d, 2 * D_EXPERT), lambda i, ea, eb, nv: (eb[i], 0, 0)),
                  pl.BlockSpec((1, D_EXPERT, d), lambda i, ea, eb, nv: (ea[i], 0, 0)),
                  pl.BlockSpec((1, D_EXPERT, d), lambda i, ea, eb, nv: (eb[i], 0, 0))],
        out_specs=pl.BlockSpec((MOE_TILE, d), lambda i, ea, eb, nv: (i, 0)))
    return pl.pallas_call(
        _moe_kernel, grid_spec=grid_spec,
        out_shape=jax.ShapeDtypeStruct((nslot, d), BF16),
        compiler_params=_params("arbitrary"), name="moe_ffn",
    )(tile_ea, tile_eb, tile_nv, xs, rw, w_gu, w_gu, w_down, w_down)


def _bucket_layout(bucket, rank, counts, n_tokens):
    nslot = n_tokens + N_BUCKETS * MOE_TILE
    nt = nslot // MOE_TILE
    padded = ((counts + MOE_TILE - 1) // MOE_TILE) * MOE_TILE
    pends = jnp.cumsum(padded)
    pstarts = pends - padded
    onehot = bucket[:, None] == jnp.arange(N_BUCKETS, dtype=jnp.int32)[None, :]
    token_slot = jnp.sum(jnp.where(onehot, pstarts[None, :], 0), axis=1).astype(jnp.int32) + rank
    tile_start = jnp.arange(nt, dtype=jnp.int32) * MOE_TILE
    tile_bucket = jnp.sum(tile_start[:, None] >= pends[None, :], axis=1).astype(jnp.int32)
    live = tile_bucket < N_BUCKETS
    tb = jnp.minimum(tile_bucket, N_BUCKETS - 1)
    tile_nv = jnp.where(live, jnp.clip(pstarts[tb] + counts[tb] - tile_start, 0, MOE_TILE), 0)
    npad = padded - counts
    pad_ends = jnp.cumsum(npad)
    q = jnp.arange(nslot - n_tokens, dtype=jnp.int32)
    qb = q[:, None] >= pad_ends[None, :]
    q_bucket = jnp.minimum(jnp.sum(qb, axis=1), N_BUCKETS - 1).astype(jnp.int32)
    sel = q_bucket[:, None] == jnp.arange(N_BUCKETS, dtype=jnp.int32)[None, :]
    pick = lambda v: jnp.sum(jnp.where(sel, v[None, :], 0), axis=1)
    pad_slot = pick(pstarts + counts) + q - pick(pad_ends - npad)
    pad_slot = jnp.where(q < pad_ends[-1], pad_slot, pends[-1] + q - pad_ends[-1]).astype(jnp.int32)
    keys = jnp.concatenate([token_slot, pad_slot])
    vals = jnp.concatenate([jnp.arange(n_tokens, dtype=jnp.int32), jnp.zeros((nslot - n_tokens,), jnp.int32)])
    _, slot_token = lax.sort_key_val(keys, vals)
    last_live = jnp.max(jnp.where(live, tb, 0))
    tb = jnp.where(live, tb, last_live)
    grp, pair = tb // N_PAIRS, tb % N_PAIRS
    tile_ea = grp * EXPERTS_PER_GROUP + jnp.asarray(PAIR_LO, jnp.int32)[pair]
    tile_eb = grp * EXPERTS_PER_GROUP + jnp.asarray(PAIR_HI, jnp.int32)[pair]
    return slot_token, token_slot, tile_ea.astype(jnp.int32), tile_eb.astype(jnp.int32), \
        tile_nv.astype(jnp.int32)


def _final_kernel(x_ref, moe_ref, gf_ref, o_ref):
    o_ref[0] = x_ref[0] + gf_ref[0] * moe_ref[0].astype(F32)


def _final_residual(x, moe, gf, tm):
    bsz, s, d = x.shape
    tok = pl.BlockSpec((1, tm, d), lambda b, i: (b, i, 0))
    return pl.pallas_call(
        _final_kernel, grid=(bsz, s // tm),
        in_specs=[tok, tok, pl.BlockSpec((1, 1, d), lambda b, i: (b, 0, 0))],
        out_specs=tok, out_shape=jax.ShapeDtypeStruct((bsz, s, d), F32),
        compiler_params=_params("arbitrary", "arbitrary"), name="final_residual",
    )(x, moe, gf)


def _pad_w_in(w_in):
    d = w_in.shape[0]
    sizes = (SSD_WIDTH, SSD_CONV_DIM, SSD_HEADS, MLA_Q_LORA, MLA_KV_LORA, MLA_ROPE,
             MOBA_HEADS * MOBA_HEAD_DIM, MOBA_HEADS * MOBA_HEAD_DIM, MOBA_HEADS * MOBA_HEAD_DIM)
    offs = np.cumsum((0,) + sizes)
    z, xbc, dtw, cq, ckv, krope, mq, mk, mv = [w_in[:, offs[i]:offs[i + 1]] for i in range(9)]
    zc = lambda n: jnp.zeros((d, n), w_in.dtype)
    kpe = jnp.concatenate([zc(MLA_NOPE), krope, zc(HEAD_PAD - MLA_QK)], axis=1)
    cols = [z, xbc, dtw, zc(LANES - SSD_HEADS), cq, ckv, kpe,
            _pad_heads(mq, MOBA_HEADS, MOBA_HEAD_DIM), _pad_heads(mk, MOBA_HEADS, MOBA_HEAD_DIM), mv]
    return jnp.concatenate(cols, axis=1).astype(BF16)


N_STREAMS = 2


def kernel(x, c, ada_w, ada_b, norm_mix_g, norm_ffn_g, w_in, ssd_conv_w, ssd_conv_b, ssd_dt_bias,
           ssd_a_log, ssd_d, ssd_norm_g, mla_q_norm_g, mla_w_uq, mla_kv_norm_g, mla_w_ukv, mla_q_gain,
           mla_k_gain, mla_out_g, moba_q_gain, moba_k_gain, moba_out_g, rel_bias, w_out, router_w,
           router_bias, moe_w_gate, moe_w_up, moe_w_down):
    bsz, s, d = x.shape
    depth = ada_w.shape[0]
    tm = min(512, s)
    nb = s // MOBA_BLOCK
    n_streams = N_STREAMS if bsz % N_STREAMS == 0 else 1
    gb = bsz // n_streams
    mod = _modulation(c, ada_w, ada_b).reshape(depth, bsz, 6, 1, d)
    b0_moba, b1_moba = _moba_bias_tiles(rel_bias)
    b0_mla = _causal_tile(MLA_HEADS)
    rw = jnp.pad(router_w, ((0, 0), (0, LANES - N_EXPERTS))).astype(BF16)
    xs = [x[i * gb:(i + 1) * gb] for i in range(n_streams)]
    moe_prev = [None] * n_streams
    gf_prev = [None] * n_streams
    for l in range(depth):
        mla_params = _mla_prep_params(s, mla_q_norm_g[l], mla_w_uq[l], mla_kv_norm_g[l], mla_w_ukv[l],
                                      mla_q_gain[l], mla_k_gain[l])
        moba_params = _moba_prep_params(moba_q_gain[l], moba_k_gain[l])
        w_pad = _pad_w_in(w_in[l])
        w_out_b = w_out[l].astype(BF16)
        w_gu = jnp.concatenate([moe_w_gate[l], moe_w_up[l]], axis=-1).astype(BF16)
        w_down_b = moe_w_down[l].astype(BF16)
        routed = []
        for i in range(n_streams):
            sh_m, sc_m, g_m, sh_f, sc_f, g_f = [mod[l, i * gb:(i + 1) * gb, j] for j in range(6)]
            xi, (z, xbc, dt, q, k, v, mq, mk, mv, kmean) = _inproj(
                xs[i], moe_prev[i], gf_prev[i], norm_mix_g[l], sc_m, sh_m, w_pad, mla_params,
                moba_params, tm)
            y_ssd = _ssd(z, xbc, dt, ssd_conv_w[l], ssd_conv_b[l], ssd_dt_bias[l], ssd_a_log[l],
                         ssd_d[l], ssd_norm_g[l])
            y_mla = _attention(q, k, v, b0_mla, mla_out_g[l])
            km = kmean.reshape(gb, nb, MOBA_HEADS, HEAD_PAD).transpose(0, 2, 1, 3)
            kmr = jnp.pad(km, ((0, 0), (0, 0), (GATE_LANE0, HEAD_PAD - GATE_LANE0 - nb), (0, 0))).astype(BF16)
            y_moba = _attention(mq, mk, mv, b0_moba, moba_out_g[l], b1_moba, kmr)
            xi, hf, route, counts = _outproj(y_ssd, y_mla, y_moba, xi, w_out_b, g_m, norm_ffn_g[l],
                                             sc_f, sh_f, rw, router_bias, tm)
            bucket = route[:, 0, :].reshape(-1).astype(jnp.int32)
            rank = route[:, 1, :].reshape(-1).astype(jnp.int32)
            slot_token, token_slot, tile_ea, tile_eb, tile_nv = _bucket_layout(
                bucket, rank, counts[:N_BUCKETS, 0].astype(jnp.int32), gb * s)
            xs[i] = xi
            gf_prev[i] = g_f
            routed.append((hf.reshape(-1, d)[slot_token], token_slot, tile_ea, tile_eb, tile_nv))
        for i, (rows, token_slot, tile_ea, tile_eb, tile_nv) in enumerate(routed):
            ys = _moe(rows, rw, tile_ea, tile_eb, tile_nv, w_gu, w_down_b)
            moe_prev[i] = ys[token_slot].reshape(gb, s, d)
    outs = [_final_residual(xs[i], moe_prev[i], gf_prev[i], tm) for i in range(n_streams)]
    return outs[0] if n_streams == 1 else jnp.concatenate(outs, axis=0)
```

```python
import functools
import math

import numpy as np
import jax
import jax.numpy as jnp
from jax import lax
from jax.experimental import pallas as pl
from jax.experimental.pallas import tpu as pltpu

D_MODEL = 1024
DEPTH = 2
SSD_HEADS = 8
SSD_HEAD_DIM = 64
SSD_WIDTH = SSD_HEADS * SSD_HEAD_DIM
SSD_GROUPS = 2
SSD_STATE = 128
SSD_CONV = 4
SSD_CHUNK = 128
SSD_CONV_DIM = SSD_WIDTH + 2 * SSD_GROUPS * SSD_STATE
MLA_HEADS = 4
MLA_Q_LORA = 256
MLA_KV_LORA = 128
MLA_NOPE = 64
MLA_ROPE = 32
MLA_V = 64
MLA_QK = MLA_NOPE + MLA_ROPE
ROPE_BASE = 10000.0
MOBA_HEADS = 4
MOBA_HEAD_DIM = 64
MOBA_BLOCK = 256
MOBA_TOPK = 3
REL_BUCKETS = 32
REL_MAX_DIST = 128
N_EXPERTS = 16
N_EXPERT_GROUPS = 4
EXPERTS_PER_GROUP = 4
D_EXPERT = 512
EPS = 1e-6

LANES = 128
HEAD_PAD = 128
ATT_BLOCK = 256
ATT_HEADS = 4
V_PAD = 64
VMEM_LIMIT = 56 * 1024 * 1024
MASK_NEG = -1e9
N_PAIRS = 6
N_BUCKETS = N_EXPERT_GROUPS * N_PAIRS
COUNT_ROWS = 32
MOE_TILE = 256
PAIR_LO = (0, 0, 0, 1, 1, 2)
PAIR_HI = (1, 2, 3, 2, 3, 3)

COL_Z = 0
COL_XBC = COL_Z + SSD_WIDTH
COL_DT = COL_XBC + SSD_CONV_DIM
COL_MLA = COL_DT + LANES
MLA_IN_W = MLA_Q_LORA + MLA_KV_LORA + LANES
COL_MOBA = COL_MLA + MLA_IN_W
MOBA_IN_W = 3 * MOBA_HEADS * MOBA_HEAD_DIM
N_IN_PAD = COL_MOBA + MOBA_IN_W

F32 = jnp.float32
BF16 = jnp.bfloat16


def _params(*sem):
    return pltpu.CompilerParams(dimension_semantics=sem, vmem_limit_bytes=VMEM_LIMIT)


def _dot(a, b):
    return jnp.dot(a, b, preferred_element_type=F32)


def _dot_nt(a, b):
    return lax.dot_general(a, b, (((1,), (1,)), ((), ())), preferred_element_type=F32)


def _dot_exact(a, b):
    return jnp.dot(a, b, preferred_element_type=F32, precision=lax.Precision.HIGHEST)


def _bf16_pieces(v, n):
    pieces = []
    for _ in range(n):
        p = v.astype(BF16)
        pieces.append(p)
        v = v - p.astype(F32)
    return pieces


def _dot_select_rows(sel01, v, n):
    sel = sel01.astype(BF16)
    return sum(_dot(sel, p) for p in _bf16_pieces(v, n))


def _dot_select_cols(v, sel01, n):
    sel = sel01.astype(BF16)
    return sum(_dot(p, sel) for p in _bf16_pieces(v, n))


def _silu(v):
    return v * (1.0 / (1.0 + jnp.exp(-v)))


def _sigmoid(v):
    return 1.0 / (1.0 + jnp.exp(-v))


def _softplus(v):
    return jnp.maximum(v, 0.0) + jnp.log1p(jnp.exp(-jnp.abs(v)))


def _ada_kernel(c_ref, w_ref, b_ref, o_ref):
    c = c_ref[...]
    o_ref[0] = _dot_exact(_silu(c), w_ref[0]) + b_ref[0]


def _modulation(c, ada_w, ada_b):
    depth, d, n = ada_w.shape
    bsz = c.shape[0]
    tn = d
    return pl.pallas_call(
        _ada_kernel,
        grid=(depth, n // tn),
        in_specs=[pl.BlockSpec((bsz, d), lambda l, j: (0, 0)),
                  pl.BlockSpec((1, d, tn), lambda l, j: (l, 0, j)),
                  pl.BlockSpec((1, 1, tn), lambda l, j: (l, 0, j))],
        out_specs=pl.BlockSpec((1, bsz, tn), lambda l, j: (l, 0, j)),
        out_shape=jax.ShapeDtypeStruct((depth, bsz, n), F32),
        compiler_params=_params("arbitrary", "arbitrary"),
        name="ada_modulation",
    )(c, ada_w, ada_b.reshape(depth, 1, n))


N_MLA_PARAMS = 9
N_MOBA_PARAMS = 2


def _inproj_body(x, g_ref, sc_ref, sh_ref, w_ref, prep_refs, out_refs):
    z_ref, xbc_ref, dt_ref = out_refs[:3]
    mla_outs, moba_outs = out_refs[3:6], out_refs[6:10]
    mla_params, moba_params = prep_refs[:N_MLA_PARAMS], prep_refs[N_MLA_PARAMS:]
    ms = jnp.mean(x * x, axis=-1, keepdims=True)
    hm = (x * lax.rsqrt(ms + EPS) * g_ref[...]) * (1.0 + sc_ref[0]) + sh_ref[0]
    hb = hm.astype(BF16)
    mla_in = _dot(hb, w_ref[:, COL_MLA:COL_MOBA])
    q, kv, kpe = _mla_up_proj(mla_in, *mla_params[:4])
    moba_in = _dot(hb, w_ref[:, COL_MOBA:N_IN_PAD])
    dt_ref[0] = _dot(hb, w_ref[:, COL_DT:COL_MLA])
    _mla_prep_math(q, kv, kpe, *mla_params[4:], *mla_outs)
    _moba_prep_math(moba_in, *moba_params, *moba_outs)
    z_ref[0] = _dot(hb, w_ref[:, COL_Z:COL_XBC]).astype(z_ref.dtype)
    xbc_ref[0] = _dot(hb, w_ref[:, COL_XBC:COL_DT]).astype(xbc_ref.dtype)


def _inproj_kernel(x_ref, g_ref, sc_ref, sh_ref, w_ref, *refs):
    n = N_MLA_PARAMS + N_MOBA_PARAMS
    _inproj_body(x_ref[0], g_ref, sc_ref, sh_ref, w_ref, refs[:n], refs[n:])


def _inproj_res_kernel(x_ref, moe_ref, gf_ref, g_ref, sc_ref, sh_ref, w_ref, *refs):
    n = N_MLA_PARAMS + N_MOBA_PARAMS
    xo_ref = refs[n]
    x = x_ref[0] + gf_ref[0] * moe_ref[0].astype(F32)
    xo_ref[0] = x
    _inproj_body(x, g_ref, sc_ref, sh_ref, w_ref, refs[:n], refs[n + 1:])


def _inproj(x, moe_prev, gf_prev, g, sc, sh, w_pad, mla_params, moba_params, tm):
    bsz, s, d = x.shape
    nb_tile = tm // MOBA_BLOCK
    hw = ATT_HEADS * HEAD_PAD
    tok = lambda w: pl.BlockSpec((1, tm, w), lambda b, i: (b, i, 0))
    per_b = pl.BlockSpec((1, 1, d), lambda b, i: (b, 0, 0))
    full = lambda a: pl.BlockSpec(a.shape, lambda b, i: (0,) * a.ndim)
    pos = pl.BlockSpec((tm, HEAD_PAD), lambda b, i: (i, 0))
    vt_spec = pl.BlockSpec((1, hw, tm), lambda b, i: (b, 0, i))
    tok_shape = lambda w, dt: jax.ShapeDtypeStruct((bsz, s, w), dt)
    vt_shape = jax.ShapeDtypeStruct((bsz, hw, s), BF16)
    out_shapes = [tok_shape(SSD_WIDTH, BF16), tok_shape(SSD_CONV_DIM, BF16), tok_shape(LANES, F32),
                  tok_shape(hw, BF16), tok_shape(hw, BF16), vt_shape,
                  tok_shape(hw, BF16), tok_shape(hw, BF16), vt_shape,
                  jax.ShapeDtypeStruct((bsz, s // MOBA_BLOCK, 1, hw), F32)]
    out_specs = [tok(SSD_WIDTH), tok(SSD_CONV_DIM), tok(LANES),
                 tok(hw), tok(hw), vt_spec, tok(hw), tok(hw), vt_spec,
                 pl.BlockSpec((1, nb_tile, 1, hw), lambda b, i: (b, i, 0, 0))]
    mla_specs = [full(a) for a in mla_params[:4]] + [pos] * 3 + [full(a) for a in mla_params[7:]]
    prep = tuple(mla_params) + tuple(moba_params)
    prep_specs = mla_specs + [full(a) for a in moba_params]
    common_specs = [pl.BlockSpec((1, d), lambda b, i: (0, 0)), per_b, per_b,
                    pl.BlockSpec((d, N_IN_PAD), lambda b, i: (0, 0))] + prep_specs
    common = (g.reshape(1, d), sc, sh, w_pad) + prep
    if moe_prev is None:
        outs = pl.pallas_call(
            _inproj_kernel, grid=(bsz, s // tm),
            in_specs=[tok(d)] + common_specs, out_specs=out_specs, out_shape=out_shapes,
            compiler_params=_params("arbitrary", "arbitrary"), name="in_proj",
        )(x, *common)
        return x, outs
    outs = pl.pallas_call(
        _inproj_res_kernel, grid=(bsz, s // tm),
        in_specs=[tok(d), tok(d), per_b] + common_specs,
        out_specs=[tok(d)] + out_specs,
        out_shape=[jax.ShapeDtypeStruct((bsz, s, d), F32)] + out_shapes,
        compiler_params=_params("arbitrary", "arbitrary"), name="in_proj_res",
    )(x, moe_prev, gf_prev, *common)
    return outs[0], outs[1:]


SSD_STEP_CHUNKS = 4
assert SSD_STEP_CHUNKS % 2 == 0


def _ssd_kernel(z_ref, xbc_ref, dt_ref, cw_ref, cb_ref, dtb_ref, alog_ref, dskip_ref, ng_ref,
                expand_ref, o_ref, state_ref, ext_ref):
    L, G, N, P = SSD_CHUNK, SSD_GROUPS, SSD_STATE, SSD_HEAD_DIM
    R = SSD_HEADS // G
    GW = R * P
    first = pl.program_id(1) == 0

    @pl.when(first)
    def _():
        state_ref[...] = jnp.zeros_like(state_ref)
        ext_ref[L:2 * L, :] = jnp.zeros((L, SSD_CONV_DIM), BF16)

    row = lax.broadcasted_iota(jnp.int32, (L, L), 0)
    col = lax.broadcasted_iota(jnp.int32, (L, L), 1)
    causal = col <= row
    tril = jnp.where(causal, 1.0, 0.0).astype(BF16)
    a_row = -jnp.exp(alog_ref[...])
    t_i = lax.broadcasted_iota(jnp.int32, (L, 2 * L), 0)
    r_i = lax.broadcasted_iota(jnp.int32, (L, 2 * L), 1)
    shifts = []
    for half in range(2):
        per_tap = []
        for k in range(SSD_CONV - 1):
            v = t_i - (SSD_CONV - 1) + k
            src = jnp.where(v >= 0, half * L + v, (1 - half) * L + L + v)
            per_tap.append(jnp.where(r_i == src, 1.0, 0.0).astype(BF16))
        shifts.append(per_tap)

    for ck in range(SSD_STEP_CHUNKS):
        r0 = ck * L
        half = ck % 2
        cur = xbc_ref[0, r0:r0 + L, :]
        ext_ref[half * L:(half + 1) * L, :] = cur
        ext = ext_ref[...]
        conv = cb_ref[...] + cw_ref[SSD_CONV - 1:SSD_CONV, :] * cur.astype(F32)
        for k in range(SSD_CONV - 1):
            conv = conv + cw_ref[k:k + 1, :] * _dot(shifts[half][k], ext)
        xc = _silu(conv)
        xs = xc[:, :SSD_WIDTH]

        dt = _softplus(dt_ref[0, r0:r0 + L, :] + dtb_ref[...])
        a_cum = _dot_select_rows(tril, dt * a_row, 3)
        a_cum_t = a_cum.T
        a_last = a_cum[L - 1:L, :]
        stacked = jnp.concatenate(
            [dt, jnp.exp(a_cum), jnp.exp(a_last - a_cum),
             jnp.broadcast_to(jnp.exp(a_last), (8, LANES))], axis=0)
        wide = _dot_select_cols(stacked, expand_ref[...], 2)
        dt_w, ea_w, te_w, cd_w = wide[0:L], wide[L:2 * L], wide[2 * L:3 * L], wide[3 * L:3 * L + 1]
        xdt = xs * dt_w
        xdt_b = xdt.astype(BF16)
        xw_b = (xdt * te_w).astype(BF16)

        y_parts = []
        for g in range(G):
            bm = xc[:, SSD_WIDTH + g * N:SSD_WIDTH + (g + 1) * N]
            cm = xc[:, SSD_WIDTH + (G + g) * N:SSD_WIDTH + (G + g + 1) * N]
            bm_b, cm_b = bm.astype(BF16), cm.astype(BF16)
            cb = _dot_nt(cm_b, bm_b)
            st = state_ref[g]
            y_off = _dot(cm_b, st.astype(BF16)) * ea_w[:, g * GW:(g + 1) * GW]
            diag = []
            for r in range(R):
                h = g * R + r
                seg = a_cum[:, h:h + 1] - a_cum_t[h:h + 1, :]
                m = (cb * jnp.exp(jnp.where(causal, seg, -jnp.inf))).astype(BF16)
                diag.append(_dot(m, xdt_b[:, h * P:(h + 1) * P]))
            y_parts.append(jnp.concatenate(diag, axis=1) + y_off)
            state_ref[g] = st * cd_w[:, g * GW:(g + 1) * GW] + _dot(
                bm.T.astype(BF16), xw_b[:, g * GW:(g + 1) * GW])

        y = jnp.concatenate(y_parts, axis=1) + xs * dskip_ref[...]
        y = y * _silu(z_ref[0, r0:r0 + L, :].astype(F32))
        outs = []
        for g in range(G):
            yg = y[:, g * GW:(g + 1) * GW]
            outs.append(yg * lax.rsqrt(jnp.mean(yg * yg, axis=-1, keepdims=True) + EPS))
        o_ref[0, r0:r0 + L, :] = (jnp.concatenate(outs, axis=1) * ng_ref[...]).astype(o_ref.dtype)


def _ssd(z, xbc, dt, conv_w, conv_b, dt_bias, a_log, d_skip, norm_g):
    bsz, s, _ = z.shape
    ts = SSD_STEP_CHUNKS * SSD_CHUNK
    pad_h = lambda v: jnp.pad(v.astype(F32), (0, LANES - SSD_HEADS)).reshape(1, LANES)
    expand = jnp.repeat(jnp.eye(LANES, SSD_HEADS, dtype=BF16), SSD_HEAD_DIM, axis=1)
    tok = lambda w: pl.BlockSpec((1, ts, w), lambda b, i: (b, i, 0))
    full = lambda a: pl.BlockSpec(a.shape, lambda b, i: (0,) * a.ndim)
    params = (conv_w, conv_b.reshape(1, -1), pad_h(dt_bias), pad_h(a_log),
              jnp.repeat(d_skip, SSD_HEAD_DIM).reshape(1, -1), norm_g.reshape(1, -1), expand)
    return pl.pallas_call(
        _ssd_kernel, grid=(bsz, s // ts),
        in_specs=[tok(SSD_WIDTH), tok(SSD_CONV_DIM), tok(LANES)] + [full(p) for p in params],
        out_specs=tok(SSD_WIDTH),
        out_shape=jax.ShapeDtypeStruct((bsz, s, SSD_WIDTH), BF16),
        scratch_shapes=[pltpu.VMEM((SSD_GROUPS, SSD_STATE, SSD_WIDTH // SSD_GROUPS), F32),
                        pltpu.VMEM((2 * SSD_CHUNK, SSD_CONV_DIM), BF16)],
        compiler_params=_params("arbitrary", "arbitrary"), name="ssd_scan",
    )(z, xbc, dt, *params)


def _augment_v_t(v):
    lane = lax.broadcasted_iota(jnp.int32, (v.shape[0], HEAD_PAD), 1)
    tiles = []
    for p in range(v.shape[1] // HEAD_PAD):
        pt = v[:, p * HEAD_PAD:(p + 1) * HEAD_PAD]
        tiles.append(jnp.where(lane < V_PAD, pt, 1.0))
        tiles.append(jnp.where(lane < V_PAD, pltpu.roll(pt, V_PAD, axis=1), 1.0))
    return jnp.concatenate(tiles, axis=1).T


def _rope_mix(v, cos, sin_hi, sin_lo):
    w = v.shape[1]
    half = MLA_ROPE // 2
    return v * cos + pltpu.roll(v, half, axis=1) * sin_hi + pltpu.roll(v, w - half, axis=1) * sin_lo


def _mla_up_proj(xin, gq_ref, wq_ref, gkv_ref, wkv_ref):
    cq = xin[:, :MLA_Q_LORA]
    ckv = xin[:, MLA_Q_LORA:MLA_Q_LORA + MLA_KV_LORA]
    kpe = xin[:, MLA_Q_LORA + MLA_KV_LORA:]
    cqn = cq * lax.rsqrt(jnp.mean(cq * cq, axis=-1, keepdims=True) + EPS) * gq_ref[...]
    ckvn = ckv * lax.rsqrt(jnp.mean(ckv * ckv, axis=-1, keepdims=True) + EPS) * gkv_ref[...]
    q = _dot(cqn.astype(BF16), wq_ref[...])
    kv = _dot(ckvn.astype(BF16), wkv_ref[...])
    return q, kv, kpe


def _mla_prep_math(q, kv, kpe, cos_ref, shi_ref, slo_ref, qgain_ref, kgain_ref, q_ref, k_ref, v_ref):
    H = MLA_HEADS
    cos, shi, slo = cos_ref[...], shi_ref[...], slo_ref[...]
    tile = lambda t: jnp.concatenate([t] * H, axis=1)
    q = _rope_mix(q, tile(cos), tile(shi), tile(slo))
    kpe = _rope_mix(kpe, cos, shi, slo)
    scale = MLA_QK ** -0.5
    for h in range(H):
        sl = slice(h * HEAD_PAD, (h + 1) * HEAD_PAD)
        qh = q[:, sl]
        qn = qh * lax.rsqrt(jnp.sum(qh * qh, axis=-1, keepdims=True) / MLA_QK + EPS)
        q_ref[0, :, sl] = (qn * (qgain_ref[...] * scale)).astype(q_ref.dtype)
        kh = kv[:, sl] + kpe
        kn = kh * lax.rsqrt(jnp.sum(kh * kh, axis=-1, keepdims=True) / MLA_QK + EPS)
        k_ref[0, :, sl] = (kn * kgain_ref[...]).astype(k_ref.dtype)
    v_ref[0] = _augment_v_t(kv[:, H * HEAD_PAD:]).astype(v_ref.dtype)


def _rope_tables(s):
    half = MLA_ROPE // 2
    inv = ROPE_BASE ** (-2.0 * jnp.arange(half, dtype=F32) / MLA_ROPE)
    ang = jnp.arange(s, dtype=F32)[:, None] * inv[None, :]
    cos, sin = jnp.cos(ang), jnp.sin(ang)
    zeros = lambda n: jnp.zeros((s, n), F32)
    cos_t = jnp.concatenate([jnp.ones((s, MLA_NOPE), F32), cos, cos, zeros(HEAD_PAD - MLA_QK)], axis=1)
    sin_hi = jnp.concatenate([zeros(MLA_NOPE + half), sin, zeros(HEAD_PAD - MLA_QK)], axis=1)
    sin_lo = jnp.concatenate([zeros(MLA_NOPE), -sin, zeros(half + HEAD_PAD - MLA_QK)], axis=1)
    return cos_t, sin_hi, sin_lo


def _pad_heads(w, heads, width):
    lead = w.shape[:-1]
    w = w.reshape(*lead, heads, width)
    w = jnp.pad(w, [(0, 0)] * len(lead) + [(0, 0), (0, HEAD_PAD - width)])
    return w.reshape(*lead, heads * HEAD_PAD)


def _mla_prep_params(s, gq, w_uq, gkv, w_ukv, q_gain, k_gain):
    H = MLA_HEADS
    wq = _pad_heads(w_uq, H, MLA_QK).astype(BF16)
    wkv = w_ukv.reshape(MLA_KV_LORA, H, MLA_NOPE + MLA_V)
    wk = _pad_heads(wkv[:, :, :MLA_NOPE].reshape(MLA_KV_LORA, H * MLA_NOPE), H, MLA_NOPE)
    wv = wkv[:, :, MLA_NOPE:].reshape(MLA_KV_LORA, H * MLA_V)
    wkv_p = jnp.concatenate([wk, wv], axis=1).astype(BF16)
    cos_t, sin_hi, sin_lo = _rope_tables(s)
    pad_g = lambda v: jnp.pad(v, (0, HEAD_PAD - MLA_QK)).reshape(1, HEAD_PAD)
    return (gq.reshape(1, -1), wq, gkv.reshape(1, -1), wkv_p, cos_t, sin_hi, sin_lo,
            pad_g(q_gain), pad_g(k_gain))


GATE_LANE0 = MOBA_HEAD_DIM


def _moba_prep_math(xin, qgain_ref, kgain_ref, q_ref, k_ref, v_ref, km_ref):
    H, W, B = MOBA_HEADS, MOBA_HEADS * HEAD_PAD, MOBA_BLOCK
    tm = xin.shape[0]
    blk0 = pl.program_id(1) * (tm // B)
    lane = lax.broadcasted_iota(jnp.int32, (1, HEAD_PAD), 1)
    scale = MOBA_HEAD_DIM ** -0.5
    own = lax.broadcasted_iota(jnp.int32, (tm, HEAD_PAD), 1) < MOBA_HEAD_DIM
    QW = H * MOBA_HEAD_DIM

    def head_tile(base, h):
        pair = xin[:, base + (h // 2) * HEAD_PAD:base + (h // 2 + 1) * HEAD_PAD]
        t = pair if h % 2 == 0 else pltpu.roll(pair, MOBA_HEAD_DIM, axis=1)
        ss = jnp.sum(jnp.where(own, t * t, 0.0), axis=-1, keepdims=True)
        return t * lax.rsqrt(ss / MOBA_HEAD_DIM + EPS)

    for h in range(H):
        sl = slice(h * HEAD_PAD, (h + 1) * HEAD_PAD)
        q_ref[0, :, sl] = (head_tile(0, h) * (qgain_ref[...] * scale)).astype(q_ref.dtype)
        kn = head_tile(QW, h) * kgain_ref[...]
        for j in range(tm // B):
            rows = slice(j * B, (j + 1) * B)
            tag = (lane == GATE_LANE0 + blk0 + j).astype(F32)
            km_ref[0, j, :, sl] = jnp.mean(kn[rows], axis=0, keepdims=True)
            k_ref[0, rows, sl] = (kn[rows] + tag).astype(k_ref.dtype)
    v_ref[0] = _augment_v_t(xin[:, 2 * QW:]).astype(v_ref.dtype)


def _moba_prep_params(q_gain, k_gain):
    pad_g = lambda v: jnp.pad(v, (0, HEAD_PAD - MOBA_HEAD_DIM)).reshape(1, HEAD_PAD)
    return pad_g(q_gain), pad_g(k_gain)


def _attn_kernel(*refs, gated):
    if gated:
        q_ref, k_ref, vt_ref, b0_ref, b01_ref, kmr_ref, og_ref, o_ref, m_ref, acc_ref, qx_ref = refs
    else:
        q_ref, k_ref, vt_ref, b0_ref, b01_ref, og_ref, o_ref, m_ref, acc_ref, qx_ref = refs
    T = ATT_BLOCK
    H = ATT_HEADS
    j = pl.program_id(1)
    hsl = lambda h: slice(h * HEAD_PAD, (h + 1) * HEAD_PAD)

    def step(r, rows, qts_ref, bias_ref=None, first=False):
        sts = []
        for h in range(H):
            st = _dot(k_ref[0, pl.ds(r, rows), hsl(h)], qts_ref[h])
            sts.append(st if bias_ref is None else st + bias_ref[h])
        for h in range(H):
            vt = vt_ref[0, hsl(h), pl.ds(r, rows)]
            mx = jnp.max(sts[h], axis=0, keepdims=True)
            if first:
                m_ref[h] = mx
                acc_ref[h] = _dot(vt, jnp.exp(sts[h] - mx).astype(BF16))
            else:
                m_old = m_ref[h]
                m_new = jnp.maximum(m_old, mx)
                m_ref[h] = m_new
                acc_ref[h] = (jnp.exp(m_old - m_new) * acc_ref[h]
                              + _dot(vt, jnp.exp(sts[h] - m_new).astype(BF16)))

    for h in range(H):
        qt = q_ref[0, :, hsl(h)].astype(F32).T
        qt_b = qt.astype(BF16)
        if not gated:
            qx_ref[h] = qt_b
        else:
            nbp = -(-(k_ref.shape[1] // T) // 8) * 8
            g = _dot(kmr_ref[0, h], qt_b)[GATE_LANE0:GATE_LANE0 + nbp]
            row = lax.broadcasted_iota(jnp.int32, (nbp, T), 0)
            g = jnp.where(row < j, g, -jnp.inf)
            chosen = jnp.zeros((nbp, T), jnp.bool_)
            for _ in range(MOBA_TOPK):
                mx = jnp.max(g, axis=0, keepdims=True)
                hit = (g == mx) & (mx > -jnp.inf)
                first = jnp.min(jnp.where(hit, row, nbp), axis=0, keepdims=True)
                pick = row == first
                chosen = chosen | pick
                g = jnp.where(pick, -jnp.inf, g)
            neg = jnp.where(chosen | (row == j), 0.0, MASK_NEG)
            qx_ref[h] = jnp.concatenate(
                [qt[:GATE_LANE0], qt[GATE_LANE0:GATE_LANE0 + nbp] + neg, qt[GATE_LANE0 + nbp:]],
                axis=0).astype(BF16)

    @pl.when(j == 0)
    def _():
        step(0, T, qx_ref, b0_ref, first=True)

    @pl.when(j >= 1)
    def _():
        step(pl.multiple_of((j - 1) * T, T), 2 * T, qx_ref, b01_ref, first=True)

    n_far = jnp.maximum(j - 1, 0)

    def pair_body(i, carry):
        step(pl.multiple_of(i * (2 * T), 2 * T), 2 * T, qx_ref)
        return carry

    lax.fori_loop(0, lax.shift_right_logical(n_far, 1), pair_body, 0)

    @pl.when((n_far & 1) == 1)
    def _():
        step(pl.multiple_of((n_far - 1) * T, T), T, qx_ref)

    outs = []
    for h in range(H):
        a = acc_ref[h]
        outs.append(a[:V_PAD] * (1.0 / a[V_PAD:]))
    y = jnp.concatenate(outs, axis=0).T
    y = y * lax.rsqrt(jnp.mean(y * y, axis=-1, keepdims=True) + EPS) * og_ref[...]
    o_ref[0] = y.astype(o_ref.dtype)


def _attention(q, k, v, bias0, out_gain, bias1=None, kmr=None):
    bsz, s, w = q.shape
    T = ATT_BLOCK
    gated = kmr is not None
    qspec = pl.BlockSpec((1, T, w), lambda b, i: (b, i, 0))
    seq = lambda a: pl.BlockSpec((1,) + a.shape[1:], lambda b, i: (b,) + (0,) * (a.ndim - 1))
    full = lambda a: pl.BlockSpec(a.shape, lambda b, i: (0,) * a.ndim)
    og = out_gain.reshape(1, -1)
    bias01 = jnp.concatenate([bias1 if gated else jnp.zeros_like(bias0), bias0], axis=1)
    if gated:
        args = (q, k, v, bias0, bias01, kmr, og)
        specs = [qspec, seq(k), seq(v), full(bias0), full(bias01), seq(kmr), full(og)]
    else:
        args = (q, k, v, bias0, bias01, og)
        specs = [qspec, seq(k), seq(v), full(bias0), full(bias01), full(og)]
    wo = ATT_HEADS * V_PAD
    scratch = [pltpu.VMEM((ATT_HEADS, 1, T), F32), pltpu.VMEM((ATT_HEADS, HEAD_PAD, T), F32),
               pltpu.VMEM((ATT_HEADS, HEAD_PAD, T), BF16)]
    return pl.pallas_call(
        functools.partial(_attn_kernel, gated=gated), grid=(bsz, s // T),
        in_specs=specs, out_specs=pl.BlockSpec((1, T, wo), lambda b, i: (b, i, 0)),
        out_shape=jax.ShapeDtypeStruct((bsz, s, wo), BF16),
        scratch_shapes=scratch,
        compiler_params=_params("arbitrary", "arbitrary"),
        name="moba_attention" if gated else "mla_attention",
    )(*args)


def _rel_bucket(n):
    max_exact = REL_BUCKETS // 2
    nf = jnp.maximum(n, max_exact).astype(F32)
    large = max_exact + (jnp.log(nf / max_exact) / math.log(REL_MAX_DIST / max_exact)
                         * (REL_BUCKETS - max_exact)).astype(jnp.int32)
    large = jnp.minimum(large, REL_BUCKETS - 1)
    return jnp.where(n < max_exact, n, large)


def _moba_bias_tiles(rel_bias):
    T = ATT_BLOCK
    i = jnp.arange(T)[:, None]
    c = jnp.arange(T)[None, :]
    bias_t = rel_bias.T
    far = bias_t[:, REL_BUCKETS - 1][:, None, None]
    d0 = i - c

    def lookup(bucket):
        hit = bucket[None, :, :, None] == jnp.arange(REL_BUCKETS)[None, None, None, :]
        return jnp.sum(jnp.where(hit, bias_t[:, None, None, :], 0.0), axis=-1)

    b0 = jnp.where((d0 >= 0)[None], lookup(_rel_bucket(jnp.maximum(d0, 0))) - far, MASK_NEG)
    b1 = lookup(_rel_bucket(T + i - c)) - far
    return b0.astype(F32).transpose(0, 2, 1), b1.astype(F32).transpose(0, 2, 1)


def _causal_tile(heads):
    T = ATT_BLOCK
    i = jnp.arange(T)[:, None]
    c = jnp.arange(T)[None, :]
    return jnp.broadcast_to(jnp.where(c <= i, 0.0, MASK_NEG).astype(F32).T, (heads, T, T))


def _outproj_kernel(ys_ref, ya_ref, yb_ref, x_ref, w_ref, gm_ref, g_ref, sc_ref, sh_ref, rw_ref,
                    rb_ref, upper_ref, xo_ref, hf_ref, route_ref, counts_out_ref, count_ref):
    w0, w1 = SSD_WIDTH, SSD_WIDTH + MLA_HEADS * MLA_V
    y = (_dot(ys_ref[0], w_ref[0:w0, :]) + _dot(ya_ref[0], w_ref[w0:w1, :])
         + _dot(yb_ref[0], w_ref[w1:, :]))
    x = x_ref[0] + gm_ref[0] * y
    xo_ref[0] = x
    ms = jnp.mean(x * x, axis=-1, keepdims=True)
    hf = (x * lax.rsqrt(ms + EPS) * g_ref[...]) * (1.0 + sc_ref[0]) + sh_ref[0]
    hb = hf.astype(BF16)
    hf_ref[0] = hb
    logits_t = _dot(hb, rw_ref[...]).T
    score = _sigmoid(logits_t[0:N_EXPERTS, :])
    biased = score + rb_ref[...]
    E = EXPERTS_PER_GROUP
    gsum = []
    for g in range(N_EXPERT_GROUPS):
        r = [biased[g * E + e:g * E + e + 1, :] for e in range(E)]
        best = r[0] + r[1]
        for lo, hi in zip(PAIR_LO[1:], PAIR_HI[1:]):
            best = jnp.maximum(best, r[lo] + r[hi])
        gsum.append(best)
    gmax = functools.reduce(jnp.maximum, gsum)
    taken = jnp.zeros_like(gmax, dtype=jnp.bool_)
    gid = jnp.zeros_like(gmax)
    vb = [jnp.zeros_like(gmax) for _ in range(E)]
    for g in range(N_EXPERT_GROUPS):
        is_g = (gsum[g] == gmax) & jnp.logical_not(taken)
        taken = taken | is_g
        gid = jnp.where(is_g, float(g), gid)
        for e in range(E):
            vb[e] = jnp.where(is_g, biased[g * E + e:g * E + e + 1, :], vb[e])
    sel = []
    for e in range(E):
        rank = jnp.zeros_like(gmax)
        for o in range(E):
            if o < e:
                rank = rank + (vb[o] >= vb[e]).astype(F32)
            elif o > e:
                rank = rank + (vb[o] > vb[e]).astype(F32)
        sel.append(rank < 2.0)
    pidx = jnp.zeros_like(gmax)
    for p, (lo, hi) in enumerate(zip(PAIR_LO, PAIR_HI)):
        pidx = jnp.where(sel[lo] & sel[hi], float(p), pidx)
    bucket = gid * N_PAIRS + pidx
    @pl.when((pl.program_id(0) == 0) & (pl.program_id(1) == 0))
    def _():
        count_ref[...] = jnp.zeros_like(count_ref)

    tm = bucket.shape[1]
    brow = lax.broadcasted_iota(jnp.int32, (COUNT_ROWS, tm), 0).astype(F32)
    onehot = jnp.where(brow == bucket, 1.0, 0.0)
    prefix = _dot(onehot.astype(BF16), upper_ref[...])
    base = count_ref[...]
    rank = jnp.sum(onehot * (prefix - 1.0 + base[:, 0:1]), axis=0, keepdims=True)
    count_ref[...] = base + jnp.sum(onehot, axis=1, keepdims=True)
    counts_out_ref[...] = count_ref[...]
    rows = lax.broadcasted_iota(jnp.int32, (8, tm), 0)
    route_ref[0] = jnp.where(rows == 0, bucket, jnp.where(rows == 1, rank, 0.0))


def _outproj(y_ssd, y_mla, y_moba, x, w_out, gm, g, sc, sh, rw, router_bias, tm):
    bsz, s, d = x.shape
    tok = lambda w: pl.BlockSpec((1, tm, w), lambda b, i: (b, i, 0))
    per_b = pl.BlockSpec((1, 1, d), lambda b, i: (b, 0, 0))
    full = lambda a: pl.BlockSpec(a.shape, lambda b, i: (0,) * a.ndim)
    rb = router_bias.astype(F32).reshape(N_EXPERTS, 1)
    g2 = g.reshape(1, d)
    upper = (jnp.arange(tm)[:, None] <= jnp.arange(tm)[None, :]).astype(BF16)
    return pl.pallas_call(
        _outproj_kernel, grid=(bsz, s // tm),
        in_specs=[tok(y_ssd.shape[-1]), tok(y_mla.shape[-1]), tok(y_moba.shape[-1]), tok(d),
                  full(w_out), per_b, full(g2), per_b, per_b, full(rw), full(rb), full(upper)],
        out_specs=[tok(d), tok(d), pl.BlockSpec((1, 8, tm), lambda b, i: (b, 0, i)),
                   pl.BlockSpec((COUNT_ROWS, LANES), lambda b, i: (0, 0))],
        out_shape=[jax.ShapeDtypeStruct((bsz, s, d), F32), jax.ShapeDtypeStruct((bsz, s, d), BF16),
                   jax.ShapeDtypeStruct((bsz, 8, s), F32),
                   jax.ShapeDtypeStruct((COUNT_ROWS, LANES), F32)],
        scratch_shapes=[pltpu.VMEM((COUNT_ROWS, LANES), F32)],
        compiler_params=_params("arbitrary", "arbitrary"), name="out_proj_router",
    )(y_ssd, y_mla, y_moba, x, w_out, gm, g2, sc, sh, rw, rb, upper)


def _moe_kernel(ea_ref, eb_ref, nv_ref, x_ref, rw_ref, gua_ref, gub_ref, da_ref, db_ref, o_ref):
    i = pl.program_id(0)

    @pl.when(nv_ref[i] > 0)
    def _():
        xb = x_ref[...]
        logits = _dot(xb, rw_ref[...])
        lane = lax.broadcasted_iota(jnp.int32, logits.shape, 1)
        pick = lambda e: _sigmoid(jnp.sum(jnp.where(lane == e, logits, 0.0), axis=-1, keepdims=True))
        s_a, s_b = pick(ea_ref[i]), pick(eb_ref[i])
        tot = s_a + s_b
        out = None
        for gu_ref, d_ref, gate in ((gua_ref, da_ref, s_a / tot), (gub_ref, db_ref, s_b / tot)):
            gu = _dot(xb, gu_ref[0])
            hid = (_silu(gu[:, :D_EXPERT]) * gu[:, D_EXPERT:]).astype(BF16)
            y = _dot(hid, d_ref[0]) * gate
            out = y if out is None else out + y
        o_ref[...] = out.astype(o_ref.dtype)

    @pl.when(nv_ref[i] <= 0)
    def _():
        o_ref[...] = jnp.zeros_like(o_ref)


def _moe(xs, rw, tile_ea, tile_eb, tile_nv, w_gu, w_down):
    nslot, d = xs.shape
    nt = nslot // MOE_TILE
    grid_spec = pltpu.PrefetchScalarGridSpec(
        num_scalar_prefetch=3, grid=(nt,),
        in_specs=[pl.BlockSpec((MOE_TILE, d), lambda i, ea, eb, nv: (i, 0)),
                  pl.BlockSpec(rw.shape, lambda i, ea, eb, nv: (0, 0)),
                  pl.BlockSpec((1, d, 2 * D_EXPERT), lambda i, ea, eb, nv: (ea[i], 0, 0)),
                  pl.BlockSpec((1, d, 2 * D_EXPERT), lambda i, ea, eb, nv: (eb[i], 0, 0)),
                  pl.BlockSpec((1, D_EXPERT, d), lambda i, ea, eb, nv: (ea[i], 0, 0)),
                  pl.BlockSpec((1, D_EXPERT, d), lambda i, ea, eb, nv: (eb[i], 0, 0))],
        out_specs=pl.BlockSpec((MOE_TILE, d), lambda i, ea, eb, nv: (i, 0)))
    return pl.pallas_call(
        _moe_kernel, grid_spec=grid_spec,
        out_shape=jax.ShapeDtypeStruct((nslot, d), BF16),
        compiler_params=_params("arbitrary"), name="moe_ffn",
    )(tile_ea, tile_eb, tile_nv, xs, rw, w_gu, w_gu, w_down, w_down)


def _bucket_layout(bucket, rank, counts, n_tokens):
    nslot = n_tokens + N_BUCKETS * MOE_TILE
    nt = nslot // MOE_TILE
    padded = ((counts + MOE_TILE - 1) // MOE_TILE) * MOE_TILE
    pends = jnp.cumsum(padded)
    pstarts = pends - padded
    onehot = bucket[:, None] == jnp.arange(N_BUCKETS, dtype=jnp.int32)[None, :]
    token_slot = jnp.sum(jnp.where(onehot, pstarts[None, :], 0), axis=1).astype(jnp.int32) + rank
    tile_start = jnp.arange(nt, dtype=jnp.int32) * MOE_TILE
    tile_bucket = jnp.sum(tile_start[:, None] >= pends[None, :], axis=1).astype(jnp.int32)
    live = tile_bucket < N_BUCKETS
    tb = jnp.minimum(tile_bucket, N_BUCKETS - 1)
    tile_nv = jnp.where(live, jnp.clip(pstarts[tb] + counts[tb] - tile_start, 0, MOE_TILE), 0)
    npad = padded - counts
    pad_ends = jnp.cumsum(npad)
    q = jnp.arange(nslot - n_tokens, dtype=jnp.int32)
    qb = q[:, None] >= pad_ends[None, :]
    q_bucket = jnp.minimum(jnp.sum(qb, axis=1), N_BUCKETS - 1).astype(jnp.int32)
    sel = q_bucket[:, None] == jnp.arange(N_BUCKETS, dtype=jnp.int32)[None, :]
    pick = lambda v: jnp.sum(jnp.where(sel, v[None, :], 0), axis=1)
    pad_slot = pick(pstarts + counts) + q - pick(pad_ends - npad)
    pad_slot = jnp.where(q < pad_ends[-1], pad_slot, pends[-1] + q - pad_ends[-1]).astype(jnp.int32)
    keys = jnp.concatenate([token_slot, pad_slot])
    vals = jnp.concatenate([jnp.arange(n_tokens, dtype=jnp.int32), jnp.zeros((nslot - n_tokens,), jnp.int32)])
    _, slot_token = lax.sort_key_val(keys, vals)
    last_live = jnp.max(jnp.where(live, tb, 0))
    tb = jnp.where(live, tb, last_live)
    grp, pair = tb // N_PAIRS, tb % N_PAIRS
    tile_ea = grp * EXPERTS_PER_GROUP + jnp.asarray(PAIR_LO, jnp.int32)[pair]
    tile_eb = grp * EXPERTS_PER_GROUP + jnp.asarray(PAIR_HI, jnp.int32)[pair]
    return slot_token, token_slot, tile_ea.astype(jnp.int32), tile_eb.astype(jnp.int32), \
        tile_nv.astype(jnp.int32)


def _final_kernel(x_ref, moe_ref, gf_ref, o_ref):
    o_ref[0] = x_ref[0] + gf_ref[0] * moe_ref[0].astype(F32)


def _final_residual(x, moe, gf, tm):
    bsz, s, d = x.shape
    tok = pl.BlockSpec((1, tm, d), lambda b, i: (b, i, 0))
    return pl.pallas_call(
        _final_kernel, grid=(bsz, s // tm),
        in_specs=[tok, tok, pl.BlockSpec((1, 1, d), lambda b, i: (b, 0, 0))],
        out_specs=tok, out_shape=jax.ShapeDtypeStruct((bsz, s, d), F32),
        compiler_params=_params("arbitrary", "arbitrary"), name="final_residual",
    )(x, moe, gf)


def _pad_w_in(w_in):
    d = w_in.shape[0]
    sizes = (SSD_WIDTH, SSD_CONV_DIM, SSD_HEADS, MLA_Q_LORA, MLA_KV_LORA, MLA_ROPE,
             MOBA_HEADS * MOBA_HEAD_DIM, MOBA_HEADS * MOBA_HEAD_DIM, MOBA_HEADS * MOBA_HEAD_DIM)
    offs = np.cumsum((0,) + sizes)
    z, xbc, dtw, cq, ckv, krope, mq, mk, mv = [w_in[:, offs[i]:offs[i + 1]] for i in range(9)]
    zc = lambda n: jnp.zeros((d, n), w_in.dtype)
    kpe = jnp.concatenate([zc(MLA_NOPE), krope, zc(HEAD_PAD - MLA_QK)], axis=1)
    cols = [z, xbc, dtw, zc(LANES - SSD_HEADS), cq, ckv, kpe,
            mq, mk, mv]
    return jnp.concatenate(cols, axis=1).astype(BF16)


def kernel(x, c, ada_w, ada_b, norm_mix_g, norm_ffn_g, w_in, ssd_conv_w, ssd_conv_b, ssd_dt_bias,
           ssd_a_log, ssd_d, ssd_norm_g, mla_q_norm_g, mla_w_uq, mla_kv_norm_g, mla_w_ukv, mla_q_gain,
           mla_k_gain, mla_out_g, moba_q_gain, moba_k_gain, moba_out_g, rel_bias, w_out, router_w,
           router_bias, moe_w_gate, moe_w_up, moe_w_down):
    bsz, s, d = x.shape
    depth = ada_w.shape[0]
    tm = min(512, s)
    nb = s // MOBA_BLOCK
    mod = _modulation(c, ada_w, ada_b).reshape(depth, bsz, 6, 1, d)
    b0_moba, b1_moba = _moba_bias_tiles(rel_bias)
    b0_mla = _causal_tile(MLA_HEADS)
    rw = jnp.pad(router_w, ((0, 0), (0, LANES - N_EXPERTS))).astype(BF16)
    moe_prev, gf_prev = None, None
    for l in range(depth):
        sh_m, sc_m, g_m, sh_f, sc_f, g_f = [mod[l, :, i] for i in range(6)]
        mla_params = _mla_prep_params(s, mla_q_norm_g[l], mla_w_uq[l], mla_kv_norm_g[l], mla_w_ukv[l],
                                      mla_q_gain[l], mla_k_gain[l])
        moba_params = _moba_prep_params(moba_q_gain[l], moba_k_gain[l])
        x, (z, xbc, dt, q, k, v, mq, mk, mv, kmean) = _inproj(
            x, moe_prev, gf_prev, norm_mix_g[l], sc_m, sh_m, _pad_w_in(w_in[l]), mla_params,
            moba_params, tm)
        y_ssd = _ssd(z, xbc, dt, ssd_conv_w[l], ssd_conv_b[l], ssd_dt_bias[l], ssd_a_log[l],
                     ssd_d[l], ssd_norm_g[l])
        y_mla = _attention(q, k, v, b0_mla, mla_out_g[l])
        km = kmean.reshape(bsz, nb, MOBA_HEADS, HEAD_PAD).transpose(0, 2, 1, 3)
        kmr = jnp.pad(km, ((0, 0), (0, 0), (GATE_LANE0, HEAD_PAD - GATE_LANE0 - nb), (0, 0))).astype(BF16)
        y_moba = _attention(mq, mk, mv, b0_moba, moba_out_g[l], b1_moba, kmr)
        x, hf, route, counts = _outproj(y_ssd, y_mla, y_moba, x, w_out[l].astype(BF16), g_m,
                                        norm_ffn_g[l], sc_f, sh_f, rw, router_bias, tm)
        bucket = route[:, 0, :].reshape(-1).astype(jnp.int32)
        rank = route[:, 1, :].reshape(-1).astype(jnp.int32)
        slot_token, token_slot, tile_ea, tile_eb, tile_nv = _bucket_layout(
            bucket, rank, counts[:N_BUCKETS, 0].astype(jnp.int32), bsz * s)
        xs = hf.reshape(-1, d)[slot_token]
        w_gu = jnp.concatenate([moe_w_gate[l], moe_w_up[l]], axis=-1).astype(BF16)
        ys = _moe(xs, rw, tile_ea, tile_eb, tile_nv, w_gu, moe_w_down[l].astype(BF16))
        moe_prev = ys[token_slot].reshape(bsz, s, d)
        gf_prev = g_f
    return _final_residual(x, moe_prev, gf_prev, tm)
```

```python
import functools
import math

import numpy as np
import jax
import jax.numpy as jnp
from jax import lax
from jax.experimental import pallas as pl
from jax.experimental.pallas import tpu as pltpu

D_MODEL = 1024
DEPTH = 2
SSD_HEADS = 8
SSD_HEAD_DIM = 64
SSD_WIDTH = SSD_HEADS * SSD_HEAD_DIM
SSD_GROUPS = 2
SSD_STATE = 128
SSD_CONV = 4
SSD_CHUNK = 128
SSD_CONV_DIM = SSD_WIDTH + 2 * SSD_GROUPS * SSD_STATE
MLA_HEADS = 4
MLA_Q_LORA = 256
MLA_KV_LORA = 128
MLA_NOPE = 64
MLA_ROPE = 32
MLA_V = 64
MLA_QK = MLA_NOPE + MLA_ROPE
ROPE_BASE = 10000.0
MOBA_HEADS = 4
MOBA_HEAD_DIM = 64
MOBA_BLOCK = 256
MOBA_TOPK = 3
REL_BUCKETS = 32
REL_MAX_DIST = 128
N_EXPERTS = 16
N_EXPERT_GROUPS = 4
EXPERTS_PER_GROUP = 4
D_EXPERT = 512
EPS = 1e-6

LANES = 128
HEAD_PAD = 128
ATT_BLOCK = 256
ATT_HEADS = 4
V_PAD = 64
VMEM_LIMIT = 56 * 1024 * 1024
MASK_NEG = -1e9
N_PAIRS = 6
N_BUCKETS = N_EXPERT_GROUPS * N_PAIRS
COUNT_ROWS = 32
MOE_TILE = 256
PAIR_LO = (0, 0, 0, 1, 1, 2)
PAIR_HI = (1, 2, 3, 2, 3, 3)

COL_Z = 0
COL_XBC = COL_Z + SSD_WIDTH
COL_DT = COL_XBC + SSD_CONV_DIM
COL_MLA = COL_DT + LANES
MLA_IN_W = MLA_Q_LORA + MLA_KV_LORA + LANES
COL_MOBA = COL_MLA + MLA_IN_W
MOBA_IN_W = 2 * MOBA_HEADS * HEAD_PAD + MOBA_HEADS * MOBA_HEAD_DIM
N_IN_PAD = COL_MOBA + MOBA_IN_W

F32 = jnp.float32
BF16 = jnp.bfloat16


def _params(*sem):
    return pltpu.CompilerParams(dimension_semantics=sem, vmem_limit_bytes=VMEM_LIMIT)


def _dot(a, b):
    return jnp.dot(a, b, preferred_element_type=F32)


def _dot_nt(a, b):
    return lax.dot_general(a, b, (((1,), (1,)), ((), ())), preferred_element_type=F32)


def _dot_exact(a, b):
    return jnp.dot(a, b, preferred_element_type=F32, precision=lax.Precision.HIGHEST)


def _bf16_pieces(v, n):
    pieces = []
    for _ in range(n):
        p = v.astype(BF16)
        pieces.append(p)
        v = v - p.astype(F32)
    return pieces


def _dot_select_rows(sel01, v, n):
    sel = sel01.astype(BF16)
    return sum(_dot(sel, p) for p in _bf16_pieces(v, n))


def _dot_select_cols(v, sel01, n):
    sel = sel01.astype(BF16)
    return sum(_dot(p, sel) for p in _bf16_pieces(v, n))


def _silu(v):
    return v * (1.0 / (1.0 + jnp.exp(-v)))


def _sigmoid(v):
    return 1.0 / (1.0 + jnp.exp(-v))


def _softplus(v):
    return jnp.maximum(v, 0.0) + jnp.log1p(jnp.exp(-jnp.abs(v)))


def _ada_kernel(c_ref, w_ref, b_ref, o_ref):
    c = c_ref[...]
    o_ref[0] = _dot_exact(_silu(c), w_ref[0]) + b_ref[0]


def _modulation(c, ada_w, ada_b):
    depth, d, n = ada_w.shape
    bsz = c.shape[0]
    tn = d
    return pl.pallas_call(
        _ada_kernel,
        grid=(depth, n // tn),
        in_specs=[pl.BlockSpec((bsz, d), lambda l, j: (0, 0)),
                  pl.BlockSpec((1, d, tn), lambda l, j: (l, 0, j)),
                  pl.BlockSpec((1, 1, tn), lambda l, j: (l, 0, j))],
        out_specs=pl.BlockSpec((1, bsz, tn), lambda l, j: (l, 0, j)),
        out_shape=jax.ShapeDtypeStruct((depth, bsz, n), F32),
        compiler_params=_params("arbitrary", "arbitrary"),
        name="ada_modulation",
    )(c, ada_w, ada_b.reshape(depth, 1, n))


N_MLA_PARAMS = 9
N_MOBA_PARAMS = 2


def _inproj_body(x, g_ref, sc_ref, sh_ref, w_ref, prep_refs, out_refs):
    z_ref, xbc_ref, dt_ref = out_refs[:3]
    mla_outs, moba_outs = out_refs[3:6], out_refs[6:10]
    mla_params, moba_params = prep_refs[:N_MLA_PARAMS], prep_refs[N_MLA_PARAMS:]
    ms = jnp.mean(x * x, axis=-1, keepdims=True)
    hm = (x * lax.rsqrt(ms + EPS) * g_ref[...]) * (1.0 + sc_ref[0]) + sh_ref[0]
    hb = hm.astype(BF16)
    mla_in = _dot(hb, w_ref[:, COL_MLA:COL_MOBA])
    q, kv, kpe = _mla_up_proj(mla_in, *mla_params[:4])
    moba_in = _dot(hb, w_ref[:, COL_MOBA:N_IN_PAD])
    dt_ref[0] = _dot(hb, w_ref[:, COL_DT:COL_MLA])
    _mla_prep_math(q, kv, kpe, *mla_params[4:], *mla_outs)
    _moba_prep_math(moba_in, *moba_params, *moba_outs)
    z_ref[0] = _dot(hb, w_ref[:, COL_Z:COL_XBC]).astype(z_ref.dtype)
    xbc_ref[0] = _dot(hb, w_ref[:, COL_XBC:COL_DT]).astype(xbc_ref.dtype)


def _inproj_kernel(x_ref, g_ref, sc_ref, sh_ref, w_ref, *refs):
    n = N_MLA_PARAMS + N_MOBA_PARAMS
    _inproj_body(x_ref[0], g_ref, sc_ref, sh_ref, w_ref, refs[:n], refs[n:])


def _inproj_res_kernel(x_ref, moe_ref, gf_ref, g_ref, sc_ref, sh_ref, w_ref, *refs):
    n = N_MLA_PARAMS + N_MOBA_PARAMS
    xo_ref = refs[n]
    x = x_ref[0] + gf_ref[0] * moe_ref[0].astype(F32)
    xo_ref[0] = x
    _inproj_body(x, g_ref, sc_ref, sh_ref, w_ref, refs[:n], refs[n + 1:])


def _inproj(x, moe_prev, gf_prev, g, sc, sh, w_pad, mla_params, moba_params, tm):
    bsz, s, d = x.shape
    nb_tile = tm // MOBA_BLOCK
    hw = ATT_HEADS * HEAD_PAD
    tok = lambda w: pl.BlockSpec((1, tm, w), lambda b, i: (b, i, 0))
    per_b = pl.BlockSpec((1, 1, d), lambda b, i: (b, 0, 0))
    full = lambda a: pl.BlockSpec(a.shape, lambda b, i: (0,) * a.ndim)
    pos = pl.BlockSpec((tm, HEAD_PAD), lambda b, i: (i, 0))
    vt_spec = pl.BlockSpec((1, hw, tm), lambda b, i: (b, 0, i))
    tok_shape = lambda w, dt: jax.ShapeDtypeStruct((bsz, s, w), dt)
    vt_shape = jax.ShapeDtypeStruct((bsz, hw, s), BF16)
    out_shapes = [tok_shape(SSD_WIDTH, BF16), tok_shape(SSD_CONV_DIM, BF16), tok_shape(LANES, F32),
                  tok_shape(hw, BF16), tok_shape(hw, BF16), vt_shape,
                  tok_shape(hw, BF16), tok_shape(hw, BF16), vt_shape,
                  jax.ShapeDtypeStruct((bsz, s // MOBA_BLOCK, 1, hw), F32)]
    out_specs = [tok(SSD_WIDTH), tok(SSD_CONV_DIM), tok(LANES),
                 tok(hw), tok(hw), vt_spec, tok(hw), tok(hw), vt_spec,
                 pl.BlockSpec((1, nb_tile, 1, hw), lambda b, i: (b, i, 0, 0))]
    mla_specs = [full(a) for a in mla_params[:4]] + [pos] * 3 + [full(a) for a in mla_params[7:]]
    prep = tuple(mla_params) + tuple(moba_params)
    prep_specs = mla_specs + [full(a) for a in moba_params]
    common_specs = [pl.BlockSpec((1, d), lambda b, i: (0, 0)), per_b, per_b,
                    pl.BlockSpec((d, N_IN_PAD), lambda b, i: (0, 0))] + prep_specs
    common = (g.reshape(1, d), sc, sh, w_pad) + prep
    if moe_prev is None:
        outs = pl.pallas_call(
            _inproj_kernel, grid=(bsz, s // tm),
            in_specs=[tok(d)] + common_specs, out_specs=out_specs, out_shape=out_shapes,
            compiler_params=_params("arbitrary", "arbitrary"), name="in_proj",
        )(x, *common)
        return x, outs
    outs = pl.pallas_call(
        _inproj_res_kernel, grid=(bsz, s // tm),
        in_specs=[tok(d), tok(d), per_b] + common_specs,
        out_specs=[tok(d)] + out_specs,
        out_shape=[jax.ShapeDtypeStruct((bsz, s, d), F32)] + out_shapes,
        compiler_params=_params("arbitrary", "arbitrary"), name="in_proj_res",
    )(x, moe_prev, gf_prev, *common)
    return outs[0], outs[1:]


SSD_STEP_CHUNKS = 4
assert SSD_STEP_CHUNKS % 2 == 0


def _ssd_kernel(z_ref, xbc_ref, dt_ref, cw_ref, cb_ref, dtb_ref, alog_ref, dskip_ref, ng_ref,
                expand_ref, o_ref, state_ref, ext_ref):
    L, G, N, P = SSD_CHUNK, SSD_GROUPS, SSD_STATE, SSD_HEAD_DIM
    R = SSD_HEADS // G
    GW = R * P
    first = pl.program_id(1) == 0

    @pl.when(first)
    def _():
        state_ref[...] = jnp.zeros_like(state_ref)
        ext_ref[L:2 * L, :] = jnp.zeros((L, SSD_CONV_DIM), BF16)

    row = lax.broadcasted_iota(jnp.int32, (L, L), 0)
    col = lax.broadcasted_iota(jnp.int32, (L, L), 1)
    causal = col <= row
    tril = jnp.where(causal, 1.0, 0.0).astype(BF16)
    a_row = -jnp.exp(alog_ref[...])
    t_i = lax.broadcasted_iota(jnp.int32, (L, 2 * L), 0)
    r_i = lax.broadcasted_iota(jnp.int32, (L, 2 * L), 1)
    shifts = []
    for half in range(2):
        per_tap = []
        for k in range(SSD_CONV - 1):
            v = t_i - (SSD_CONV - 1) + k
            src = jnp.where(v >= 0, half * L + v, (1 - half) * L + L + v)
            per_tap.append(jnp.where(r_i == src, 1.0, 0.0).astype(BF16))
        shifts.append(per_tap)

    for ck in range(SSD_STEP_CHUNKS):
        r0 = ck * L
        half = ck % 2
        cur = xbc_ref[0, r0:r0 + L, :]
        ext_ref[half * L:(half + 1) * L, :] = cur
        ext = ext_ref[...]
        conv = cb_ref[...] + cw_ref[SSD_CONV - 1:SSD_CONV, :] * cur.astype(F32)
        for k in range(SSD_CONV - 1):
            conv = conv + cw_ref[k:k + 1, :] * _dot(shifts[half][k], ext)
        xc = _silu(conv)
        xs = xc[:, :SSD_WIDTH]

        dt = _softplus(dt_ref[0, r0:r0 + L, :] + dtb_ref[...])
        a_cum = _dot_select_rows(tril, dt * a_row, 3)
        a_cum_t = a_cum.T
        a_last = a_cum[L - 1:L, :]
        stacked = jnp.concatenate(
            [dt, jnp.exp(a_cum), jnp.exp(a_last - a_cum),
             jnp.broadcast_to(jnp.exp(a_last), (8, LANES))], axis=0)
        wide = _dot_select_cols(stacked, expand_ref[...], 2)
        dt_w, ea_w, te_w, cd_w = wide[0:L], wide[L:2 * L], wide[2 * L:3 * L], wide[3 * L:3 * L + 1]
        xdt = xs * dt_w
        xdt_b = xdt.astype(BF16)
        xw_b = (xdt * te_w).astype(BF16)

        y_parts = []
        for g in range(G):
            bm = xc[:, SSD_WIDTH + g * N:SSD_WIDTH + (g + 1) * N]
            cm = xc[:, SSD_WIDTH + (G + g) * N:SSD_WIDTH + (G + g + 1) * N]
            bm_b, cm_b = bm.astype(BF16), cm.astype(BF16)
            cb = _dot_nt(cm_b, bm_b)
            st = state_ref[g]
            y_off = _dot(cm_b, st.astype(BF16)) * ea_w[:, g * GW:(g + 1) * GW]
            diag = []
            for r in range(R):
                h = g * R + r
                seg = a_cum[:, h:h + 1] - a_cum_t[h:h + 1, :]
                m = (cb * jnp.exp(jnp.where(causal, seg, -jnp.inf))).astype(BF16)
                diag.append(_dot(m, xdt_b[:, h * P:(h + 1) * P]))
            y_parts.append(jnp.concatenate(diag, axis=1) + y_off)
            state_ref[g] = st * cd_w[:, g * GW:(g + 1) * GW] + _dot(
                bm.T.astype(BF16), xw_b[:, g * GW:(g + 1) * GW])

        y = jnp.concatenate(y_parts, axis=1) + xs * dskip_ref[...]
        y = y * _silu(z_ref[0, r0:r0 + L, :].astype(F32))
        outs = []
        for g in range(G):
            yg = y[:, g * GW:(g + 1) * GW]
            outs.append(yg * lax.rsqrt(jnp.mean(yg * yg, axis=-1, keepdims=True) + EPS))
        o_ref[0, r0:r0 + L, :] = (jnp.concatenate(outs, axis=1) * ng_ref[...]).astype(o_ref.dtype)


def _ssd(z, xbc, dt, conv_w, conv_b, dt_bias, a_log, d_skip, norm_g):
    bsz, s, _ = z.shape
    ts = SSD_STEP_CHUNKS * SSD_CHUNK
    pad_h = lambda v: jnp.pad(v.astype(F32), (0, LANES - SSD_HEADS)).reshape(1, LANES)
    expand = jnp.repeat(jnp.eye(LANES, SSD_HEADS, dtype=BF16), SSD_HEAD_DIM, axis=1)
    tok = lambda w: pl.BlockSpec((1, ts, w), lambda b, i: (b, i, 0))
    full = lambda a: pl.BlockSpec(a.shape, lambda b, i: (0,) * a.ndim)
    params = (conv_w, conv_b.reshape(1, -1), pad_h(dt_bias), pad_h(a_log),
              jnp.repeat(d_skip, SSD_HEAD_DIM).reshape(1, -1), norm_g.reshape(1, -1), expand)
    return pl.pallas_call(
        _ssd_kernel, grid=(bsz, s // ts),
        in_specs=[tok(SSD_WIDTH), tok(SSD_CONV_DIM), tok(LANES)] + [full(p) for p in params],
        out_specs=tok(SSD_WIDTH),
        out_shape=jax.ShapeDtypeStruct((bsz, s, SSD_WIDTH), BF16),
        scratch_shapes=[pltpu.VMEM((SSD_GROUPS, SSD_STATE, SSD_WIDTH // SSD_GROUPS), F32),
                        pltpu.VMEM((2 * SSD_CHUNK, SSD_CONV_DIM), BF16)],
        compiler_params=_params("arbitrary", "arbitrary"), name="ssd_scan",
    )(z, xbc, dt, *params)


def _augment_v_t(v):
    lane = lax.broadcasted_iota(jnp.int32, (v.shape[0], HEAD_PAD), 1)
    tiles = []
    for p in range(v.shape[1] // HEAD_PAD):
        pt = v[:, p * HEAD_PAD:(p + 1) * HEAD_PAD]
        tiles.append(jnp.where(lane < V_PAD, pt, 1.0))
        tiles.append(jnp.where(lane < V_PAD, pltpu.roll(pt, V_PAD, axis=1), 1.0))
    return jnp.concatenate(tiles, axis=1).T


def _rope_mix(v, cos, sin_hi, sin_lo):
    w = v.shape[1]
    half = MLA_ROPE // 2
    return v * cos + pltpu.roll(v, half, axis=1) * sin_hi + pltpu.roll(v, w - half, axis=1) * sin_lo


def _mla_up_proj(xin, gq_ref, wq_ref, gkv_ref, wkv_ref):
    cq = xin[:, :MLA_Q_LORA]
    ckv = xin[:, MLA_Q_LORA:MLA_Q_LORA + MLA_KV_LORA]
    kpe = xin[:, MLA_Q_LORA + MLA_KV_LORA:]
    cqn = cq * lax.rsqrt(jnp.mean(cq * cq, axis=-1, keepdims=True) + EPS) * gq_ref[...]
    ckvn = ckv * lax.rsqrt(jnp.mean(ckv * ckv, axis=-1, keepdims=True) + EPS) * gkv_ref[...]
    q = _dot(cqn.astype(BF16), wq_ref[...])
    kv = _dot(ckvn.astype(BF16), wkv_ref[...])
    return q, kv, kpe


def _mla_prep_math(q, kv, kpe, cos_ref, shi_ref, slo_ref, qgain_ref, kgain_ref, q_ref, k_ref, v_ref):
    H = MLA_HEADS
    cos, shi, slo = cos_ref[...], shi_ref[...], slo_ref[...]
    tile = lambda t: jnp.concatenate([t] * H, axis=1)
    q = _rope_mix(q, tile(cos), tile(shi), tile(slo))
    kpe = _rope_mix(kpe, cos, shi, slo)
    scale = MLA_QK ** -0.5
    for h in range(H):
        sl = slice(h * HEAD_PAD, (h + 1) * HEAD_PAD)
        qh = q[:, sl]
        qn = qh * lax.rsqrt(jnp.sum(qh * qh, axis=-1, keepdims=True) / MLA_QK + EPS)
        q_ref[0, :, sl] = (qn * (qgain_ref[...] * scale)).astype(q_ref.dtype)
        kh = kv[:, sl] + kpe
        kn = kh * lax.rsqrt(jnp.sum(kh * kh, axis=-1, keepdims=True) / MLA_QK + EPS)
        k_ref[0, :, sl] = (kn * kgain_ref[...]).astype(k_ref.dtype)
    v_ref[0] = _augment_v_t(kv[:, H * HEAD_PAD:]).astype(v_ref.dtype)


def _rope_tables(s):
    half = MLA_ROPE // 2
    inv = ROPE_BASE ** (-2.0 * jnp.arange(half, dtype=F32) / MLA_ROPE)
    ang = jnp.arange(s, dtype=F32)[:, None] * inv[None, :]
    cos, sin = jnp.cos(ang), jnp.sin(ang)
    zeros = lambda n: jnp.zeros((s, n), F32)
    cos_t = jnp.concatenate([jnp.ones((s, MLA_NOPE), F32), cos, cos, zeros(HEAD_PAD - MLA_QK)], axis=1)
    sin_hi = jnp.concatenate([zeros(MLA_NOPE + half), sin, zeros(HEAD_PAD - MLA_QK)], axis=1)
    sin_lo = jnp.concatenate([zeros(MLA_NOPE), -sin, zeros(half + HEAD_PAD - MLA_QK)], axis=1)
    return cos_t, sin_hi, sin_lo


def _pad_heads(w, heads, width):
    lead = w.shape[:-1]
    w = w.reshape(*lead, heads, width)
    w = jnp.pad(w, [(0, 0)] * len(lead) + [(0, 0), (0, HEAD_PAD - width)])
    return w.reshape(*lead, heads * HEAD_PAD)


def _mla_prep_params(s, gq, w_uq, gkv, w_ukv, q_gain, k_gain):
    H = MLA_HEADS
    wq = _pad_heads(w_uq, H, MLA_QK).astype(BF16)
    wkv = w_ukv.reshape(MLA_KV_LORA, H, MLA_NOPE + MLA_V)
    wk = _pad_heads(wkv[:, :, :MLA_NOPE].reshape(MLA_KV_LORA, H * MLA_NOPE), H, MLA_NOPE)
    wv = wkv[:, :, MLA_NOPE:].reshape(MLA_KV_LORA, H * MLA_V)
    wkv_p = jnp.concatenate([wk, wv], axis=1).astype(BF16)
    cos_t, sin_hi, sin_lo = _rope_tables(s)
    pad_g = lambda v: jnp.pad(v, (0, HEAD_PAD - MLA_QK)).reshape(1, HEAD_PAD)
    return (gq.reshape(1, -1), wq, gkv.reshape(1, -1), wkv_p, cos_t, sin_hi, sin_lo,
            pad_g(q_gain), pad_g(k_gain))


GATE_LANE0 = MOBA_HEAD_DIM


def _moba_prep_math(xin, qgain_ref, kgain_ref, q_ref, k_ref, v_ref, km_ref):
    H, W, B = MOBA_HEADS, MOBA_HEADS * HEAD_PAD, MOBA_BLOCK
    tm = xin.shape[0]
    blk0 = pl.program_id(1) * (tm // B)
    lane = lax.broadcasted_iota(jnp.int32, (1, HEAD_PAD), 1)
    scale = MOBA_HEAD_DIM ** -0.5
    for h in range(H):
        sl = slice(h * HEAD_PAD, (h + 1) * HEAD_PAD)
        qh = xin[:, sl]
        qn = qh * lax.rsqrt(jnp.sum(qh * qh, axis=-1, keepdims=True) / MOBA_HEAD_DIM + EPS)
        q_ref[0, :, sl] = (qn * (qgain_ref[...] * scale)).astype(q_ref.dtype)
        kh = xin[:, W + h * HEAD_PAD:W + (h + 1) * HEAD_PAD]
        kn = kh * lax.rsqrt(jnp.sum(kh * kh, axis=-1, keepdims=True) / MOBA_HEAD_DIM + EPS)
        kn = kn * kgain_ref[...]
        for j in range(tm // B):
            rows = slice(j * B, (j + 1) * B)
            tag = (lane == GATE_LANE0 + blk0 + j).astype(F32)
            km_ref[0, j, :, sl] = jnp.mean(kn[rows], axis=0, keepdims=True)
            k_ref[0, rows, sl] = (kn[rows] + tag).astype(k_ref.dtype)
    v_ref[0] = _augment_v_t(xin[:, 2 * W:]).astype(v_ref.dtype)


def _moba_prep_params(q_gain, k_gain):
    pad_g = lambda v: jnp.pad(v, (0, HEAD_PAD - MOBA_HEAD_DIM)).reshape(1, HEAD_PAD)
    return pad_g(q_gain), pad_g(k_gain)


def _attn_kernel(*refs, gated):
    if gated:
        q_ref, k_ref, vt_ref, b0_ref, b01_ref, kmr_ref, og_ref, o_ref, m_ref, acc_ref, qx_ref = refs
    else:
        q_ref, k_ref, vt_ref, b0_ref, b01_ref, og_ref, o_ref, m_ref, acc_ref, qx_ref = refs
    T = ATT_BLOCK
    H = ATT_HEADS
    j = pl.program_id(1)
    hsl = lambda h: slice(h * HEAD_PAD, (h + 1) * HEAD_PAD)

    def step(r, rows, qts_ref, bias_ref=None, first=False):
        sts = []
        for h in range(H):
            st = _dot(k_ref[0, pl.ds(r, rows), hsl(h)], qts_ref[h])
            sts.append(st if bias_ref is None else st + bias_ref[h])
        for h in range(H):
            vt = vt_ref[0, hsl(h), pl.ds(r, rows)]
            mx = jnp.max(sts[h], axis=0, keepdims=True)
            if first:
                m_ref[h] = mx
                acc_ref[h] = _dot(vt, jnp.exp(sts[h] - mx).astype(BF16))
            else:
                m_old = m_ref[h]
                m_new = jnp.maximum(m_old, mx)
                m_ref[h] = m_new
                acc_ref[h] = (jnp.exp(m_old - m_new) * acc_ref[h]
                              + _dot(vt, jnp.exp(sts[h] - m_new).astype(BF16)))

    for h in range(H):
        qt = q_ref[0, :, hsl(h)].astype(F32).T
        qt_b = qt.astype(BF16)
        if not gated:
            qx_ref[h] = qt_b
        else:
            nbp = -(-(k_ref.shape[1] // T) // 8) * 8
            g = _dot(kmr_ref[0, h], qt_b)[GATE_LANE0:GATE_LANE0 + nbp]
            row = lax.broadcasted_iota(jnp.int32, (nbp, T), 0)
            g = jnp.where(row < j, g, -jnp.inf)
            chosen = jnp.zeros((nbp, T), jnp.bool_)
            for _ in range(MOBA_TOPK):
                mx = jnp.max(g, axis=0, keepdims=True)
                hit = (g == mx) & (mx > -jnp.inf)
                first = jnp.min(jnp.where(hit, row, nbp), axis=0, keepdims=True)
                pick = row == first
                chosen = chosen | pick
                g = jnp.where(pick, -jnp.inf, g)
            neg = jnp.where(chosen | (row == j), 0.0, MASK_NEG)
            qx_ref[h] = jnp.concatenate(
                [qt[:GATE_LANE0], qt[GATE_LANE0:GATE_LANE0 + nbp] + neg, qt[GATE_LANE0 + nbp:]],
                axis=0).astype(BF16)

    @pl.when(j == 0)
    def _():
        step(0, T, qx_ref, b0_ref, first=True)

    @pl.when(j >= 1)
    def _():
        step(pl.multiple_of((j - 1) * T, T), 2 * T, qx_ref, b01_ref, first=True)

    n_far = jnp.maximum(j - 1, 0)

    def pair_body(i, carry):
        step(pl.multiple_of(i * (2 * T), 2 * T), 2 * T, qx_ref)
        return carry

    lax.fori_loop(0, lax.shift_right_logical(n_far, 1), pair_body, 0)

    @pl.when((n_far & 1) == 1)
    def _():
        step(pl.multiple_of((n_far - 1) * T, T), T, qx_ref)

    outs = []
    for h in range(H):
        a = acc_ref[h]
        outs.append(a[:V_PAD] * (1.0 / a[V_PAD:]))
    y = jnp.concatenate(outs, axis=0).T
    y = y * lax.rsqrt(jnp.mean(y * y, axis=-1, keepdims=True) + EPS) * og_ref[...]
    o_ref[0] = y.astype(o_ref.dtype)


def _attention(q, k, v, bias0, out_gain, bias1=None, kmr=None):
    bsz, s, w = q.shape
    T = ATT_BLOCK
    gated = kmr is not None
    qspec = pl.BlockSpec((1, T, w), lambda b, i: (b, i, 0))
    seq = lambda a: pl.BlockSpec((1,) + a.shape[1:], lambda b, i: (b,) + (0,) * (a.ndim - 1))
    full = lambda a: pl.BlockSpec(a.shape, lambda b, i: (0,) * a.ndim)
    og = out_gain.reshape(1, -1)
    bias01 = jnp.concatenate([bias1 if gated else jnp.zeros_like(bias0), bias0], axis=1)
    if gated:
        args = (q, k, v, bias0, bias01, kmr, og)
        specs = [qspec, seq(k), seq(v), full(bias0), full(bias01), seq(kmr), full(og)]
    else:
        args = (q, k, v, bias0, bias01, og)
        specs = [qspec, seq(k), seq(v), full(bias0), full(bias01), full(og)]
    wo = ATT_HEADS * V_PAD
    scratch = [pltpu.VMEM((ATT_HEADS, 1, T), F32), pltpu.VMEM((ATT_HEADS, HEAD_PAD, T), F32),
               pltpu.VMEM((ATT_HEADS, HEAD_PAD, T), BF16)]
    return pl.pallas_call(
        functools.partial(_attn_kernel, gated=gated), grid=(bsz, s // T),
        in_specs=specs, out_specs=pl.BlockSpec((1, T, wo), lambda b, i: (b, i, 0)),
        out_shape=jax.ShapeDtypeStruct((bsz, s, wo), BF16),
        scratch_shapes=scratch,
        compiler_params=_params("arbitrary", "arbitrary"),
        name="moba_attention" if gated else "mla_attention",
    )(*args)


def _rel_bucket(n):
    max_exact = REL_BUCKETS // 2
    nf = jnp.maximum(n, max_exact).astype(F32)
    large = max_exact + (jnp.log(nf / max_exact) / math.log(REL_MAX_DIST / max_exact)
                         * (REL_BUCKETS - max_exact)).astype(jnp.int32)
    large = jnp.minimum(large, REL_BUCKETS - 1)
    return jnp.where(n < max_exact, n, large)


def _moba_bias_tiles(rel_bias):
    T = ATT_BLOCK
    i = jnp.arange(T)[:, None]
    c = jnp.arange(T)[None, :]
    bias_t = rel_bias.T
    far = bias_t[:, REL_BUCKETS - 1][:, None, None]
    d0 = i - c

    def lookup(bucket):
        hit = bucket[None, :, :, None] == jnp.arange(REL_BUCKETS)[None, None, None, :]
        return jnp.sum(jnp.where(hit, bias_t[:, None, None, :], 0.0), axis=-1)

    b0 = jnp.where((d0 >= 0)[None], lookup(_rel_bucket(jnp.maximum(d0, 0))) - far, MASK_NEG)
    b1 = lookup(_rel_bucket(T + i - c)) - far
    return b0.astype(F32).transpose(0, 2, 1), b1.astype(F32).transpose(0, 2, 1)


def _causal_tile(heads):
    T = ATT_BLOCK
    i = jnp.arange(T)[:, None]
    c = jnp.arange(T)[None, :]
    return jnp.broadcast_to(jnp.where(c <= i, 0.0, MASK_NEG).astype(F32).T, (heads, T, T))


def _outproj_kernel(ys_ref, ya_ref, yb_ref, x_ref, w_ref, gm_ref, g_ref, sc_ref, sh_ref, rw_ref,
                    rb_ref, upper_ref, xo_ref, hf_ref, route_ref, counts_out_ref, count_ref):
    w0, w1 = SSD_WIDTH, SSD_WIDTH + MLA_HEADS * MLA_V
    y = (_dot(ys_ref[0], w_ref[0:w0, :]) + _dot(ya_ref[0], w_ref[w0:w1, :])
         + _dot(yb_ref[0], w_ref[w1:, :]))
    x = x_ref[0] + gm_ref[0] * y
    xo_ref[0] = x
    ms = jnp.mean(x * x, axis=-1, keepdims=True)
    hf = (x * lax.rsqrt(ms + EPS) * g_ref[...]) * (1.0 + sc_ref[0]) + sh_ref[0]
    hb = hf.astype(BF16)
    hf_ref[0] = hb
    logits_t = _dot(hb, rw_ref[...]).T
    score = _sigmoid(logits_t[0:N_EXPERTS, :])
    biased = score + rb_ref[...]
    E = EXPERTS_PER_GROUP
    gsum = []
    for g in range(N_EXPERT_GROUPS):
        r = [biased[g * E + e:g * E + e + 1, :] for e in range(E)]
        best = r[0] + r[1]
        for lo, hi in zip(PAIR_LO[1:], PAIR_HI[1:]):
            best = jnp.maximum(best, r[lo] + r[hi])
        gsum.append(best)
    gmax = functools.reduce(jnp.maximum, gsum)
    taken = jnp.zeros_like(gmax, dtype=jnp.bool_)
    gid = jnp.zeros_like(gmax)
    vb = [jnp.zeros_like(gmax) for _ in range(E)]
    for g in range(N_EXPERT_GROUPS):
        is_g = (gsum[g] == gmax) & jnp.logical_not(taken)
        taken = taken | is_g
        gid = jnp.where(is_g, float(g), gid)
        for e in range(E):
            vb[e] = jnp.where(is_g, biased[g * E + e:g * E + e + 1, :], vb[e])
    sel = []
    for e in range(E):
        rank = jnp.zeros_like(gmax)
        for o in range(E):
            if o < e:
                rank = rank + (vb[o] >= vb[e]).astype(F32)
            elif o > e:
                rank = rank + (vb[o] > vb[e]).astype(F32)
        sel.append(rank < 2.0)
    pidx = jnp.zeros_like(gmax)
    for p, (lo, hi) in enumerate(zip(PAIR_LO, PAIR_HI)):
        pidx = jnp.where(sel[lo] & sel[hi], float(p), pidx)
    bucket = gid * N_PAIRS + pidx
    @pl.when((pl.program_id(0) == 0) & (pl.program_id(1) == 0))
    def _():
        count_ref[...] = jnp.zeros_like(count_ref)

    tm = bucket.shape[1]
    brow = lax.broadcasted_iota(jnp.int32, (COUNT_ROWS, tm), 0).astype(F32)
    onehot = jnp.where(brow == bucket, 1.0, 0.0)
    prefix = _dot(onehot.astype(BF16), upper_ref[...])
    base = count_ref[...]
    rank = jnp.sum(onehot * (prefix - 1.0 + base[:, 0:1]), axis=0, keepdims=True)
    count_ref[...] = base + jnp.sum(onehot, axis=1, keepdims=True)
    counts_out_ref[...] = count_ref[...]
    rows = lax.broadcasted_iota(jnp.int32, (8, tm), 0)
    route_ref[0] = jnp.where(rows == 0, bucket, jnp.where(rows == 1, rank, 0.0))


def _outproj(y_ssd, y_mla, y_moba, x, w_out, gm, g, sc, sh, rw, router_bias, tm):
    bsz, s, d = x.shape
    tok = lambda w: pl.BlockSpec((1, tm, w), lambda b, i: (b, i, 0))
    per_b = pl.BlockSpec((1, 1, d), lambda b, i: (b, 0, 0))
    full = lambda a: pl.BlockSpec(a.shape, lambda b, i: (0,) * a.ndim)
    rb = router_bias.astype(F32).reshape(N_EXPERTS, 1)
    g2 = g.reshape(1, d)
    upper = (jnp.arange(tm)[:, None] <= jnp.arange(tm)[None, :]).astype(BF16)
    return pl.pallas_call(
        _outproj_kernel, grid=(bsz, s // tm),
        in_specs=[tok(y_ssd.shape[-1]), tok(y_mla.shape[-1]), tok(y_moba.shape[-1]), tok(d),
                  full(w_out), per_b, full(g2), per_b, per_b, full(rw), full(rb), full(upper)],
        out_specs=[tok(d), tok(d), pl.BlockSpec((1, 8, tm), lambda b, i: (b, 0, i)),
                   pl.BlockSpec((COUNT_ROWS, LANES), lambda b, i: (0, 0))],
        out_shape=[jax.ShapeDtypeStruct((bsz, s, d), F32), jax.ShapeDtypeStruct((bsz, s, d), BF16),
                   jax.ShapeDtypeStruct((bsz, 8, s), F32),
                   jax.ShapeDtypeStruct((COUNT_ROWS, LANES), F32)],
        scratch_shapes=[pltpu.VMEM((COUNT_ROWS, LANES), F32)],
        compiler_params=_params("arbitrary", "arbitrary"), name="out_proj_router",
    )(y_ssd, y_mla, y_moba, x, w_out, gm, g2, sc, sh, rw, rb, upper)


def _moe_kernel(ea_ref, eb_ref, nv_ref, x_ref, rw_ref, gua_ref, gub_ref, da_ref, db_ref, o_ref):
    i = pl.program_id(0)

    @pl.when(nv_ref[i] > 0)
    def _():
        xb = x_ref[...]
        logits = _dot(xb, rw_ref[...])
        lane = lax.broadcasted_iota(jnp.int32, logits.shape, 1)
        pick = lambda e: _sigmoid(jnp.sum(jnp.where(lane == e, logits, 0.0), axis=-1, keepdims=True))
        s_a, s_b = pick(ea_ref[i]), pick(eb_ref[i])
        tot = s_a + s_b
        out = None
        for gu_ref, d_ref, gate in ((gua_ref, da_ref, s_a / tot), (gub_ref, db_ref, s_b / tot)):
            gu = _dot(xb, gu_ref[0])
            hid = (_silu(gu[:, :D_EXPERT]) * gu[:, D_EXPERT:]).astype(BF16)
            y = _dot(hid, d_ref[0]) * gate
            out = y if out is None else out + y
        o_ref[...] = out.astype(o_ref.dtype)

    @pl.when(nv_ref[i] <= 0)
    def _():
        o_ref[...] = jnp.zeros_like(o_ref)


def _moe(xs, rw, tile_ea, tile_eb, tile_nv, w_gu, w_down):
    nslot, d = xs.shape
    nt = nslot // MOE_TILE
    grid_spec = pltpu.PrefetchScalarGridSpec(
        num_scalar_prefetch=3, grid=(nt,),
        in_specs=[pl.BlockSpec((MOE_TILE, d), lambda i, ea, eb, nv: (i, 0)),
                  pl.BlockSpec(rw.shape, lambda i, ea, eb, nv: (0, 0)),
                  pl.BlockSpec((1, d, 2 * D_EXPERT), lambda i, ea, eb, nv: (ea[i], 0, 0)),
                  pl.BlockSpec((1, d, 2 * D_EXPERT), lambda i, ea, eb, nv: (eb[i], 0, 0)),
                  pl.BlockSpec((1, D_EXPERT, d), lambda i, ea, eb, nv: (ea[i], 0, 0)),
                  pl.BlockSpec((1, D_EXPERT, d), lambda i, ea, eb, nv: (eb[i], 0, 0))],
        out_specs=pl.BlockSpec((MOE_TILE, d), lambda i, ea, eb, nv: (i, 0)))
    return pl.pallas_call(
        _moe_kernel, grid_spec=grid_spec,
        out_shape=jax.ShapeDtypeStruct((nslot, d), BF16),
        compiler_params=_params("arbitrary"), name="moe_ffn",
    )(tile_ea, tile_eb, tile_nv, xs, rw, w_gu, w_gu, w_down, w_down)


def _bucket_layout(bucket, rank, counts, n_tokens):
    nslot = n_tokens + N_BUCKETS * MOE_TILE
    nt = nslot // MOE_TILE
    padded = ((counts + MOE_TILE - 1) // MOE_TILE) * MOE_TILE
    pends = jnp.cumsum(padded)
    pstarts = pends - padded
    onehot = bucket[:, None] == jnp.arange(N_BUCKETS, dtype=jnp.int32)[None, :]
    token_slot = jnp.sum(jnp.where(onehot, pstarts[None, :], 0), axis=1).astype(jnp.int32) + rank
    tile_start = jnp.arange(nt, dtype=jnp.int32) * MOE_TILE
    tile_bucket = jnp.sum(tile_start[:, None] >= pends[None, :], axis=1).astype(jnp.int32)
    live = tile_bucket < N_BUCKETS
    tb = jnp.minimum(tile_bucket, N_BUCKETS - 1)
    tile_nv = jnp.where(live, jnp.clip(pstarts[tb] + counts[tb] - tile_start, 0, MOE_TILE), 0)
    npad = padded - counts
    pad_ends = jnp.cumsum(npad)
    q = jnp.arange(nslot - n_tokens, dtype=jnp.int32)
    qb = q[:, None] >= pad_ends[None, :]
    q_bucket = jnp.minimum(jnp.sum(qb, axis=1), N_BUCKETS - 1).astype(jnp.int32)
    sel = q_bucket[:, None] == jnp.arange(N_BUCKETS, dtype=jnp.int32)[None, :]
    pick = lambda v: jnp.sum(jnp.where(sel, v[None, :], 0), axis=1)
    pad_slot = pick(pstarts + counts) + q - pick(pad_ends - npad)
    pad_slot = jnp.where(q < pad_ends[-1], pad_slot, pends[-1] + q - pad_ends[-1]).astype(jnp.int32)
    keys = jnp.concatenate([token_slot, pad_slot])
    vals = jnp.concatenate([jnp.arange(n_tokens, dtype=jnp.int32), jnp.zeros((nslot - n_tokens,), jnp.int32)])
    _, slot_token = lax.sort_key_val(keys, vals)
    last_live = jnp.max(jnp.where(live, tb, 0))
    tb = jnp.where(live, tb, last_live)
    grp, pair = tb // N_PAIRS, tb % N_PAIRS
    tile_ea = grp * EXPERTS_PER_GROUP + jnp.asarray(PAIR_LO, jnp.int32)[pair]
    tile_eb = grp * EXPERTS_PER_GROUP + jnp.asarray(PAIR_HI, jnp.int32)[pair]
    return slot_token, token_slot, tile_ea.astype(jnp.int32), tile_eb.astype(jnp.int32), \
        tile_nv.astype(jnp.int32)


def _final_kernel(x_ref, moe_ref, gf_ref, o_ref):
    o_ref[0] = x_ref[0] + gf_ref[0] * moe_ref[0].astype(F32)


def _final_residual(x, moe, gf, tm):
    bsz, s, d = x.shape
    tok = pl.BlockSpec((1, tm, d), lambda b, i: (b, i, 0))
    return pl.pallas_call(
        _final_kernel, grid=(bsz, s // tm),
        in_specs=[tok, tok, pl.BlockSpec((1, 1, d), lambda b, i: (b, 0, 0))],
        out_specs=tok, out_shape=jax.ShapeDtypeStruct((bsz, s, d), F32),
        compiler_params=_params("arbitrary", "arbitrary"), name="final_residual",
    )(x, moe, gf)


def _pad_w_in(w_in):
    d = w_in.shape[0]
    sizes = (SSD_WIDTH, SSD_CONV_DIM, SSD_HEADS, MLA_Q_LORA, MLA_KV_LORA, MLA_ROPE,
             MOBA_HEADS * MOBA_HEAD_DIM, MOBA_HEADS * MOBA_HEAD_DIM, MOBA_HEADS * MOBA_HEAD_DIM)
    offs = np.cumsum((0,) + sizes)
    z, xbc, dtw, cq, ckv, krope, mq, mk, mv = [w_in[:, offs[i]:offs[i + 1]] for i in range(9)]
    zc = lambda n: jnp.zeros((d, n), w_in.dtype)
    kpe = jnp.concatenate([zc(MLA_NOPE), krope, zc(HEAD_PAD - MLA_QK)], axis=1)
    cols = [z, xbc, dtw, zc(LANES - SSD_HEADS), cq, ckv, kpe,
            _pad_heads(mq, MOBA_HEADS, MOBA_HEAD_DIM), _pad_heads(mk, MOBA_HEADS, MOBA_HEAD_DIM), mv]
    return jnp.concatenate(cols, axis=1).astype(BF16)


def kernel(x, c, ada_w, ada_b, norm_mix_g, norm_ffn_g, w_in, ssd_conv_w, ssd_conv_b, ssd_dt_bias,
           ssd_a_log, ssd_d, ssd_norm_g, mla_q_norm_g, mla_w_uq, mla_kv_norm_g, mla_w_ukv, mla_q_gain,
           mla_k_gain, mla_out_g, moba_q_gain, moba_k_gain, moba_out_g, rel_bias, w_out, router_w,
           router_bias, moe_w_gate, moe_w_up, moe_w_down):
    bsz, s, d = x.shape
    depth = ada_w.shape[0]
    tm = min(512, s)
    nb = s // MOBA_BLOCK
    mod = _modulation(c, ada_w, ada_b).reshape(depth, bsz, 6, 1, d)
    b0_moba, b1_moba = _moba_bias_tiles(rel_bias)
    b0_mla = _causal_tile(MLA_HEADS)
    rw = jnp.pad(router_w, ((0, 0), (0, LANES - N_EXPERTS))).astype(BF16)
    moe_prev, gf_prev = None, None
    for l in range(depth):
        sh_m, sc_m, g_m, sh_f, sc_f, g_f = [mod[l, :, i] for i in range(6)]
        mla_params = _mla_prep_params(s, mla_q_norm_g[l], mla_w_uq[l], mla_kv_norm_g[l], mla_w_ukv[l],
                                      mla_q_gain[l], mla_k_gain[l])
        moba_params = _moba_prep_params(moba_q_gain[l], moba_k_gain[l])
        x, (z, xbc, dt, q, k, v, mq, mk, mv, kmean) = _inproj(
            x, moe_prev, gf_prev, norm_mix_g[l], sc_m, sh_m, _pad_w_in(w_in[l]), mla_params,
            moba_params, tm)
        y_ssd = _ssd(z, xbc, dt, ssd_conv_w[l], ssd_conv_b[l], ssd_dt_bias[l], ssd_a_log[l],
                     ssd_d[l], ssd_norm_g[l])
        y_mla = _attention(q, k, v, b0_mla, mla_out_g[l])
        km = kmean.reshape(bsz, nb, MOBA_HEADS, HEAD_PAD).transpose(0, 2, 1, 3)
        kmr = jnp.pad(km, ((0, 0), (0, 0), (GATE_LANE0, HEAD_PAD - GATE_LANE0 - nb), (0, 0))).astype(BF16)
        y_moba = _attention(mq, mk, mv, b0_moba, moba_out_g[l], b1_moba, kmr)
        x, hf, route, counts = _outproj(y_ssd, y_mla, y_moba, x, w_out[l].astype(BF16), g_m,
                                        norm_ffn_g[l], sc_f, sh_f, rw, router_bias, tm)
        bucket = route[:, 0, :].reshape(-1).astype(jnp.int32)
        rank = route[:, 1, :].reshape(-1).astype(jnp.int32)
        slot_token, token_slot, tile_ea, tile_eb, tile_nv = _bucket_layout(
            bucket, rank, counts[:N_BUCKETS, 0].astype(jnp.int32), bsz * s)
        xs = hf.reshape(-1, d)[slot_token]
        w_gu = jnp.concatenate([moe_w_gate[l], moe_w_up[l]], axis=-1).astype(BF16)
        ys = _moe(xs, rw, tile_ea, tile_eb, tile_nv, w_gu, moe_w_down[l].astype(BF16))
        moe_prev = ys[token_slot].reshape(bsz, s, d)
        gf_prev = g_f
    return _final_residual(x, moe_prev, gf_prev, tm)
```

```python
import functools
import math

import numpy as np
import jax
import jax.numpy as jnp
from jax import lax
from jax.experimental import pallas as pl
from jax.experimental.pallas import tpu as pltpu

D_MODEL = 1024
DEPTH = 2
SSD_HEADS = 8
SSD_HEAD_DIM = 64
SSD_WIDTH = SSD_HEADS * SSD_HEAD_DIM
SSD_GROUPS = 2
SSD_STATE = 128
SSD_CONV = 4
SSD_CHUNK = 128
SSD_CONV_DIM = SSD_WIDTH + 2 * SSD_GROUPS * SSD_STATE
MLA_HEADS = 4
MLA_Q_LORA = 256
MLA_KV_LORA = 128
MLA_NOPE = 64
MLA_ROPE = 32
MLA_V = 64
MLA_QK = MLA_NOPE + MLA_ROPE
ROPE_BASE = 10000.0
MOBA_HEADS = 4
MOBA_HEAD_DIM = 64
MOBA_BLOCK = 256
MOBA_TOPK = 3
REL_BUCKETS = 32
REL_MAX_DIST = 128
N_EXPERTS = 16
N_EXPERT_GROUPS = 4
EXPERTS_PER_GROUP = 4
D_EXPERT = 512
EPS = 1e-6

LANES = 128
HEAD_PAD = 128
ATT_BLOCK = 256
ATT_HEADS = 4
V_PAD = 64
VT_ROWS = 80
VMEM_LIMIT = 56 * 1024 * 1024
MASK_NEG = -1e9
N_PAIRS = 6
N_BUCKETS = N_EXPERT_GROUPS * N_PAIRS
COUNT_ROWS = 32
MOE_TILE = 256
PAIR_LO = (0, 0, 0, 1, 1, 2)
PAIR_HI = (1, 2, 3, 2, 3, 3)

COL_Z = 0
COL_XBC = COL_Z + SSD_WIDTH
COL_DT = COL_XBC + SSD_CONV_DIM
COL_MLA = COL_DT + LANES
MLA_IN_W = MLA_Q_LORA + MLA_KV_LORA + LANES
COL_MOBA = COL_MLA + MLA_IN_W
MOBA_IN_W = 2 * MOBA_HEADS * HEAD_PAD + MOBA_HEADS * MOBA_HEAD_DIM
N_IN_PAD = COL_MOBA + MOBA_IN_W

F32 = jnp.float32
BF16 = jnp.bfloat16


def _params(*sem):
    return pltpu.CompilerParams(dimension_semantics=sem, vmem_limit_bytes=VMEM_LIMIT)


def _dot(a, b):
    return jnp.dot(a, b, preferred_element_type=F32)


def _dot_nt(a, b):
    return lax.dot_general(a, b, (((1,), (1,)), ((), ())), preferred_element_type=F32)


def _dot_exact(a, b):
    return jnp.dot(a, b, preferred_element_type=F32, precision=lax.Precision.HIGHEST)


def _bf16_pieces(v, n):
    pieces = []
    for _ in range(n):
        p = v.astype(BF16)
        pieces.append(p)
        v = v - p.astype(F32)
    return pieces


def _dot_select_rows(sel01, v, n):
    sel = sel01.astype(BF16)
    return sum(_dot(sel, p) for p in _bf16_pieces(v, n))


def _dot_select_cols(v, sel01, n):
    sel = sel01.astype(BF16)
    return sum(_dot(p, sel) for p in _bf16_pieces(v, n))


def _silu(v):
    return v * (1.0 / (1.0 + jnp.exp(-v)))


def _sigmoid(v):
    return 1.0 / (1.0 + jnp.exp(-v))


def _softplus(v):
    return jnp.maximum(v, 0.0) + jnp.log1p(jnp.exp(-jnp.abs(v)))


def _ada_kernel(c_ref, w_ref, b_ref, o_ref):
    c = c_ref[...]
    o_ref[0] = _dot_exact(_silu(c), w_ref[0]) + b_ref[0]


def _modulation(c, ada_w, ada_b):
    depth, d, n = ada_w.shape
    bsz = c.shape[0]
    tn = d
    return pl.pallas_call(
        _ada_kernel,
        grid=(depth, n // tn),
        in_specs=[pl.BlockSpec((bsz, d), lambda l, j: (0, 0)),
                  pl.BlockSpec((1, d, tn), lambda l, j: (l, 0, j)),
                  pl.BlockSpec((1, 1, tn), lambda l, j: (l, 0, j))],
        out_specs=pl.BlockSpec((1, bsz, tn), lambda l, j: (l, 0, j)),
        out_shape=jax.ShapeDtypeStruct((depth, bsz, n), F32),
        compiler_params=_params("arbitrary", "arbitrary"),
        name="ada_modulation",
    )(c, ada_w, ada_b.reshape(depth, 1, n))


N_MLA_PARAMS = 9
N_MOBA_PARAMS = 2


def _inproj_body(x, g_ref, sc_ref, sh_ref, w_ref, prep_refs, out_refs):
    z_ref, xbc_ref, dt_ref = out_refs[:3]
    mla_outs, moba_outs = out_refs[3:6], out_refs[6:10]
    mla_params, moba_params = prep_refs[:N_MLA_PARAMS], prep_refs[N_MLA_PARAMS:]
    ms = jnp.mean(x * x, axis=-1, keepdims=True)
    hm = (x * lax.rsqrt(ms + EPS) * g_ref[...]) * (1.0 + sc_ref[0]) + sh_ref[0]
    hb = hm.astype(BF16)
    mla_in = _dot(hb, w_ref[:, COL_MLA:COL_MOBA])
    q, kv, kpe = _mla_up_proj(mla_in, *mla_params[:4])
    moba_in = _dot(hb, w_ref[:, COL_MOBA:N_IN_PAD])
    dt_ref[0] = _dot(hb, w_ref[:, COL_DT:COL_MLA])
    _mla_prep_math(q, kv, kpe, *mla_params[4:], *mla_outs)
    _moba_prep_math(moba_in, *moba_params, *moba_outs)
    z_ref[0] = _dot(hb, w_ref[:, COL_Z:COL_XBC]).astype(z_ref.dtype)
    xbc_ref[0] = _dot(hb, w_ref[:, COL_XBC:COL_DT]).astype(xbc_ref.dtype)


def _inproj_kernel(x_ref, g_ref, sc_ref, sh_ref, w_ref, *refs):
    n = N_MLA_PARAMS + N_MOBA_PARAMS
    _inproj_body(x_ref[0], g_ref, sc_ref, sh_ref, w_ref, refs[:n], refs[n:])


def _inproj_res_kernel(x_ref, moe_ref, gf_ref, g_ref, sc_ref, sh_ref, w_ref, *refs):
    n = N_MLA_PARAMS + N_MOBA_PARAMS
    xo_ref = refs[n]
    x = x_ref[0] + gf_ref[0] * moe_ref[0].astype(F32)
    xo_ref[0] = x
    _inproj_body(x, g_ref, sc_ref, sh_ref, w_ref, refs[:n], refs[n + 1:])


def _inproj(x, moe_prev, gf_prev, g, sc, sh, w_pad, mla_params, moba_params, tm):
    bsz, s, d = x.shape
    nb_tile = tm // MOBA_BLOCK
    hw = ATT_HEADS * HEAD_PAD
    tok = lambda w: pl.BlockSpec((1, tm, w), lambda b, i: (b, i, 0))
    per_b = pl.BlockSpec((1, 1, d), lambda b, i: (b, 0, 0))
    full = lambda a: pl.BlockSpec(a.shape, lambda b, i: (0,) * a.ndim)
    pos = pl.BlockSpec((tm, HEAD_PAD), lambda b, i: (i, 0))
    vr = ATT_HEADS * VT_ROWS
    vt_spec = pl.BlockSpec((1, vr, tm), lambda b, i: (b, 0, i))
    tok_shape = lambda w, dt: jax.ShapeDtypeStruct((bsz, s, w), dt)
    vt_shape = jax.ShapeDtypeStruct((bsz, vr, s), BF16)
    out_shapes = [tok_shape(SSD_WIDTH, BF16), tok_shape(SSD_CONV_DIM, BF16), tok_shape(LANES, F32),
                  tok_shape(hw, BF16), tok_shape(hw, BF16), vt_shape,
                  tok_shape(hw, BF16), tok_shape(hw, BF16), vt_shape,
                  jax.ShapeDtypeStruct((bsz, s // MOBA_BLOCK, 1, hw), F32)]
    out_specs = [tok(SSD_WIDTH), tok(SSD_CONV_DIM), tok(LANES),
                 tok(hw), tok(hw), vt_spec, tok(hw), tok(hw), vt_spec,
                 pl.BlockSpec((1, nb_tile, 1, hw), lambda b, i: (b, i, 0, 0))]
    mla_specs = [full(a) for a in mla_params[:4]] + [pos] * 3 + [full(a) for a in mla_params[7:]]
    prep = tuple(mla_params) + tuple(moba_params)
    prep_specs = mla_specs + [full(a) for a in moba_params]
    common_specs = [pl.BlockSpec((1, d), lambda b, i: (0, 0)), per_b, per_b,
                    pl.BlockSpec((d, N_IN_PAD), lambda b, i: (0, 0))] + prep_specs
    common = (g.reshape(1, d), sc, sh, w_pad) + prep
    if moe_prev is None:
        outs = pl.pallas_call(
            _inproj_kernel, grid=(bsz, s // tm),
            in_specs=[tok(d)] + common_specs, out_specs=out_specs, out_shape=out_shapes,
            compiler_params=_params("arbitrary", "arbitrary"), name="in_proj",
        )(x, *common)
        return x, outs
    outs = pl.pallas_call(
        _inproj_res_kernel, grid=(bsz, s // tm),
        in_specs=[tok(d), tok(d), per_b] + common_specs,
        out_specs=[tok(d)] + out_specs,
        out_shape=[jax.ShapeDtypeStruct((bsz, s, d), F32)] + out_shapes,
        compiler_params=_params("arbitrary", "arbitrary"), name="in_proj_res",
    )(x, moe_prev, gf_prev, *common)
    return outs[0], outs[1:]


SSD_STEP_CHUNKS = 4
assert SSD_STEP_CHUNKS % 2 == 0


def _ssd_kernel(z_ref, xbc_ref, dt_ref, cw_ref, cb_ref, dtb_ref, alog_ref, dskip_ref, ng_ref,
                expand_ref, o_ref, state_ref, ext_ref):
    L, G, N, P = SSD_CHUNK, SSD_GROUPS, SSD_STATE, SSD_HEAD_DIM
    R = SSD_HEADS // G
    GW = R * P
    first = pl.program_id(1) == 0

    @pl.when(first)
    def _():
        state_ref[...] = jnp.zeros_like(state_ref)
        ext_ref[L:2 * L, :] = jnp.zeros((L, SSD_CONV_DIM), BF16)

    row = lax.broadcasted_iota(jnp.int32, (L, L), 0)
    col = lax.broadcasted_iota(jnp.int32, (L, L), 1)
    causal = col <= row
    tril = jnp.where(causal, 1.0, 0.0).astype(BF16)
    a_row = -jnp.exp(alog_ref[...])
    t_i = lax.broadcasted_iota(jnp.int32, (L, 2 * L), 0)
    r_i = lax.broadcasted_iota(jnp.int32, (L, 2 * L), 1)
    shifts = []
    for half in range(2):
        per_tap = []
        for k in range(SSD_CONV - 1):
            v = t_i - (SSD_CONV - 1) + k
            src = jnp.where(v >= 0, half * L + v, (1 - half) * L + L + v)
            per_tap.append(jnp.where(r_i == src, 1.0, 0.0).astype(BF16))
        shifts.append(per_tap)

    for ck in range(SSD_STEP_CHUNKS):
        r0 = ck * L
        half = ck % 2
        cur = xbc_ref[0, r0:r0 + L, :]
        ext_ref[half * L:(half + 1) * L, :] = cur
        ext = ext_ref[...]
        conv = cb_ref[...] + cw_ref[SSD_CONV - 1:SSD_CONV, :] * cur.astype(F32)
        for k in range(SSD_CONV - 1):
            conv = conv + cw_ref[k:k + 1, :] * _dot(shifts[half][k], ext)
        xc = _silu(conv)
        xs = xc[:, :SSD_WIDTH]

        dt = _softplus(dt_ref[0, r0:r0 + L, :] + dtb_ref[...])
        a_cum = _dot_select_rows(tril, dt * a_row, 3)
        a_cum_t = a_cum.T
        a_last = a_cum[L - 1:L, :]
        stacked = jnp.concatenate(
            [dt, jnp.exp(a_cum), jnp.exp(a_last - a_cum),
             jnp.broadcast_to(jnp.exp(a_last), (8, LANES))], axis=0)
        wide = _dot_select_cols(stacked, expand_ref[...], 2)
        dt_w, ea_w, te_w, cd_w = wide[0:L], wide[L:2 * L], wide[2 * L:3 * L], wide[3 * L:3 * L + 1]
        xdt = xs * dt_w
        xdt_b = xdt.astype(BF16)
        xw_b = (xdt * te_w).astype(BF16)

        y_parts = []
        for g in range(G):
            bm = xc[:, SSD_WIDTH + g * N:SSD_WIDTH + (g + 1) * N]
            cm = xc[:, SSD_WIDTH + (G + g) * N:SSD_WIDTH + (G + g + 1) * N]
            bm_b, cm_b = bm.astype(BF16), cm.astype(BF16)
            cb = _dot_nt(cm_b, bm_b)
            st = state_ref[g]
            y_off = _dot(cm_b, st.astype(BF16)) * ea_w[:, g * GW:(g + 1) * GW]
            diag = []
            for r in range(R):
                h = g * R + r
                seg = a_cum[:, h:h + 1] - a_cum_t[h:h + 1, :]
                m = (cb * jnp.exp(jnp.where(causal, seg, -jnp.inf))).astype(BF16)
                diag.append(_dot(m, xdt_b[:, h * P:(h + 1) * P]))
            y_parts.append(jnp.concatenate(diag, axis=1) + y_off)
            state_ref[g] = st * cd_w[:, g * GW:(g + 1) * GW] + _dot(
                bm.T.astype(BF16), xw_b[:, g * GW:(g + 1) * GW])

        y = jnp.concatenate(y_parts, axis=1) + xs * dskip_ref[...]
        y = y * _silu(z_ref[0, r0:r0 + L, :].astype(F32))
        outs = []
        for g in range(G):
            yg = y[:, g * GW:(g + 1) * GW]
            outs.append(yg * lax.rsqrt(jnp.mean(yg * yg, axis=-1, keepdims=True) + EPS))
        o_ref[0, r0:r0 + L, :] = (jnp.concatenate(outs, axis=1) * ng_ref[...]).astype(o_ref.dtype)


def _ssd(z, xbc, dt, conv_w, conv_b, dt_bias, a_log, d_skip, norm_g):
    bsz, s, _ = z.shape
    ts = SSD_STEP_CHUNKS * SSD_CHUNK
    pad_h = lambda v: jnp.pad(v.astype(F32), (0, LANES - SSD_HEADS)).reshape(1, LANES)
    expand = jnp.repeat(jnp.eye(LANES, SSD_HEADS, dtype=BF16), SSD_HEAD_DIM, axis=1)
    tok = lambda w: pl.BlockSpec((1, ts, w), lambda b, i: (b, i, 0))
    full = lambda a: pl.BlockSpec(a.shape, lambda b, i: (0,) * a.ndim)
    params = (conv_w, conv_b.reshape(1, -1), pad_h(dt_bias), pad_h(a_log),
              jnp.repeat(d_skip, SSD_HEAD_DIM).reshape(1, -1), norm_g.reshape(1, -1), expand)
    return pl.pallas_call(
        _ssd_kernel, grid=(bsz, s // ts),
        in_specs=[tok(SSD_WIDTH), tok(SSD_CONV_DIM), tok(LANES)] + [full(p) for p in params],
        out_specs=tok(SSD_WIDTH),
        out_shape=jax.ShapeDtypeStruct((bsz, s, SSD_WIDTH), BF16),
        scratch_shapes=[pltpu.VMEM((SSD_GROUPS, SSD_STATE, SSD_WIDTH // SSD_GROUPS), F32),
                        pltpu.VMEM((2 * SSD_CHUNK, SSD_CONV_DIM), BF16)],
        compiler_params=_params("arbitrary", "arbitrary"), name="ssd_scan",
    )(z, xbc, dt, *params)


def _augment_v_t(v):
    vt = v.T
    ones = jnp.ones((VT_ROWS - V_PAD, v.shape[0]), F32)
    parts = []
    for h in range(v.shape[1] // V_PAD):
        parts += [vt[h * V_PAD:(h + 1) * V_PAD], ones]
    return jnp.concatenate(parts, axis=0)


def _rope_mix(v, cos, sin_hi, sin_lo):
    w = v.shape[1]
    half = MLA_ROPE // 2
    return v * cos + pltpu.roll(v, half, axis=1) * sin_hi + pltpu.roll(v, w - half, axis=1) * sin_lo


def _mla_up_proj(xin, gq_ref, wq_ref, gkv_ref, wkv_ref):
    cq = xin[:, :MLA_Q_LORA]
    ckv = xin[:, MLA_Q_LORA:MLA_Q_LORA + MLA_KV_LORA]
    kpe = xin[:, MLA_Q_LORA + MLA_KV_LORA:]
    cqn = cq * lax.rsqrt(jnp.mean(cq * cq, axis=-1, keepdims=True) + EPS) * gq_ref[...]
    ckvn = ckv * lax.rsqrt(jnp.mean(ckv * ckv, axis=-1, keepdims=True) + EPS) * gkv_ref[...]
    q = _dot(cqn.astype(BF16), wq_ref[...])
    kv = _dot(ckvn.astype(BF16), wkv_ref[...])
    return q, kv, kpe


def _mla_prep_math(q, kv, kpe, cos_ref, shi_ref, slo_ref, qgain_ref, kgain_ref, q_ref, k_ref, v_ref):
    H = MLA_HEADS
    cos, shi, slo = cos_ref[...], shi_ref[...], slo_ref[...]
    tile = lambda t: jnp.concatenate([t] * H, axis=1)
    q = _rope_mix(q, tile(cos), tile(shi), tile(slo))
    kpe = _rope_mix(kpe, cos, shi, slo)
    scale = MLA_QK ** -0.5
    for h in range(H):
        sl = slice(h * HEAD_PAD, (h + 1) * HEAD_PAD)
        qh = q[:, sl]
        qn = qh * lax.rsqrt(jnp.sum(qh * qh, axis=-1, keepdims=True) / MLA_QK + EPS)
        q_ref[0, :, sl] = (qn * (qgain_ref[...] * scale)).astype(q_ref.dtype)
        kh = kv[:, sl] + kpe
        kn = kh * lax.rsqrt(jnp.sum(kh * kh, axis=-1, keepdims=True) / MLA_QK + EPS)
        k_ref[0, :, sl] = (kn * kgain_ref[...]).astype(k_ref.dtype)
    v_ref[0] = _augment_v_t(kv[:, H * HEAD_PAD:]).astype(v_ref.dtype)


def _rope_tables(s):
    half = MLA_ROPE // 2
    inv = ROPE_BASE ** (-2.0 * jnp.arange(half, dtype=F32) / MLA_ROPE)
    ang = jnp.arange(s, dtype=F32)[:, None] * inv[None, :]
    cos, sin = jnp.cos(ang), jnp.sin(ang)
    zeros = lambda n: jnp.zeros((s, n), F32)
    cos_t = jnp.concatenate([jnp.ones((s, MLA_NOPE), F32), cos, cos, zeros(HEAD_PAD - MLA_QK)], axis=1)
    sin_hi = jnp.concatenate([zeros(MLA_NOPE + half), sin, zeros(HEAD_PAD - MLA_QK)], axis=1)
    sin_lo = jnp.concatenate([zeros(MLA_NOPE), -sin, zeros(half + HEAD_PAD - MLA_QK)], axis=1)
    return cos_t, sin_hi, sin_lo


def _pad_heads(w, heads, width):
    lead = w.shape[:-1]
    w = w.reshape(*lead, heads, width)
    w = jnp.pad(w, [(0, 0)] * len(lead) + [(0, 0), (0, HEAD_PAD - width)])
    return w.reshape(*lead, heads * HEAD_PAD)


def _mla_prep_params(s, gq, w_uq, gkv, w_ukv, q_gain, k_gain):
    H = MLA_HEADS
    wq = _pad_heads(w_uq, H, MLA_QK).astype(BF16)
    wkv = w_ukv.reshape(MLA_KV_LORA, H, MLA_NOPE + MLA_V)
    wk = _pad_heads(wkv[:, :, :MLA_NOPE].reshape(MLA_KV_LORA, H * MLA_NOPE), H, MLA_NOPE)
    wv = wkv[:, :, MLA_NOPE:].reshape(MLA_KV_LORA, H * MLA_V)
    wkv_p = jnp.concatenate([wk, wv], axis=1).astype(BF16)
    cos_t, sin_hi, sin_lo = _rope_tables(s)
    pad_g = lambda v: jnp.pad(v, (0, HEAD_PAD - MLA_QK)).reshape(1, HEAD_PAD)
    return (gq.reshape(1, -1), wq, gkv.reshape(1, -1), wkv_p, cos_t, sin_hi, sin_lo,
            pad_g(q_gain), pad_g(k_gain))


GATE_LANE0 = MOBA_HEAD_DIM


def _moba_prep_math(xin, qgain_ref, kgain_ref, q_ref, k_ref, v_ref, km_ref):
    H, W, B = MOBA_HEADS, MOBA_HEADS * HEAD_PAD, MOBA_BLOCK
    tm = xin.shape[0]
    blk0 = pl.program_id(1) * (tm // B)
    lane = lax.broadcasted_iota(jnp.int32, (1, HEAD_PAD), 1)
    scale = MOBA_HEAD_DIM ** -0.5
    for h in range(H):
        sl = slice(h * HEAD_PAD, (h + 1) * HEAD_PAD)
        qh = xin[:, sl]
        qn = qh * lax.rsqrt(jnp.sum(qh * qh, axis=-1, keepdims=True) / MOBA_HEAD_DIM + EPS)
        q_ref[0, :, sl] = (qn * (qgain_ref[...] * scale)).astype(q_ref.dtype)
        kh = xin[:, W + h * HEAD_PAD:W + (h + 1) * HEAD_PAD]
        kn = kh * lax.rsqrt(jnp.sum(kh * kh, axis=-1, keepdims=True) / MOBA_HEAD_DIM + EPS)
        kn = kn * kgain_ref[...]
        for j in range(tm // B):
            rows = slice(j * B, (j + 1) * B)
            tag = (lane == GATE_LANE0 + blk0 + j).astype(F32)
            km_ref[0, j, :, sl] = jnp.mean(kn[rows], axis=0, keepdims=True)
            k_ref[0, rows, sl] = (kn[rows] + tag).astype(k_ref.dtype)
    v_ref[0] = _augment_v_t(xin[:, 2 * W:]).astype(v_ref.dtype)


def _moba_prep_params(q_gain, k_gain):
    pad_g = lambda v: jnp.pad(v, (0, HEAD_PAD - MOBA_HEAD_DIM)).reshape(1, HEAD_PAD)
    return pad_g(q_gain), pad_g(k_gain)


def _attn_kernel(*refs, gated):
    if gated:
        q_ref, k_ref, vt_ref, b0_ref, b01_ref, kmr_ref, og_ref, o_ref, m_ref, acc_ref, qx_ref = refs
    else:
        q_ref, k_ref, vt_ref, b0_ref, b01_ref, og_ref, o_ref, m_ref, acc_ref, qx_ref = refs
    T = ATT_BLOCK
    H = ATT_HEADS
    j = pl.program_id(1)
    hsl = lambda h: slice(h * HEAD_PAD, (h + 1) * HEAD_PAD)

    def step(r, rows, qts_ref, bias_ref=None, first=False):
        sts = []
        for h in range(H):
            st = _dot(k_ref[0, pl.ds(r, rows), hsl(h)], qts_ref[h])
            sts.append(st if bias_ref is None else st + bias_ref[h])
        for h in range(H):
            vt = vt_ref[0, h * VT_ROWS:(h + 1) * VT_ROWS, pl.ds(r, rows)]
            mx = jnp.max(sts[h], axis=0, keepdims=True)
            if first:
                m_ref[h] = mx
                acc_ref[h] = _dot(vt, jnp.exp(sts[h] - mx).astype(BF16))
            else:
                m_old = m_ref[h]
                m_new = jnp.maximum(m_old, mx)
                m_ref[h] = m_new
                acc_ref[h] = (jnp.exp(m_old - m_new) * acc_ref[h]
                              + _dot(vt, jnp.exp(sts[h] - m_new).astype(BF16)))

    for h in range(H):
        qt = q_ref[0, :, hsl(h)].astype(F32).T
        qt_b = qt.astype(BF16)
        if not gated:
            qx_ref[h] = qt_b
        else:
            nbp = -(-(k_ref.shape[1] // T) // 8) * 8
            g = _dot(kmr_ref[0, h], qt_b)[GATE_LANE0:GATE_LANE0 + nbp]
            row = lax.broadcasted_iota(jnp.int32, (nbp, T), 0)
            g = jnp.where(row < j, g, -jnp.inf)
            chosen = jnp.zeros((nbp, T), jnp.bool_)
            for _ in range(MOBA_TOPK):
                mx = jnp.max(g, axis=0, keepdims=True)
                hit = (g == mx) & (mx > -jnp.inf)
                first = jnp.min(jnp.where(hit, row, nbp), axis=0, keepdims=True)
                pick = row == first
                chosen = chosen | pick
                g = jnp.where(pick, -jnp.inf, g)
            neg = jnp.where(chosen | (row == j), 0.0, MASK_NEG)
            qx_ref[h] = jnp.concatenate(
                [qt[:GATE_LANE0], qt[GATE_LANE0:GATE_LANE0 + nbp] + neg, qt[GATE_LANE0 + nbp:]],
                axis=0).astype(BF16)

    @pl.when(j == 0)
    def _():
        step(0, T, qx_ref, b0_ref, first=True)

    @pl.when(j >= 1)
    def _():
        step(pl.multiple_of((j - 1) * T, T), 2 * T, qx_ref, b01_ref, first=True)

    n_far = jnp.maximum(j - 1, 0)

    def pair_body(i, carry):
        step(pl.multiple_of(i * (2 * T), 2 * T), 2 * T, qx_ref)
        return carry

    lax.fori_loop(0, lax.shift_right_logical(n_far, 1), pair_body, 0)

    @pl.when((n_far & 1) == 1)
    def _():
        step(pl.multiple_of((n_far - 1) * T, T), T, qx_ref)

    outs = []
    for h in range(H):
        a = acc_ref[h]
        outs.append(a[:V_PAD] * (1.0 / a[V_PAD:V_PAD + 1]))
    y = jnp.concatenate(outs, axis=0).T
    y = y * lax.rsqrt(jnp.mean(y * y, axis=-1, keepdims=True) + EPS) * og_ref[...]
    o_ref[0] = y.astype(o_ref.dtype)


def _attention(q, k, v, bias0, out_gain, bias1=None, kmr=None):
    bsz, s, w = q.shape
    T = ATT_BLOCK
    gated = kmr is not None
    qspec = pl.BlockSpec((1, T, w), lambda b, i: (b, i, 0))
    seq = lambda a: pl.BlockSpec((1,) + a.shape[1:], lambda b, i: (b,) + (0,) * (a.ndim - 1))
    full = lambda a: pl.BlockSpec(a.shape, lambda b, i: (0,) * a.ndim)
    og = out_gain.reshape(1, -1)
    bias01 = jnp.concatenate([bias1 if gated else jnp.zeros_like(bias0), bias0], axis=1)
    if gated:
        args = (q, k, v, bias0, bias01, kmr, og)
        specs = [qspec, seq(k), seq(v), full(bias0), full(bias01), seq(kmr), full(og)]
    else:
        args = (q, k, v, bias0, bias01, og)
        specs = [qspec, seq(k), seq(v), full(bias0), full(bias01), full(og)]
    wo = ATT_HEADS * V_PAD
    scratch = [pltpu.VMEM((ATT_HEADS, 1, T), F32), pltpu.VMEM((ATT_HEADS, VT_ROWS, T), F32),
               pltpu.VMEM((ATT_HEADS, HEAD_PAD, T), BF16)]
    return pl.pallas_call(
        functools.partial(_attn_kernel, gated=gated), grid=(bsz, s // T),
        in_specs=specs, out_specs=pl.BlockSpec((1, T, wo), lambda b, i: (b, i, 0)),
        out_shape=jax.ShapeDtypeStruct((bsz, s, wo), BF16),
        scratch_shapes=scratch,
        compiler_params=_params("arbitrary", "arbitrary"),
        name="moba_attention" if gated else "mla_attention",
    )(*args)


def _rel_bucket(n):
    max_exact = REL_BUCKETS // 2
    nf = jnp.maximum(n, max_exact).astype(F32)
    large = max_exact + (jnp.log(nf / max_exact) / math.log(REL_MAX_DIST / max_exact)
                         * (REL_BUCKETS - max_exact)).astype(jnp.int32)
    large = jnp.minimum(large, REL_BUCKETS - 1)
    return jnp.where(n < max_exact, n, large)


def _moba_bias_tiles(rel_bias):
    T = ATT_BLOCK
    i = jnp.arange(T)[:, None]
    c = jnp.arange(T)[None, :]
    bias_t = rel_bias.T
    far = bias_t[:, REL_BUCKETS - 1][:, None, None]
    d0 = i - c

    def lookup(bucket):
        hit = bucket[None, :, :, None] == jnp.arange(REL_BUCKETS)[None, None, None, :]
        return jnp.sum(jnp.where(hit, bias_t[:, None, None, :], 0.0), axis=-1)

    b0 = jnp.where((d0 >= 0)[None], lookup(_rel_bucket(jnp.maximum(d0, 0))) - far, MASK_NEG)
    b1 = lookup(_rel_bucket(T + i - c)) - far
    return b0.astype(F32).transpose(0, 2, 1), b1.astype(F32).transpose(0, 2, 1)


def _causal_tile(heads):
    T = ATT_BLOCK
    i = jnp.arange(T)[:, None]
    c = jnp.arange(T)[None, :]
    return jnp.broadcast_to(jnp.where(c <= i, 0.0, MASK_NEG).astype(F32).T, (heads, T, T))


def _outproj_kernel(ys_ref, ya_ref, yb_ref, x_ref, w_ref, gm_ref, g_ref, sc_ref, sh_ref, rw_ref,
                    rb_ref, upper_ref, xo_ref, hf_ref, route_ref, counts_out_ref, count_ref):
    w0, w1 = SSD_WIDTH, SSD_WIDTH + MLA_HEADS * MLA_V
    y = (_dot(ys_ref[0], w_ref[0:w0, :]) + _dot(ya_ref[0], w_ref[w0:w1, :])
         + _dot(yb_ref[0], w_ref[w1:, :]))
    x = x_ref[0] + gm_ref[0] * y
    xo_ref[0] = x
    ms = jnp.mean(x * x, axis=-1, keepdims=True)
    hf = (x * lax.rsqrt(ms + EPS) * g_ref[...]) * (1.0 + sc_ref[0]) + sh_ref[0]
    hb = hf.astype(BF16)
    hf_ref[0] = hb
    logits_t = _dot(hb, rw_ref[...]).T
    score = _sigmoid(logits_t[0:N_EXPERTS, :])
    biased = score + rb_ref[...]
    E = EXPERTS_PER_GROUP
    gsum = []
    for g in range(N_EXPERT_GROUPS):
        r = [biased[g * E + e:g * E + e + 1, :] for e in range(E)]
        best = r[0] + r[1]
        for lo, hi in zip(PAIR_LO[1:], PAIR_HI[1:]):
            best = jnp.maximum(best, r[lo] + r[hi])
        gsum.append(best)
    gmax = functools.reduce(jnp.maximum, gsum)
    taken = jnp.zeros_like(gmax, dtype=jnp.bool_)
    gid = jnp.zeros_like(gmax)
    vb = [jnp.zeros_like(gmax) for _ in range(E)]
    for g in range(N_EXPERT_GROUPS):
        is_g = (gsum[g] == gmax) & jnp.logical_not(taken)
        taken = taken | is_g
        gid = jnp.where(is_g, float(g), gid)
        for e in range(E):
            vb[e] = jnp.where(is_g, biased[g * E + e:g * E + e + 1, :], vb[e])
    sel = []
    for e in range(E):
        rank = jnp.zeros_like(gmax)
        for o in range(E):
            if o < e:
                rank = rank + (vb[o] >= vb[e]).astype(F32)
            elif o > e:
                rank = rank + (vb[o] > vb[e]).astype(F32)
        sel.append(rank < 2.0)
    pidx = jnp.zeros_like(gmax)
    for p, (lo, hi) in enumerate(zip(PAIR_LO, PAIR_HI)):
        pidx = jnp.where(sel[lo] & sel[hi], float(p), pidx)
    bucket = gid * N_PAIRS + pidx
    @pl.when((pl.program_id(0) == 0) & (pl.program_id(1) == 0))
    def _():
        count_ref[...] = jnp.zeros_like(count_ref)

    tm = bucket.shape[1]
    brow = lax.broadcasted_iota(jnp.int32, (COUNT_ROWS, tm), 0).astype(F32)
    onehot = jnp.where(brow == bucket, 1.0, 0.0)
    prefix = _dot(onehot.astype(BF16), upper_ref[...])
    base = count_ref[...]
    rank = jnp.sum(onehot * (prefix - 1.0 + base[:, 0:1]), axis=0, keepdims=True)
    count_ref[...] = base + jnp.sum(onehot, axis=1, keepdims=True)
    counts_out_ref[...] = count_ref[...]
    rows = lax.broadcasted_iota(jnp.int32, (8, tm), 0)
    route_ref[0] = jnp.where(rows == 0, bucket, jnp.where(rows == 1, rank, 0.0))


def _outproj(y_ssd, y_mla, y_moba, x, w_out, gm, g, sc, sh, rw, router_bias, tm):
    bsz, s, d = x.shape
    tok = lambda w: pl.BlockSpec((1, tm, w), lambda b, i: (b, i, 0))
    per_b = pl.BlockSpec((1, 1, d), lambda b, i: (b, 0, 0))
    full = lambda a: pl.BlockSpec(a.shape, lambda b, i: (0,) * a.ndim)
    rb = router_bias.astype(F32).reshape(N_EXPERTS, 1)
    g2 = g.reshape(1, d)
    upper = (jnp.arange(tm)[:, None] <= jnp.arange(tm)[None, :]).astype(BF16)
    return pl.pallas_call(
        _outproj_kernel, grid=(bsz, s // tm),
        in_specs=[tok(y_ssd.shape[-1]), tok(y_mla.shape[-1]), tok(y_moba.shape[-1]), tok(d),
                  full(w_out), per_b, full(g2), per_b, per_b, full(rw), full(rb), full(upper)],
        out_specs=[tok(d), tok(d), pl.BlockSpec((1, 8, tm), lambda b, i: (b, 0, i)),
                   pl.BlockSpec((COUNT_ROWS, LANES), lambda b, i: (0, 0))],
        out_shape=[jax.ShapeDtypeStruct((bsz, s, d), F32), jax.ShapeDtypeStruct((bsz, s, d), BF16),
                   jax.ShapeDtypeStruct((bsz, 8, s), F32),
                   jax.ShapeDtypeStruct((COUNT_ROWS, LANES), F32)],
        scratch_shapes=[pltpu.VMEM((COUNT_ROWS, LANES), F32)],
        compiler_params=_params("arbitrary", "arbitrary"), name="out_proj_router",
    )(y_ssd, y_mla, y_moba, x, w_out, gm, g2, sc, sh, rw, rb, upper)


def _moe_kernel(ea_ref, eb_ref, nv_ref, x_ref, rw_ref, gua_ref, gub_ref, da_ref, db_ref, o_ref):
    i = pl.program_id(0)

    @pl.when(nv_ref[i] > 0)
    def _():
        xb = x_ref[...]
        logits = _dot(xb, rw_ref[...])
        lane = lax.broadcasted_iota(jnp.int32, logits.shape, 1)
        pick = lambda e: _sigmoid(jnp.sum(jnp.where(lane == e, logits, 0.0), axis=-1, keepdims=True))
        s_a, s_b = pick(ea_ref[i]), pick(eb_ref[i])
        tot = s_a + s_b
        out = None
        for gu_ref, d_ref, gate in ((gua_ref, da_ref, s_a / tot), (gub_ref, db_ref, s_b / tot)):
            gu = _dot(xb, gu_ref[0])
            hid = (_silu(gu[:, :D_EXPERT]) * gu[:, D_EXPERT:]).astype(BF16)
            y = _dot(hid, d_ref[0]) * gate
            out = y if out is None else out + y
        o_ref[...] = out.astype(o_ref.dtype)

    @pl.when(nv_ref[i] <= 0)
    def _():
        o_ref[...] = jnp.zeros_like(o_ref)


def _moe(xs, rw, tile_ea, tile_eb, tile_nv, w_gu, w_down):
    nslot, d = xs.shape
    nt = nslot // MOE_TILE
    grid_spec = pltpu.PrefetchScalarGridSpec(
        num_scalar_prefetch=3, grid=(nt,),
        in_specs=[pl.BlockSpec((MOE_TILE, d), lambda i, ea, eb, nv: (i, 0)),
                  pl.BlockSpec(rw.shape, lambda i, ea, eb, nv: (0, 0)),
                  pl.BlockSpec((1, d, 2 * D_EXPERT), lambda i, ea, eb, nv: (ea[i], 0, 0)),
                  pl.BlockSpec((1, d, 2 * D_EXPERT), lambda i, ea, eb, nv: (eb[i], 0, 0)),
                  pl.BlockSpec((1, D_EXPERT, d), lambda i, ea, eb, nv: (ea[i], 0, 0)),
                  pl.BlockSpec((1, D_EXPERT, d), lambda i, ea, eb, nv: (eb[i], 0, 0))],
        out_specs=pl.BlockSpec((MOE_TILE, d), lambda i, ea, eb, nv: (i, 0)))
    return pl.pallas_call(
        _moe_kernel, grid_spec=grid_spec,
        out_shape=jax.ShapeDtypeStruct((nslot, d), BF16),
        compiler_params=_params("arbitrary"), name="moe_ffn",
    )(tile_ea, tile_eb, tile_nv, xs, rw, w_gu, w_gu, w_down, w_down)


def _bucket_layout(bucket, rank, counts, n_tokens):
    nslot = n_tokens + N_BUCKETS * MOE_TILE
    nt = nslot // MOE_TILE
    padded = ((counts + MOE_TILE - 1) // MOE_TILE) * MOE_TILE
    pends = jnp.cumsum(padded)
    pstarts = pends - padded
    onehot = bucket[:, None] == jnp.arange(N_BUCKETS, dtype=jnp.int32)[None, :]
    token_slot = jnp.sum(jnp.where(onehot, pstarts[None, :], 0), axis=1).astype(jnp.int32) + rank
    tile_start = jnp.arange(nt, dtype=jnp.int32) * MOE_TILE
    tile_bucket = jnp.sum(tile_start[:, None] >= pends[None, :], axis=1).astype(jnp.int32)
    live = tile_bucket < N_BUCKETS
    tb = jnp.minimum(tile_bucket, N_BUCKETS - 1)
    tile_nv = jnp.where(live, jnp.clip(pstarts[tb] + counts[tb] - tile_start, 0, MOE_TILE), 0)
    npad = padded - counts
    pad_ends = jnp.cumsum(npad)
    q = jnp.arange(nslot - n_tokens, dtype=jnp.int32)
    qb = q[:, None] >= pad_ends[None, :]
    q_bucket = jnp.minimum(jnp.sum(qb, axis=1), N_BUCKETS - 1).astype(jnp.int32)
    sel = q_bucket[:, None] == jnp.arange(N_BUCKETS, dtype=jnp.int32)[None, :]
    pick = lambda v: jnp.sum(jnp.where(sel, v[None, :], 0), axis=1)
    pad_slot = pick(pstarts + counts) + q - pick(pad_ends - npad)
    pad_slot = jnp.where(q < pad_ends[-1], pad_slot, pends[-1] + q - pad_ends[-1]).astype(jnp.int32)
    keys = jnp.concatenate([token_slot, pad_slot])
    vals = jnp.concatenate([jnp.arange(n_tokens, dtype=jnp.int32), jnp.zeros((nslot - n_tokens,), jnp.int32)])
    _, slot_token = lax.sort_key_val(keys, vals)
    last_live = jnp.max(jnp.where(live, tb, 0))
    tb = jnp.where(live, tb, last_live)
    grp, pair = tb // N_PAIRS, tb % N_PAIRS
    tile_ea = grp * EXPERTS_PER_GROUP + jnp.asarray(PAIR_LO, jnp.int32)[pair]
    tile_eb = grp * EXPERTS_PER_GROUP + jnp.asarray(PAIR_HI, jnp.int32)[pair]
    return slot_token, token_slot, tile_ea.astype(jnp.int32), tile_eb.astype(jnp.int32), \
        tile_nv.astype(jnp.int32)


def _final_kernel(x_ref, moe_ref, gf_ref, o_ref):
    o_ref[0] = x_ref[0] + gf_ref[0] * moe_ref[0].astype(F32)


def _final_residual(x, moe, gf, tm):
    bsz, s, d = x.shape
    tok = pl.BlockSpec((1, tm, d), lambda b, i: (b, i, 0))
    return pl.pallas_call(
        _final_kernel, grid=(bsz, s // tm),
        in_specs=[tok, tok, pl.BlockSpec((1, 1, d), lambda b, i: (b, 0, 0))],
        out_specs=tok, out_shape=jax.ShapeDtypeStruct((bsz, s, d), F32),
        compiler_params=_params("arbitrary", "arbitrary"), name="final_residual",
    )(x, moe, gf)


def _pad_w_in(w_in):
    d = w_in.shape[0]
    sizes = (SSD_WIDTH, SSD_CONV_DIM, SSD_HEADS, MLA_Q_LORA, MLA_KV_LORA, MLA_ROPE,
             MOBA_HEADS * MOBA_HEAD_DIM, MOBA_HEADS * MOBA_HEAD_DIM, MOBA_HEADS * MOBA_HEAD_DIM)
    offs = np.cumsum((0,) + sizes)
    z, xbc, dtw, cq, ckv, krope, mq, mk, mv = [w_in[:, offs[i]:offs[i + 1]] for i in range(9)]
    zc = lambda n: jnp.zeros((d, n), w_in.dtype)
    kpe = jnp.concatenate([zc(MLA_NOPE), krope, zc(HEAD_PAD - MLA_QK)], axis=1)
    cols = [z, xbc, dtw, zc(LANES - SSD_HEADS), cq, ckv, kpe,
            _pad_heads(mq, MOBA_HEADS, MOBA_HEAD_DIM), _pad_heads(mk, MOBA_HEADS, MOBA_HEAD_DIM), mv]
    return jnp.concatenate(cols, axis=1).astype(BF16)


def kernel(x, c, ada_w, ada_b, norm_mix_g, norm_ffn_g, w_in, ssd_conv_w, ssd_conv_b, ssd_dt_bias,
           ssd_a_log, ssd_d, ssd_norm_g, mla_q_norm_g, mla_w_uq, mla_kv_norm_g, mla_w_ukv, mla_q_gain,
           mla_k_gain, mla_out_g, moba_q_gain, moba_k_gain, moba_out_g, rel_bias, w_out, router_w,
           router_bias, moe_w_gate, moe_w_up, moe_w_down):
    bsz, s, d = x.shape
    depth = ada_w.shape[0]
    tm = min(512, s)
    nb = s // MOBA_BLOCK
    mod = _modulation(c, ada_w, ada_b).reshape(depth, bsz, 6, 1, d)
    b0_moba, b1_moba = _moba_bias_tiles(rel_bias)
    b0_mla = _causal_tile(MLA_HEADS)
    rw = jnp.pad(router_w, ((0, 0), (0, LANES - N_EXPERTS))).astype(BF16)
    moe_prev, gf_prev = None, None
    for l in range(depth):
        sh_m, sc_m, g_m, sh_f, sc_f, g_f = [mod[l, :, i] for i in range(6)]
        mla_params = _mla_prep_params(s, mla_q_norm_g[l], mla_w_uq[l], mla_kv_norm_g[l], mla_w_ukv[l],
                                      mla_q_gain[l], mla_k_gain[l])
        moba_params = _moba_prep_params(moba_q_gain[l], moba_k_gain[l])
        x, (z, xbc, dt, q, k, v, mq, mk, mv, kmean) = _inproj(
            x, moe_prev, gf_prev, norm_mix_g[l], sc_m, sh_m, _pad_w_in(w_in[l]), mla_params,
            moba_params, tm)
        y_ssd = _ssd(z, xbc, dt, ssd_conv_w[l], ssd_conv_b[l], ssd_dt_bias[l], ssd_a_log[l],
                     ssd_d[l], ssd_norm_g[l])
        y_mla = _attention(q, k, v, b0_mla, mla_out_g[l])
        km = kmean.reshape(bsz, nb, MOBA_HEADS, HEAD_PAD).transpose(0, 2, 1, 3)
        kmr = jnp.pad(km, ((0, 0), (0, 0), (GATE_LANE0, HEAD_PAD - GATE_LANE0 - nb), (0, 0))).astype(BF16)
        y_moba = _attention(mq, mk, mv, b0_moba, moba_out_g[l], b1_moba, kmr)
        x, hf, route, counts = _outproj(y_ssd, y_mla, y_moba, x, w_out[l].astype(BF16), g_m,
                                        norm_ffn_g[l], sc_f, sh_f, rw, router_bias, tm)
        bucket = route[:, 0, :].reshape(-1).astype(jnp.int32)
        rank = route[:, 1, :].reshape(-1).astype(jnp.int32)
        slot_token, token_slot, tile_ea, tile_eb, tile_nv = _bucket_layout(
            bucket, rank, counts[:N_BUCKETS, 0].astype(jnp.int32), bsz * s)
        xs = hf.reshape(-1, d)[slot_token]
        w_gu = jnp.concatenate([moe_w_gate[l], moe_w_up[l]], axis=-1).astype(BF16)
        ys = _moe(xs, rw, tile_ea, tile_eb, tile_nv, w_gu, moe_w_down[l].astype(BF16))
        moe_prev = ys[token_slot].reshape(bsz, s, d)
        gf_prev = g_f
    return _final_residual(x, moe_prev, gf_prev, tm)
```
